```python
import math
import jax, jax.numpy as jnp
from jax import lax
import numpy as np

D_MODEL = 1024
BATCH = 2
SEQ = 8192
DEPTH = 1

HEAD_DIM = 128
N_HEADS = D_MODEL // HEAD_DIM
GDN_HEADS = N_HEADS // 2
NSA_HEADS = N_HEADS - GDN_HEADS
GDN_WIDTH = GDN_HEADS * HEAD_DIM
NSA_WIDTH = NSA_HEADS * HEAD_DIM
GDN_CONV = 4
GDN_CHUNK = 64
CMP_BLOCK = 32
CMP_STRIDE = 16
SEL_BLOCK = 64
N_SELECT = 16
WINDOW = 512
Q_BLOCK = 128
D_FF = 2816
FFN_CONV = 3
N_GATES = 3
IN_SPLITS = (GDN_WIDTH, GDN_WIDTH, GDN_WIDTH, GDN_WIDTH, GDN_HEADS, GDN_HEADS,
             NSA_WIDTH, HEAD_DIM, HEAD_DIM, HEAD_DIM, HEAD_DIM, HEAD_DIM, HEAD_DIM,
             N_GATES * NSA_HEADS)
N_IN = 4 * GDN_WIDTH + 2 * GDN_HEADS + NSA_WIDTH + 6 * HEAD_DIM + N_GATES * NSA_HEADS
EPS = 1e-6
NEG_INF = -1e30
FORCE_BONUS = 1e4

kernel_name = "hybrid_gdn_nsa_convffn_block"


def rmsnorm(x, w):
    xf = x.astype(jnp.float32)
    y = xf * lax.rsqrt(jnp.mean(xf * xf, axis=-1, keepdims=True) + EPS)
    return (y * w.astype(jnp.float32)).astype(x.dtype)


def l2norm(x):
    xf = x.astype(jnp.float32)
    return xf * lax.rsqrt(jnp.sum(xf * xf, axis=-1, keepdims=True) + EPS)


def causal_dwconv(x, w, b=None):
    k = w.shape[0]
    y = lax.conv_general_dilated(x, w[:, None, :].astype(x.dtype), window_strides=(1,),
                                 padding=[(k - 1, 0)], dimension_numbers=('NWC', 'WIO', 'NWC'),
                                 feature_group_count=x.shape[-1])
    if b is not None:
        y = y + b
    return y


def masked_softmax(s, mask):
    p = jax.nn.softmax(jnp.where(mask, s, NEG_INF), axis=-1)
    return jnp.where(mask, p, 0.0)


def gated_delta_chunked(q, k, v, g, beta):
    B, S, H, dk = q.shape
    dv = v.shape[-1]
    n = S // GDN_CHUNK
    f32 = jnp.float32

    def to_chunks(t):
        t = t.astype(f32).reshape((B, n, GDN_CHUNK) + t.shape[2:])
        return jnp.moveaxis(t, 3, 1)

    q, k, v, g, beta = (to_chunks(t) for t in (q, k, v, g, beta))
    q = q * (dk ** -0.5)
    gc = jnp.cumsum(g, axis=-1)
    idx = jnp.arange(GDN_CHUNK)
    incl = idx[:, None] >= idx[None, :]
    strict = idx[:, None] > idx[None, :]
    decay = jnp.exp(jnp.where(incl, gc[..., :, None] - gc[..., None, :], NEG_INF))
    k_beta = k * beta[..., None]
    a_mat = jnp.where(strict, jnp.einsum('bhncd,bhnsd->bhncs', k_beta, k) * decay, 0.0)
    eye = jnp.eye(GDN_CHUNK, dtype=f32)
    t_mat = lax.linalg.triangular_solve(eye + a_mat, jnp.broadcast_to(eye, a_mat.shape),
                                        left_side=True, lower=True, unit_diagonal=True)
    u = t_mat @ (v * beta[..., None])
    w = t_mat @ (k_beta * jnp.exp(gc)[..., None])
    attn_intra = jnp.where(incl, jnp.einsum('bhncd,bhnsd->bhncs', q, k) * decay, 0.0)
    q_dec = q * jnp.exp(gc)[..., None]
    k_dec = k * jnp.exp(gc[..., -1:] - gc)[..., None]
    g_last = jnp.exp(gc[..., -1])

    def step(state, inp):
        u_i, w_i, qd_i, kd_i, a_i, gl_i = inp
        v_new = u_i - w_i @ state
        o_i = qd_i @ state + a_i @ v_new
        state = state * gl_i[..., None, None] + jnp.swapaxes(kd_i, -1, -2) @ v_new
        return state, o_i

    xs = tuple(jnp.moveaxis(t, 2, 0) for t in (u, w, q_dec, k_dec, attn_intra, g_last))
    state0 = jnp.zeros((B, H, dk, dv), f32)
    _, o = lax.scan(step, state0, xs)
    return jnp.transpose(o, (1, 0, 3, 2, 4)).reshape(B, S, H, dv)


def nsa_compress(kv, pos, w1, w2):
    B, S, D = kv.shape
    n_cmp = (S - CMP_BLOCK) // CMP_STRIDE + 1
    idx = jnp.arange(n_cmp)[:, None] * CMP_STRIDE + jnp.arange(CMP_BLOCK)[None, :]
    blocks = kv[:, idx] + pos
    hid = jax.nn.silu(blocks.reshape(B, n_cmp, CMP_BLOCK * D) @ w1)
    return hid @ w2


def nsa_attention(q, k_cmp, v_cmp, k_slc, v_slc, k_win, v_win, gates):
    f32 = jnp.float32
    q, k_cmp, v_cmp, k_slc, v_slc, k_win, v_win, gates = (
        t.astype(f32) for t in (q, k_cmp, v_cmp, k_slc, v_slc, k_win, v_win, gates))
    B, S, H, D = q.shape
    scale = D ** -0.5
    n_cmp = k_cmp.shape[1]
    n_sel = S // SEL_BLOCK
    n_pick = min(N_SELECT, n_sel)
    cs = jnp.arange(n_cmp) * CMP_STRIDE
    ss = jnp.arange(n_sel) * SEL_BLOCK
    ov = jnp.minimum(cs[:, None] + CMP_BLOCK, ss[None, :] + SEL_BLOCK) - jnp.maximum(cs[:, None], ss[None, :])
    overlap = jnp.clip(ov, 0, None).astype(f32) / CMP_BLOCK
    cmp_end = cs + CMP_BLOCK - 1
    k_blocks = k_slc.reshape(B, n_sel, SEL_BLOCK, D)
    v_blocks = v_slc.reshape(B, n_sel, SEL_BLOCK, D)
    k_win_pad = jnp.pad(k_win, ((0, 0), (WINDOW, 0), (0, 0)))
    v_win_pad = jnp.pad(v_win, ((0, 0), (WINDOW, 0), (0, 0)))
    blk = jnp.arange(n_sel)
    gather = jax.vmap(lambda kb, ib: kb[ib])

    def block(qi):
        t0 = qi * Q_BLOCK
        qb = lax.dynamic_slice_in_dim(q, t0, Q_BLOCK, axis=1)
        gb = lax.dynamic_slice_in_dim(gates, t0, Q_BLOCK, axis=1)
        t = t0 + jnp.arange(Q_BLOCK)
        mask_c = cmp_end[None, :] <= t[:, None]
        s_c = jnp.einsum('bqhd,bkd->bqhk', qb, k_cmp) * scale
        p_c = masked_softmax(s_c, mask_c[None, :, None, :])
        o_c = jnp.einsum('bqhk,bkd->bqhd', p_c, v_cmp)
        imp = jnp.einsum('bqhk,kj->bqj', p_c, overlap)
        cur = t // SEL_BLOCK
        valid = blk[None, :] <= cur[:, None]
        forced = (blk[None, :] == 0) | (blk[None, :] == cur[:, None]) | (blk[None, :] == cur[:, None] - 1)
        imp = jnp.where(valid[None], imp + jnp.where(forced, FORCE_BONUS, 0.0)[None], NEG_INF)
        _, sel = lax.top_k(imp, n_pick)
        ks = gather(k_blocks, sel).reshape(B, Q_BLOCK, n_pick * SEL_BLOCK, D)
        vs = gather(v_blocks, sel).reshape(B, Q_BLOCK, n_pick * SEL_BLOCK, D)
        pos_s = (sel[..., None] * SEL_BLOCK + jnp.arange(SEL_BLOCK)).reshape(B, Q_BLOCK, n_pick * SEL_BLOCK)
        mask_s = pos_s <= t[None, :, None]
        s_s = jnp.einsum('bqhd,bqkd->bqhk', qb, ks) * scale
        p_s = masked_softmax(s_s, mask_s[:, :, None, :])
        o_s = jnp.einsum('bqhk,bqkd->bqhd', p_s, vs)
        kw = lax.dynamic_slice_in_dim(k_win_pad, t0, Q_BLOCK + WINDOW, axis=1)
        vw = lax.dynamic_slice_in_dim(v_win_pad, t0, Q_BLOCK + WINDOW, axis=1)
        pos_w = t0 - WINDOW + jnp.arange(Q_BLOCK + WINDOW)
        mask_w = (pos_w[None, :] <= t[:, None]) & (pos_w[None, :] > t[:, None] - WINDOW) & (pos_w[None, :] >= 0)
        s_w = jnp.einsum('bqhd,bkd->bqhk', qb, kw) * scale
        p_w = masked_softmax(s_w, mask_w[None, :, None, :])
        o_w = jnp.einsum('bqhk,bkd->bqhd', p_w, vw)
        return gb[..., 0:1] * o_c + gb[..., 1:2] * o_s + gb[..., 2:3] * o_w

    out = lax.map(block, jnp.arange(S // Q_BLOCK))
    return jnp.moveaxis(out, 0, 1).reshape(B, S, H, D)


def hybrid_layer(x, c, ada_w, ada_b, norm1_w, w_in, gdn_conv_w, gdn_A_log, gdn_dt_bias,
                 gdn_out_norm_w, nsa_q_norm_w, nsa_k_norm_cmp, nsa_k_norm_slc, nsa_k_norm_win,
                 cmp_k_pos, cmp_k_w1, cmp_k_w2, cmp_v_pos, cmp_v_w1, cmp_v_w2, w_out,
                 norm2_w, ffn_w_up, ffn_conv_w, ffn_conv_b, ffn_w_down):
    f32 = jnp.float32
    B, S, _ = x.shape
    mod = jax.nn.silu(c) @ ada_w + ada_b
    shift1, scale1, gate1, shift2, scale2, gate2 = jnp.split(mod[:, None, :], 6, axis=-1)

    h = rmsnorm(x, norm1_w) * (1 + scale1) + shift1
    proj = h @ w_in
    offsets = np.cumsum(IN_SPLITS)[:-1].tolist()
    (gq, gk, gv, gz, ga, gbeta, nq, kc, vc, ksl, vsl, kwn, vwn, ng) = jnp.split(proj, offsets, axis=-1)

    qkv = jax.nn.silu(causal_dwconv(jnp.concatenate([gq, gk, gv], axis=-1), gdn_conv_w))
    gq, gk, gv = jnp.split(qkv, 3, axis=-1)
    hd = (B, S, GDN_HEADS, HEAD_DIM)
    q_g = l2norm(gq.reshape(hd))
    k_g = l2norm(gk.reshape(hd))
    v_g = gv.reshape(hd)
    beta = jax.nn.sigmoid(gbeta.astype(f32))
    log_decay = -jnp.exp(gdn_A_log.astype(f32)) * jax.nn.softplus(ga.astype(f32) + gdn_dt_bias.astype(f32))
    o_g = gated_delta_chunked(q_g, k_g, v_g, log_decay, beta)
    o_g = rmsnorm(o_g, gdn_out_norm_w) * jax.nn.silu(gz.reshape(hd).astype(f32))

    q_n = rmsnorm(nq.reshape(B, S, NSA_HEADS, HEAD_DIM), nsa_q_norm_w)
    k_cmp = rmsnorm(nsa_compress(kc, cmp_k_pos, cmp_k_w1, cmp_k_w2), nsa_k_norm_cmp)
    v_cmp = nsa_compress(vc, cmp_v_pos, cmp_v_w1, cmp_v_w2)
    k_slc = rmsnorm(ksl, nsa_k_norm_slc)
    k_win = rmsnorm(kwn, nsa_k_norm_win)
    gates = jax.nn.sigmoid(ng.reshape(B, S, NSA_HEADS, N_GATES))
    o_n = nsa_attention(q_n, k_cmp, v_cmp, k_slc, vsl, k_win, vwn, gates)

    mixed = jnp.concatenate([o_g.reshape(B, S, GDN_WIDTH), o_n.reshape(B, S, NSA_WIDTH)], axis=-1)
    x = x + gate1 * (mixed.astype(x.dtype) @ w_out)

    h2 = rmsnorm(x, norm2_w) * (1 + scale2) + shift2
    up = causal_dwconv(h2 @ ffn_w_up, ffn_conv_w, ffn_conv_b)
    a, b = jnp.split(up, 2, axis=-1)
    x = x + gate2 * ((jax.nn.silu(a) * b) @ ffn_w_down)
    return x


def setup_inputs(seed: int = 0) -> dict:
    key = jax.random.key(seed)
    ks = jax.random.split(key, 32)
    f32 = jnp.float32
    L = DEPTH

    def nrm(k, shape, s):
        return jax.random.normal(k, shape, f32) * s

    def gain(k, n):
        return 1.0 + 0.02 * jax.random.normal(k, (L, n), f32)

    dt = jnp.exp(jax.random.uniform(ks[8], (L, GDN_HEADS), f32, math.log(1e-3), math.log(1e-1)))
    return {
        'x': nrm(ks[0], (BATCH, SEQ, D_MODEL), 1.0),
        'c': nrm(ks[1], (BATCH, D_MODEL), 1.0),
        'ada_w': nrm(ks[2], (L, D_MODEL, 6 * D_MODEL), 0.5 * D_MODEL ** -0.5),
        'ada_b': nrm(ks[3], (L, 6 * D_MODEL), 0.02),
        'norm1_w': gain(ks[4], D_MODEL),
        'w_in': nrm(ks[5], (L, D_MODEL, N_IN), D_MODEL ** -0.5),
        'gdn_conv_w': nrm(ks[6], (L, GDN_CONV, 3 * GDN_WIDTH), GDN_CONV ** -0.5),
        'gdn_A_log': jnp.log(jax.random.uniform(ks[7], (L, GDN_HEADS), f32, 1.0, 16.0)),
        'gdn_dt_bias': dt + jnp.log(-jnp.expm1(-dt)),
        'gdn_out_norm_w': gain(ks[9], HEAD_DIM),
        'nsa_q_norm_w': gain(ks[10], HEAD_DIM),
        'nsa_k_norm_cmp': gain(ks[11], HEAD_DIM),
        'nsa_k_norm_slc': gain(ks[12], HEAD_DIM),
        'nsa_k_norm_win': gain(ks[13], HEAD_DIM),
        'cmp_k_pos': nrm(ks[14], (L, CMP_BLOCK, HEAD_DIM), 0.02),
        'cmp_k_w1': nrm(ks[15], (L, CMP_BLOCK * HEAD_DIM, HEAD_DIM), (CMP_BLOCK * HEAD_DIM) ** -0.5),
        'cmp_k_w2': nrm(ks[16], (L, HEAD_DIM, HEAD_DIM), HEAD_DIM ** -0.5),
        'cmp_v_pos': nrm(ks[17], (L, CMP_BLOCK, HEAD_DIM), 0.02),
        'cmp_v_w1': nrm(ks[18], (L, CMP_BLOCK * HEAD_DIM, HEAD_DIM), (CMP_BLOCK * HEAD_DIM) ** -0.5),
        'cmp_v_w2': nrm(ks[19], (L, HEAD_DIM, HEAD_DIM), HEAD_DIM ** -0.5),
        'w_out': nrm(ks[20], (L, D_MODEL, D_MODEL), D_MODEL ** -0.5),
        'norm2_w': gain(ks[21], D_MODEL),
        'ffn_w_up': nrm(ks[22], (L, D_MODEL, 2 * D_FF), D_MODEL ** -0.5),
        'ffn_conv_w': nrm(ks[23], (L, FFN_CONV, 2 * D_FF), FFN_CONV ** -0.5),
        'ffn_conv_b': nrm(ks[24], (L, 2 * D_FF), 0.02),
        'ffn_w_down': nrm(ks[25], (L, D_FF, D_MODEL), D_FF ** -0.5),
    }


def reference(x, c, ada_w, ada_b, norm1_w, w_in, gdn_conv_w, gdn_A_log, gdn_dt_bias,
              gdn_out_norm_w, nsa_q_norm_w, nsa_k_norm_cmp, nsa_k_norm_slc, nsa_k_norm_win,
              cmp_k_pos, cmp_k_w1, cmp_k_w2, cmp_v_pos, cmp_v_w1, cmp_v_w2, w_out,
              norm2_w, ffn_w_up, ffn_conv_w, ffn_conv_b, ffn_w_down):
    for l in range(DEPTH):
        x = hybrid_layer(x, c, ada_w[l], ada_b[l], norm1_w[l], w_in[l], gdn_conv_w[l], gdn_A_log[l],
                         gdn_dt_bias[l], gdn_out_norm_w[l], nsa_q_norm_w[l], nsa_k_norm_cmp[l],
                         nsa_k_norm_slc[l], nsa_k_norm_win[l], cmp_k_pos[l], cmp_k_w1[l], cmp_k_w2[l],
                         cmp_v_pos[l], cmp_v_w1[l], cmp_v_w2[l], w_out[l], norm2_w[l], ffn_w_up[l],
                         ffn_conv_w[l], ffn_conv_b[l], ffn_w_down[l])
    return x
```

```python
import functools

import numpy as np
import jax
import jax.numpy as jnp
from jax import lax
from jax.experimental import pallas as pl
from jax.experimental.pallas import tpu as pltpu

F32 = jnp.float32
BF16 = jnp.bfloat16

HEAD_DIM = 128
GDN_HEADS = 4
NSA_HEADS = 4
GDN_WIDTH = GDN_HEADS * HEAD_DIM
NSA_WIDTH = NSA_HEADS * HEAD_DIM
GDN_CONV = 4
GDN_CHUNK = 64
CMP_BLOCK = 32
CMP_STRIDE = 16
SEL_BLOCK = 64
N_SELECT = 16
WINDOW = 512
FFN_CONV = 3
N_GATES = 3
EPS = 1e-6
NEG_INF = -1e30
BIG = 3e38

LANES = 128
SUBLANES = 8
VMEM_LIMIT = 56 * 1024 * 1024

COL_GQ, COL_GK, COL_GV, COL_GZ = 0, 512, 1024, 1536
COL_NQ = 2048
COL_KC, COL_VC, COL_KSL, COL_VSL, COL_KWN, COL_VWN = 2560, 2688, 2816, 2944, 3072, 3200
COL_SMALL = 3328
N_PROJ = 3456
LANE_A, LANE_BETA, LANE_GATE = 0, 4, 8


def _params(sem):
    return pltpu.CompilerParams(dimension_semantics=sem, vmem_limit_bytes=VMEM_LIMIT)


def _dot(a, b):
    return jnp.dot(a.astype(BF16), b.astype(BF16), preferred_element_type=F32)


def _dot_nt(a, b):
    return lax.dot_general(a.astype(BF16), b.astype(BF16), (((1,), (1,)), ((), ())),
                           preferred_element_type=F32)


def _dot_tn(a, b):
    return lax.dot_general(a.astype(BF16), b.astype(BF16), (((0,), (0,)), ((), ())),
                           preferred_element_type=F32)


def _split2(a):
    hi = a.astype(BF16)
    lo = (a - hi.astype(F32)).astype(BF16)
    return hi, lo


def _split3(a):
    h1 = a.astype(BF16)
    r = a - h1.astype(F32)
    h2 = r.astype(BF16)
    h3 = (r - h2.astype(F32)).astype(BF16)
    return h1, h2, h3


def _dot_hp(a, b):
    ah, al = _split2(a)
    bh, bl = _split2(b)
    d = functools.partial(jnp.dot, preferred_element_type=F32)
    return d(ah, bh) + (d(ah, bl) + d(al, bh))


def _dot_exact_lhs(a_exact, b):
    a = a_exact.astype(BF16)
    b1, b2, b3 = _split3(b)
    d = functools.partial(jnp.dot, preferred_element_type=F32)
    return d(a, b1) + (d(a, b2) + d(a, b3))


def _sigmoid(x):
    return 1.0 / (1.0 + jnp.exp(-x))


def _silu(x):
    return x * _sigmoid(x)


def _softplus(x):
    return jnp.maximum(x, 0.0) + jnp.log1p(jnp.exp(-jnp.abs(x)))


def _ada_kernel(c_ref, w_ref, b_ref, o_ref):
    c = c_ref[...]
    o_ref[...] = _dot_hp(_silu(c), w_ref[...]) + b_ref[...]


def _ada(c, ada_w, ada_b):
    bsz, d = c.shape
    n = ada_w.shape[1]
    tn = 1024
    return pl.pallas_call(
        _ada_kernel,
        grid=(n // tn,),
        in_specs=[pl.BlockSpec((bsz, d), lambda j: (0, 0)),
                  pl.BlockSpec((d, tn), lambda j: (0, j)),
                  pl.BlockSpec((1, tn), lambda j: (0, j))],
        out_specs=pl.BlockSpec((bsz, tn), lambda j: (0, j)),
        out_shape=jax.ShapeDtypeStruct((bsz, n), F32),
        compiler_params=_params(("arbitrary",)),
        name="ada",
    )(c, ada_w, ada_b.reshape(1, n))


def _rms_mod(x, nw, scale, shift):
    y = x * lax.rsqrt(jnp.mean(x * x, axis=-1, keepdims=True) + EPS) * nw
    return y * (1.0 + scale) + shift


def _inproj_kernel(x_ref, mod_ref, nw_ref, w_ref, o_ref, *, tn):
    m = mod_ref[0]
    h = _rms_mod(x_ref[...], nw_ref[...], m[1:2], m[0:1]).astype(BF16)
    for j in range(o_ref.shape[1] // tn):
        o_ref[:, j * tn:(j + 1) * tn] = jnp.dot(h, w_ref[:, j * tn:(j + 1) * tn],
                                                 preferred_element_type=F32)


def _inproj(x2, mod3, norm1_w, w_in_p, seq):
    t, d = x2.shape
    tm = 512
    per_b = seq // tm
    return pl.pallas_call(
        functools.partial(_inproj_kernel, tn=1152),
        grid=(t // tm,),
        in_specs=[pl.BlockSpec((tm, d), lambda i: (i, 0)),
                  pl.BlockSpec((1, 6, d), lambda i: (i // per_b, 0, 0)),
                  pl.BlockSpec((1, d), lambda i: (0, 0)),
                  pl.BlockSpec((d, N_PROJ), lambda i: (0, 0))],
        out_specs=pl.BlockSpec((tm, N_PROJ), lambda i: (i, 0)),
        out_shape=jax.ShapeDtypeStruct((t, N_PROJ), F32),
        compiler_params=_params(("arbitrary",)),
        name="inproj",
    )(x2, mod3, norm1_w.reshape(1, d), w_in_p)


def _shift_rows(halo, x, k):
    xe = jnp.concatenate([halo, x], axis=0)
    return pltpu.roll(xe, k, axis=0)[halo.shape[0]:]


def _gdn_kernel(q_ref, k_ref, v_ref, z_ref, sm_ref, cw_ref, alog_ref, dtb_ref, onw_ref, o_ref,
                halo_ref, state_ref, *, tile):
    c_len = GDN_CHUNK
    n_chunks = tile // c_len

    @pl.when(pl.program_id(1) == 0)
    def _():
        halo_ref[...] = jnp.zeros_like(halo_ref)
        state_ref[...] = jnp.zeros_like(state_ref)

    def conv_silu(x_ref, idx):
        x = x_ref[...]
        halo = halo_ref[idx]
        cw = cw_ref[:, idx * GDN_WIDTH:(idx + 1) * GDN_WIDTH]
        y = cw[GDN_CONV - 1:GDN_CONV] * x
        for j in range(GDN_CONV - 1):
            y = y + cw[j:j + 1] * _shift_rows(halo, x, GDN_CONV - 1 - j)
        halo_ref[idx] = x[tile - SUBLANES:]
        return _silu(y)

    qc = conv_silu(q_ref, 0)
    kc = conv_silu(k_ref, 1)
    vc = conv_silu(v_ref, 2)

    sm = sm_ref[...]
    g_all = -jnp.exp(alog_ref[...]) * _softplus(sm + dtb_ref[...])
    beta_all = _sigmoid(sm)

    ri = lax.broadcasted_iota(jnp.int32, (tile, tile), 0)
    ci = lax.broadcasted_iota(jnp.int32, (tile, tile), 1)
    lblk = jnp.where((ri // c_len == ci // c_len) & (ci <= ri), 1.0, 0.0)
    gc_all = _dot_exact_lhs(lblk, g_all)
    gc_t = gc_all.T

    r64 = lax.broadcasted_iota(jnp.int32, (c_len, c_len), 0)
    c64 = lax.broadcasted_iota(jnp.int32, (c_len, c_len), 1)
    incl = r64 >= c64
    strict = r64 > c64
    eye = jnp.where(r64 == c64, 1.0, 0.0)

    for h in range(GDN_HEADS):
        hl = slice(h * HEAD_DIM, (h + 1) * HEAD_DIM)
        qh = qc[:, hl]
        kh = kc[:, hl]
        qh = qh * lax.rsqrt(jnp.sum(qh * qh, axis=-1, keepdims=True) + EPS) * (HEAD_DIM ** -0.5)
        kh = kh * lax.rsqrt(jnp.sum(kh * kh, axis=-1, keepdims=True) + EPS)
        vh = vc[:, hl]
        o_parts = []
        for c in range(n_chunks):
            rows = slice(c * c_len, (c + 1) * c_len)
            q = qh[rows]
            k = kh[rows]
            v = vh[rows]
            gcol = gc_all[rows, LANE_A + h:LANE_A + h + 1]
            grow = gc_t[LANE_A + h:LANE_A + h + 1, rows]
            glast = gc_t[LANE_A + h:LANE_A + h + 1, (c + 1) * c_len - 1:(c + 1) * c_len]
            bcol = beta_all[rows, LANE_BETA + h:LANE_BETA + h + 1]
            decay = jnp.exp(jnp.where(incl, gcol - grow, NEG_INF))
            k_beta = k * bcol
            a_mat = jnp.where(strict, _dot_nt(k_beta, k) * decay, 0.0)
            xp = -a_mat
            t_mat = eye + xp
            for _ in range(5):
                xp = _dot_hp(xp, xp)
                t_mat = t_mat + _dot_hp(t_mat, xp)
            eg = jnp.exp(gcol)
            u = _dot(t_mat, v * bcol)
            w = _dot(t_mat, k_beta * eg)
            attn = jnp.where(incl, _dot_nt(q, k) * decay, 0.0)
            q_dec = q * eg
            k_dec = k * jnp.exp(glast - gcol)
            state = state_ref[h]
            v_new = u - _dot(w, state)
            o_parts.append(_dot(q_dec, state) + _dot(attn, v_new))
            state_ref[h] = state * jnp.exp(glast) + _dot_tn(k_dec, v_new)
        o = jnp.concatenate(o_parts, axis=0)
        o = o * lax.rsqrt(jnp.mean(o * o, axis=-1, keepdims=True) + EPS) * onw_ref[...]
        o_ref[:, hl] = o * _silu(z_ref[:, hl])


def _gdn(proj, gdn_conv_w, alog_p, dtb_p, out_norm_w, bsz, seq):
    tile = 256
    per_b = seq // tile
    row = lambda b, s: b * per_b + s
    wide = GDN_WIDTH // LANES * LANES
    return pl.pallas_call(
        functools.partial(_gdn_kernel, tile=tile),
        grid=(bsz, per_b),
        in_specs=[pl.BlockSpec((tile, wide), lambda b, s: (row(b, s), COL_GQ // wide)),
                  pl.BlockSpec((tile, wide), lambda b, s: (row(b, s), COL_GK // wide)),
                  pl.BlockSpec((tile, wide), lambda b, s: (row(b, s), COL_GV // wide)),
                  pl.BlockSpec((tile, wide), lambda b, s: (row(b, s), COL_GZ // wide)),
                  pl.BlockSpec((tile, LANES), lambda b, s: (row(b, s), COL_SMALL // LANES)),
                  pl.BlockSpec((GDN_CONV, 3 * GDN_WIDTH), lambda b, s: (0, 0)),
                  pl.BlockSpec((1, LANES), lambda b, s: (0, 0)),
                  pl.BlockSpec((1, LANES), lambda b, s: (0, 0)),
                  pl.BlockSpec((1, HEAD_DIM), lambda b, s: (0, 0))],
        out_specs=pl.BlockSpec((tile, GDN_WIDTH), lambda b, s: (row(b, s), 0)),
        out_shape=jax.ShapeDtypeStruct((bsz * seq, GDN_WIDTH), F32),
        scratch_shapes=[pltpu.VMEM((3, SUBLANES, GDN_WIDTH), F32),
                        pltpu.VMEM((GDN_HEADS, HEAD_DIM, HEAD_DIM), F32)],
        compiler_params=_params(("arbitrary", "arbitrary")),
        name="gdn",
    )(proj, proj, proj, proj, proj, gdn_conv_w, alog_p, dtb_p, out_norm_w.reshape(1, HEAD_DIM))


def _rms(x, w):
    return x * lax.rsqrt(jnp.mean(x * x, axis=-1, keepdims=True) + EPS) * w


def _nsaprep_kernel(q_ref, ksl_ref, vsl_ref, kwn_ref, vwn_ref, qw_ref, kslw_ref, kwnw_ref,
                    qo_ref, kslo_ref, vslo_ref, kwno_ref, vwno_ref):
    for h in range(NSA_HEADS):
        hl = slice(h * HEAD_DIM, (h + 1) * HEAD_DIM)
        qo_ref[:, hl] = (_rms(q_ref[:, hl], qw_ref[...]) * (HEAD_DIM ** -0.5)).astype(BF16)
    kslo_ref[...] = _rms(ksl_ref[...], kslw_ref[...]).astype(BF16)
    kwno_ref[...] = _rms(kwn_ref[...], kwnw_ref[...]).astype(BF16)
    vslo_ref[...] = vsl_ref[...].astype(BF16)
    vwno_ref[...] = vwn_ref[...].astype(BF16)


def _nsaprep(proj, q_w, ksl_w, kwn_w):
    t = proj.shape[0]
    tm = 1024
    col = lambda c: pl.BlockSpec((tm, HEAD_DIM), lambda i: (i, c // HEAD_DIM))
    vec = pl.BlockSpec((1, HEAD_DIM), lambda i: (0, 0))
    kv_out = pl.BlockSpec((tm, HEAD_DIM), lambda i: (i, 0))
    kv_shape = jax.ShapeDtypeStruct((t, HEAD_DIM), BF16)
    return pl.pallas_call(
        _nsaprep_kernel,
        grid=(t // tm,),
        in_specs=[pl.BlockSpec((tm, NSA_WIDTH), lambda i: (i, COL_NQ // NSA_WIDTH)),
                  col(COL_KSL), col(COL_VSL), col(COL_KWN), col(COL_VWN), vec, vec, vec],
        out_specs=[pl.BlockSpec((tm, NSA_WIDTH), lambda i: (i, 0)), kv_out, kv_out, kv_out, kv_out],
        out_shape=[jax.ShapeDtypeStruct((t, NSA_WIDTH), BF16), kv_shape, kv_shape, kv_shape, kv_shape],
        compiler_params=_params(("arbitrary",)),
        name="nsaprep",
    )(proj, proj, proj, proj, proj, q_w.reshape(1, -1), ksl_w.reshape(1, -1), kwn_w.reshape(1, -1))


def _compress_kernel(kc_ref, vc_ref, kpos_ref, kw1_ref, kw2_ref, knw_ref, vpos_ref, vw1_ref, vw2_ref,
                     ko_ref, vo_ref, *, n_cmp):
    ncp = ko_ref.shape[0]
    half = CMP_BLOCK // 2

    def one(x_ref, pos_ref, w1_ref, w2_ref):
        p = jnp.zeros((ncp, HEAD_DIM), F32)
        q = jnp.zeros((ncp, HEAD_DIM), F32)
        for j in range(half):
            a = x_ref[pl.ds(j, ncp, stride=CMP_STRIDE), :]
            p = p + _dot(a + pos_ref[j:j + 1, :], w1_ref[j * HEAD_DIM:(j + 1) * HEAD_DIM, :])
            q = q + _dot(a + pos_ref[half + j:half + j + 1, :],
                         w1_ref[(half + j) * HEAD_DIM:(half + j + 1) * HEAD_DIM, :])
        hid = _silu(p + pltpu.roll(q, ncp - 1, axis=0))
        out = _dot(hid, w2_ref[...])
        valid = lax.broadcasted_iota(jnp.int32, (ncp, HEAD_DIM), 0) < n_cmp
        return jnp.where(valid, out, 0.0)

    k = one(kc_ref, kpos_ref, kw1_ref, kw2_ref)
    ko_ref[...] = jnp.where(lax.broadcasted_iota(jnp.int32, k.shape, 0) < n_cmp,
                            _rms(k, knw_ref[...]), 0.0).astype(BF16)
    vo_ref[...] = one(vc_ref, vpos_ref, vw1_ref, vw2_ref).astype(BF16)


def _compress(proj, kpos, kw1, kw2, knw, vpos, vw1, vw2, bsz, seq):
    ncp = seq // CMP_STRIDE
    n_cmp = (seq - CMP_BLOCK) // CMP_STRIDE + 1
    full = lambda a: pl.BlockSpec(a.shape, lambda b: (0,) * a.ndim)
    knw2 = knw.reshape(1, HEAD_DIM)
    out = pl.BlockSpec((None, ncp, HEAD_DIM), lambda b: (b, 0, 0))
    shp = jax.ShapeDtypeStruct((bsz, ncp, HEAD_DIM), BF16)
    return pl.pallas_call(
        functools.partial(_compress_kernel, n_cmp=n_cmp),
        grid=(bsz,),
        in_specs=[pl.BlockSpec((seq, HEAD_DIM), lambda b: (b, COL_KC // HEAD_DIM)),
                  pl.BlockSpec((seq, HEAD_DIM), lambda b: (b, COL_VC // HEAD_DIM)),
                  full(kpos), full(kw1), full(kw2), full(knw2), full(vpos), full(vw1), full(vw2)],
        out_specs=[out, out],
        out_shape=[shp, shp],
        compiler_params=_params(("arbitrary",)),
        name="compress",
    )(proj, proj, kpos, kw1, kw2, knw2, vpos, vw1, vw2)


def _nsa_kernel(q_ref, kc_ref, vc_ref, ks_ref, vs_ref, kw_ref, vw_ref, sm_ref, ov_ref, o_ref,
                m_ref, l_ref, acc_ref, *, tq, tk, n_pick):
    qi = pl.program_id(1)
    t0 = qi * tq
    nh = NSA_HEADS
    ncp = kc_ref.shape[0]
    nsp = ov_ref.shape[1]
    q = q_ref[...]
    q2 = jnp.concatenate([q[:, h * HEAD_DIM:(h + 1) * HEAD_DIM] for h in range(nh)], axis=0)
    gates = _sigmoid(sm_ref[...])

    def gate(h, g):
        c = LANE_GATE + N_GATES * h + g
        return gates[:, c:c + 1]

    def head(x, h):
        return x[h * tq:(h + 1) * tq]

    t_c = t0 + lax.broadcasted_iota(jnp.int32, (tq, ncp), 0)
    cend = lax.broadcasted_iota(jnp.int32, (tq, ncp), 1) * CMP_STRIDE + (CMP_BLOCK - 1)
    mask_c = cend <= t_c
    s_c = _dot_nt(q2, kc_ref[...])
    vcmp = vc_ref[...]
    o_c = []
    psum = jnp.zeros((tq, ncp), F32)
    for h in range(nh):
        s = jnp.where(mask_c, head(s_c, h), NEG_INF)
        m = jnp.max(s, axis=-1, keepdims=True)
        p = jnp.where(mask_c, jnp.exp(s - m), 0.0)
        l = jnp.sum(p, axis=-1, keepdims=True)
        p = p * jnp.where(l > 0.0, 1.0 / l, 0.0)
        psum = psum + p
        o_c.append(_dot(p, vcmp))
    ph, pl_ = _split2(psum)
    ov = ov_ref[...]
    imp = jnp.dot(ph, ov, preferred_element_type=F32) + jnp.dot(pl_, ov, preferred_element_type=F32)

    jj = lax.broadcasted_iota(jnp.int32, (tq, nsp), 1)
    cur = (t0 + lax.broadcasted_iota(jnp.int32, (tq, nsp), 0)) // SEL_BLOCK
    forced = (jj == 0) | (jj == cur) | (jj == cur - 1)
    valid = jj <= cur
    sel = jnp.where(valid & forced, 1.0, 0.0)
    work = jnp.where(valid, jnp.where(forced, -BIG, imp), -BIG)
    jf = jj.astype(F32)
    for _ in range(n_pick - 3):
        mx = jnp.max(work, axis=-1, keepdims=True)
        cand = jnp.where(work == mx, jnp.where(mx > -1e38, jf, float(nsp)), float(nsp))
        first = jnp.min(cand, axis=-1, keepdims=True)
        pick = jf == first
        sel = jnp.where(pick, 1.0, sel)
        work = jnp.where(pick, -BIG, work)
    sel_b = sel.astype(BF16)

    m_ref[...] = jnp.full(m_ref.shape, NEG_INF, F32)
    l_ref[...] = jnp.zeros(l_ref.shape, F32)
    acc_ref[...] = jnp.zeros(acc_ref.shape, F32)
    blocks_per_tile = tk // SEL_BLOCK
    n_kt = (t0 + tq - 1) // tk + 1

    def sweep(kt, carry):
        k0 = pl.multiple_of(kt * tk, tk)
        kt_keys = ks_ref[pl.ds(k0, tk), :]
        kt_vals = vs_ref[pl.ds(k0, tk), :]
        s = _dot_nt(q2, kt_keys)
        jr = lax.broadcasted_iota(jnp.int32, (nsp, tk), 0)
        cc = lax.broadcasted_iota(jnp.int32, (nsp, tk), 1)
        expand = jnp.where(jr == kt * blocks_per_tile + cc // SEL_BLOCK, 1.0, 0.0).astype(BF16)
        picked = jnp.dot(sel_b, expand, preferred_element_type=F32)
        kpos = k0 + lax.broadcasted_iota(jnp.int32, (tq, tk), 1)
        t_s = t0 + lax.broadcasted_iota(jnp.int32, (tq, tk), 0)
        bias = jnp.where(picked > 0.5, jnp.where(kpos <= t_s, 0.0, NEG_INF), NEG_INF)
        for h in range(nh):
            rows = slice(h * tq, (h + 1) * tq)
            sh = head(s, h) + bias
            m_prev = m_ref[rows]
            m_next = jnp.maximum(m_prev, jnp.max(sh, axis=-1, keepdims=True))
            alpha = jnp.exp(m_prev - m_next)
            p = jnp.exp(sh - pltpu.repeat(m_next, tk // LANES, axis=1))
            l_ref[rows] = alpha * l_ref[rows] + jnp.sum(p, axis=-1, keepdims=True)
            acc_ref[rows] = alpha * acc_ref[rows] + _dot(p, kt_vals)
            m_ref[rows] = m_next
        return carry

    lax.fori_loop(0, n_kt, sweep, 0)

    wlen = WINDOW + tq
    start = pl.multiple_of(jnp.maximum(t0 - WINDOW, 0), tq)
    kw = kw_ref[pl.ds(start, wlen), :]
    vw = vw_ref[pl.ds(start, wlen), :]
    s_w = _dot_nt(q2, kw)
    kpos_w = start + lax.broadcasted_iota(jnp.int32, (tq, wlen), 1)
    t_w = t0 + lax.broadcasted_iota(jnp.int32, (tq, wlen), 0)
    mask_w = (kpos_w <= t_w) & (kpos_w > t_w - WINDOW)

    for h in range(nh):
        rows = slice(h * tq, (h + 1) * tq)
        s = jnp.where(mask_w, head(s_w, h), NEG_INF)
        m = jnp.max(s, axis=-1, keepdims=True)
        p = jnp.where(mask_w, jnp.exp(s - m), 0.0)
        l = jnp.sum(p, axis=-1, keepdims=True)
        o_w = _dot(p, vw) * jnp.where(l > 0.0, 1.0 / l, 0.0)
        l_s = l_ref[rows]
        o_s = acc_ref[rows] * jnp.where(l_s > 0.0, 1.0 / l_s, 0.0)
        o_ref[:, h * HEAD_DIM:(h + 1) * HEAD_DIM] = (gate(h, 0) * o_c[h] + gate(h, 1) * o_s
                                                     + gate(h, 2) * o_w)


def _nsa(qn, kcmp, vcmp, ksl, vsl, kwn, vwn, proj, overlap, bsz, seq):
    tq, tk = 128, 512
    per_b = seq // tq
    ncp = kcmp.shape[1]
    nsp = overlap.shape[1]
    n_pick = min(N_SELECT, seq // SEL_BLOCK)
    assert n_pick == N_SELECT and seq % tk == 0 and seq >= WINDOW + tq
    seq_kv = lambda: pl.BlockSpec((seq, HEAD_DIM), lambda b, i: (b, 0))
    cmp_kv = lambda: pl.BlockSpec((None, ncp, HEAD_DIM), lambda b, i: (b, 0, 0))
    return pl.pallas_call(
        functools.partial(_nsa_kernel, tq=tq, tk=tk, n_pick=n_pick),
        grid=(bsz, per_b),
        in_specs=[pl.BlockSpec((tq, NSA_WIDTH), lambda b, i: (b * per_b + i, 0)),
                  cmp_kv(), cmp_kv(), seq_kv(), seq_kv(), seq_kv(), seq_kv(),
                  pl.BlockSpec((tq, LANES), lambda b, i: (b * per_b + i, COL_SMALL // LANES)),
                  pl.BlockSpec((ncp, nsp), lambda b, i: (0, 0))],
        out_specs=pl.BlockSpec((tq, NSA_WIDTH), lambda b, i: (b * per_b + i, 0)),
        out_shape=jax.ShapeDtypeStruct((bsz * seq, NSA_WIDTH), F32),
        scratch_shapes=[pltpu.VMEM((NSA_HEADS * tq, LANES), F32),
                        pltpu.VMEM((NSA_HEADS * tq, LANES), F32),
                        pltpu.VMEM((NSA_HEADS * tq, HEAD_DIM), F32)],
        compiler_params=_params(("arbitrary", "arbitrary")),
        name="nsa",
    )(qn, kcmp, vcmp, ksl, vsl, kwn, vwn, proj, overlap)


def _overlap_matrix(seq):
    n_cmp = (seq - CMP_BLOCK) // CMP_STRIDE + 1
    n_sel = seq // SEL_BLOCK
    ncp = seq // CMP_STRIDE
    nsp = -(-n_sel // LANES) * LANES
    cs = np.arange(n_cmp) * CMP_STRIDE
    ss = np.arange(n_sel) * SEL_BLOCK
    ov = np.minimum(cs[:, None] + CMP_BLOCK, ss[None, :] + SEL_BLOCK) - np.maximum(cs[:, None], ss[None, :])
    out = np.zeros((ncp, nsp), np.float32)
    out[:n_cmp, :n_sel] = np.clip(ov, 0, None).astype(np.float32) / CMP_BLOCK
    return jnp.asarray(out, dtype=BF16)


def _outproj_kernel(x_ref, og_ref, on_ref, mod_ref, wo_ref, nw_ref, x1_ref, h2_ref):
    m = mod_ref[0]
    y = _dot(og_ref[...], wo_ref[:GDN_WIDTH, :]) + _dot(on_ref[...], wo_ref[GDN_WIDTH:, :])
    x1 = x_ref[...] + m[2:3] * y
    x1_ref[...] = x1
    h2_ref[...] = _rms_mod(x1, nw_ref[...], m[4:5], m[3:4]).astype(BF16)


def _outproj(x2, o_g, o_n, mod3, w_out_b, norm2_w, seq):
    t, d = x2.shape
    tm = 512
    per_b = seq // tm
    return pl.pallas_call(
        _outproj_kernel,
        grid=(t // tm,),
        in_specs=[pl.BlockSpec((tm, d), lambda i: (i, 0)),
                  pl.BlockSpec((tm, GDN_WIDTH), lambda i: (i, 0)),
                  pl.BlockSpec((tm, NSA_WIDTH), lambda i: (i, 0)),
                  pl.BlockSpec((1, 6, d), lambda i: (i // per_b, 0, 0)),
                  pl.BlockSpec((d, d), lambda i: (0, 0)),
                  pl.BlockSpec((1, d), lambda i: (0, 0))],
        out_specs=[pl.BlockSpec((tm, d), lambda i: (i, 0)), pl.BlockSpec((tm, d), lambda i: (i, 0))],
        out_shape=[jax.ShapeDtypeStruct((t, d), F32), jax.ShapeDtypeStruct((t, d), BF16)],
        compiler_params=_params(("arbitrary",)),
        name="outproj",
    )(x2, o_g, o_n, mod3, w_out_b, norm2_w.reshape(1, d))


def _ffn_kernel(h2_ref, x1_ref, mod_ref, wa_ref, wb_ref, cwa_ref, cwb_ref, ba_ref, bb_ref, wd_ref,
                o_ref, tail_a, tail_b, acc_ref, *, tm):
    @pl.when(pl.program_id(1) == 0)
    def _():
        tail_a[...] = jnp.zeros_like(tail_a)
        tail_b[...] = jnp.zeros_like(tail_b)

    h2 = h2_ref[...]
    acc_ref[...] = jnp.zeros_like(acc_ref)
    n_chunks = wa_ref.shape[0]

    def conv(u, tail_ref, cw, bias, c):
        tail = tail_ref[c]
        y = (cw[FFN_CONV - 1:FFN_CONV] * u + bias)
        for j in range(FFN_CONV - 1):
            y = y + cw[j:j + 1] * _shift_rows(tail, u, FFN_CONV - 1 - j)
        tail_ref[c] = u[tm - SUBLANES:]
        return y

    def chunk(c, carry):
        ua = jnp.dot(h2, wa_ref[c], preferred_element_type=F32)
        ub = jnp.dot(h2, wb_ref[c], preferred_element_type=F32)
        a = conv(ua, tail_a, cwa_ref[c], ba_ref[c], c)
        b = conv(ub, tail_b, cwb_ref[c], bb_ref[c], c)
        act = (_silu(a) * b).astype(BF16)
        acc_ref[...] += jnp.dot(act, wd_ref[c], preferred_element_type=F32)
        return carry

    lax.fori_loop(0, n_chunks, chunk, 0)
    o_ref[...] = x1_ref[...] + mod_ref[0][5:6] * acc_ref[...]


def _ffn(h2, x1, mod3, wa, wb, cwa, cwb, ba, bb, wd, bsz, seq):
    t, d = x1.shape
    tm = 512
    per_b = seq // tm
    nc, _, cw = wa.shape
    full = lambda a: pl.BlockSpec(a.shape, lambda b, s: (0,) * a.ndim)
    return pl.pallas_call(
        functools.partial(_ffn_kernel, tm=tm),
        grid=(bsz, per_b),
        in_specs=[pl.BlockSpec((tm, d), lambda b, s: (b * per_b + s, 0)),
                  pl.BlockSpec((tm, d), lambda b, s: (b * per_b + s, 0)),
                  pl.BlockSpec((1, 6, d), lambda b, s: (b, 0, 0)),
                  full(wa), full(wb), full(cwa), full(cwb), full(ba), full(bb), full(wd)],
        out_specs=pl.BlockSpec((tm, d), lambda b, s: (b * per_b + s, 0)),
        out_shape=jax.ShapeDtypeStruct((t, d), F32),
        scratch_shapes=[pltpu.VMEM((nc, SUBLANES, cw), F32),
                        pltpu.VMEM((nc, SUBLANES, cw), F32),
                        pltpu.VMEM((tm, d), F32)],
        compiler_params=_params(("arbitrary", "arbitrary")),
        name="ffn",
    )(h2, x1, mod3, wa, wb, cwa, cwb, ba, bb, wd)


def _pad_lanes(a, n):
    return jnp.pad(a, ((0, 0), (0, n - a.shape[1])))


def _layer(x, c, ada_w, ada_b, norm1_w, w_in, gdn_conv_w, gdn_A_log, gdn_dt_bias, gdn_out_norm_w,
           nsa_q_norm_w, nsa_k_norm_cmp, nsa_k_norm_slc, nsa_k_norm_win, cmp_k_pos, cmp_k_w1, cmp_k_w2,
           cmp_v_pos, cmp_v_w1, cmp_v_w2, w_out, norm2_w, ffn_w_up, ffn_conv_w, ffn_conv_b, ffn_w_down):
    bsz, seq, d = x.shape
    x2 = x.reshape(bsz * seq, d)
    mod3 = _ada(c, ada_w, ada_b).reshape(bsz, 6, d)

    o = np.cumsum([0, GDN_WIDTH, GDN_WIDTH, GDN_WIDTH, GDN_WIDTH, GDN_HEADS, GDN_HEADS, NSA_WIDTH,
                   HEAD_DIM, HEAD_DIM, HEAD_DIM, HEAD_DIM, HEAD_DIM, HEAD_DIM, N_GATES * NSA_HEADS])
    small = jnp.concatenate([w_in[:, o[4]:o[6]], w_in[:, o[13]:o[14]]], axis=1)
    w_in_p = jnp.concatenate([w_in[:, :o[4]], w_in[:, o[6]:o[13]], _pad_lanes(small, LANES)],
                             axis=1).astype(BF16)
    proj = _inproj(x2, mod3, norm1_w, w_in_p, seq)

    o_g = _gdn(proj, gdn_conv_w, _pad_lanes(gdn_A_log.reshape(1, -1), LANES),
               _pad_lanes(gdn_dt_bias.reshape(1, -1), LANES), gdn_out_norm_w, bsz, seq)

    qn, ksl, vsl, kwn, vwn = _nsaprep(proj, nsa_q_norm_w, nsa_k_norm_slc, nsa_k_norm_win)
    kcmp, vcmp = _compress(proj, cmp_k_pos, cmp_k_w1.astype(BF16), cmp_k_w2.astype(BF16), nsa_k_norm_cmp,
                           cmp_v_pos, cmp_v_w1.astype(BF16), cmp_v_w2.astype(BF16), bsz, seq)
    o_n = _nsa(qn, kcmp, vcmp, ksl, vsl, kwn, vwn, proj, _overlap_matrix(seq), bsz, seq)

    x1, h2 = _outproj(x2, o_g, o_n, mod3, w_out.astype(BF16), norm2_w, seq)

    d_ff = ffn_w_down.shape[0]
    cw = 256
    nc = d_ff // cw
    assert nc * cw == d_ff
    split_w = lambda w: jnp.transpose(w.reshape(d, nc, cw), (1, 0, 2)).astype(BF16)
    split_v = lambda v: jnp.transpose(v.reshape(v.shape[0], nc, cw), (1, 0, 2))
    out = _ffn(h2, x1, mod3, split_w(ffn_w_up[:, :d_ff]), split_w(ffn_w_up[:, d_ff:]),
               split_v(ffn_conv_w[:, :d_ff]), split_v(ffn_conv_w[:, d_ff:]),
               split_v(ffn_conv_b[None, :d_ff]), split_v(ffn_conv_b[None, d_ff:]),
               ffn_w_down.reshape(nc, cw, d).astype(BF16), bsz, seq)
    return out.reshape(bsz, seq, d)


def kernel(x, c, ada_w, ada_b, norm1_w, w_in, gdn_conv_w, gdn_A_log, gdn_dt_bias, gdn_out_norm_w, nsa_q_norm_w, nsa_k_norm_cmp, nsa_k_norm_slc, nsa_k_norm_win, cmp_k_pos, cmp_k_w1, cmp_k_w2, cmp_v_pos, cmp_v_w1, cmp_v_w2, w_out, norm2_w, ffn_w_up, ffn_conv_w, ffn_conv_b, ffn_w_down):
    for l in range(ada_w.shape[0]):
        x = _layer(x, c, ada_w[l], ada_b[l], norm1_w[l], w_in[l], gdn_conv_w[l], gdn_A_log[l],
                   gdn_dt_bias[l], gdn_out_norm_w[l], nsa_q_norm_w[l], nsa_k_norm_cmp[l],
                   nsa_k_norm_slc[l], nsa_k_norm_win[l], cmp_k_pos[l], cmp_k_w1[l], cmp_k_w2[l],
                   cmp_v_pos[l], cmp_v_w1[l], cmp_v_w2[l], w_out[l], norm2_w[l], ffn_w_up[l],
                   ffn_conv_w[l], ffn_conv_b[l], ffn_w_down[l])
    return x
```

```python
import functools

import numpy as np
import jax
import jax.numpy as jnp
from jax import lax
from jax.experimental import pallas as pl
from jax.experimental.pallas import tpu as pltpu

F32 = jnp.float32
BF16 = jnp.bfloat16

HEAD_DIM = 128
GDN_HEADS = 4
NSA_HEADS = 4
GDN_WIDTH = GDN_HEADS * HEAD_DIM
NSA_WIDTH = NSA_HEADS * HEAD_DIM
GDN_CONV = 4
GDN_CHUNK = 64
CMP_BLOCK = 32
CMP_STRIDE = 16
SEL_BLOCK = 64
N_SELECT = 16
WINDOW = 512
FFN_CONV = 3
N_GATES = 3
EPS = 1e-6
NEG_INF = -1e30
BIG = 3e38
LOG2E = 1.4426950408889634

LANES = 128
SUBLANES = 8
VMEM_LIMIT = 56 * 1024 * 1024

COL_GQ, COL_GK, COL_GV, COL_GZ = 0, 512, 1024, 1536
COL_NQ = 2048
COL_KC, COL_VC, COL_KSL, COL_VSL, COL_KWN, COL_VWN = 2560, 2688, 2816, 2944, 3072, 3200
COL_SMALL = 3328
N_PROJ = 3456
LANE_A, LANE_BETA, LANE_GATE = 0, 4, 8


def _params(sem):
    return pltpu.CompilerParams(dimension_semantics=sem, vmem_limit_bytes=VMEM_LIMIT)


def _dot(a, b):
    return jnp.dot(a.astype(BF16), b.astype(BF16), preferred_element_type=F32)


def _dot_nt(a, b):
    return lax.dot_general(a.astype(BF16), b.astype(BF16), (((1,), (1,)), ((), ())),
                           preferred_element_type=F32)


def _dot_tn(a, b):
    return lax.dot_general(a.astype(BF16), b.astype(BF16), (((0,), (0,)), ((), ())),
                           preferred_element_type=F32)


def _split2(a):
    hi = a.astype(BF16)
    lo = (a - hi.astype(F32)).astype(BF16)
    return hi, lo


def _split3(a):
    h1 = a.astype(BF16)
    r = a - h1.astype(F32)
    h2 = r.astype(BF16)
    h3 = (r - h2.astype(F32)).astype(BF16)
    return h1, h2, h3


def _dot_hp(a, b):
    ah, al = _split2(a)
    bh, bl = _split2(b)
    d = functools.partial(jnp.dot, preferred_element_type=F32)
    return d(ah, bh) + (d(ah, bl) + d(al, bh))


def _dot_exact_lhs(a_exact, b):
    a = a_exact.astype(BF16)
    b1, b2, b3 = _split3(b)
    d = functools.partial(jnp.dot, preferred_element_type=F32)
    return d(a, b1) + (d(a, b2) + d(a, b3))


def _sigmoid(x):
    return 1.0 / (1.0 + jnp.exp(-x))


def _silu(x):
    return x * _sigmoid(x)


def _softplus(x):
    return jnp.maximum(x, 0.0) + jnp.log1p(jnp.exp(-jnp.abs(x)))


def _ada_kernel(c_ref, w_ref, b_ref, o_ref):
    c = c_ref[...]
    o_ref[...] = _dot_hp(_silu(c), w_ref[...]) + b_ref[...]


def _ada(c, ada_w, ada_b):
    bsz, d = c.shape
    n = ada_w.shape[1]
    tn = 1024
    return pl.pallas_call(
        _ada_kernel,
        grid=(n // tn,),
        in_specs=[pl.BlockSpec((bsz, d), lambda j: (0, 0)),
                  pl.BlockSpec((d, tn), lambda j: (0, j)),
                  pl.BlockSpec((1, tn), lambda j: (0, j))],
        out_specs=pl.BlockSpec((bsz, tn), lambda j: (0, j)),
        out_shape=jax.ShapeDtypeStruct((bsz, n), F32),
        compiler_params=_params(("arbitrary",)),
        name="ada",
    )(c, ada_w, ada_b.reshape(1, n))


def _rms_mod(x, nw, scale, shift):
    y = x * lax.rsqrt(jnp.mean(x * x, axis=-1, keepdims=True) + EPS) * nw
    return y * (1.0 + scale) + shift


def _inproj_kernel(x_ref, mod_ref, nw_ref, w_ref, o_ref, *, tn):
    m = mod_ref[0]
    h = _rms_mod(x_ref[...], nw_ref[...], m[1:2], m[0:1]).astype(BF16)
    for j in range(o_ref.shape[1] // tn):
        o_ref[:, j * tn:(j + 1) * tn] = jnp.dot(h, w_ref[:, j * tn:(j + 1) * tn],
                                                 preferred_element_type=F32)


def _inproj(x2, mod3, norm1_w, w_in_p, seq):
    t, d = x2.shape
    tm = 512
    per_b = seq // tm
    return pl.pallas_call(
        functools.partial(_inproj_kernel, tn=1152),
        grid=(t // tm,),
        in_specs=[pl.BlockSpec((tm, d), lambda i: (i, 0)),
                  pl.BlockSpec((1, 6, d), lambda i: (i // per_b, 0, 0)),
                  pl.BlockSpec((1, d), lambda i: (0, 0)),
                  pl.BlockSpec((d, N_PROJ), lambda i: (0, 0))],
        out_specs=pl.BlockSpec((tm, N_PROJ), lambda i: (i, 0)),
        out_shape=jax.ShapeDtypeStruct((t, N_PROJ), F32),
        compiler_params=_params(("arbitrary",)),
        name="inproj",
    )(x2, mod3, norm1_w.reshape(1, d), w_in_p)


def _shift_rows(halo, x, k):
    xe = jnp.concatenate([halo, x], axis=0)
    return pltpu.roll(xe, k, axis=0)[halo.shape[0]:]


def _gdn_kernel(q_ref, k_ref, v_ref, z_ref, sm_ref, cw_ref, alog_ref, dtb_ref, onw_ref, o_ref,
                halo_ref, state_ref, pq_ref, p2_ref, oi_ref, os_ref, *, tile):
    c_len = GDN_CHUNK
    n_chunks = tile // c_len

    @pl.when(pl.program_id(1) == 0)
    def _():
        halo_ref[...] = jnp.zeros_like(halo_ref)
        state_ref[...] = jnp.zeros_like(state_ref)

    def conv_silu(x_ref, idx):
        x = x_ref[...]
        halo = halo_ref[idx]
        cw = cw_ref[:, idx * GDN_WIDTH:(idx + 1) * GDN_WIDTH]
        y = cw[GDN_CONV - 1:GDN_CONV] * x
        for j in range(GDN_CONV - 1):
            y = y + cw[j:j + 1] * _shift_rows(halo, x, GDN_CONV - 1 - j)
        halo_ref[idx] = x[tile - SUBLANES:]
        return _silu(y)

    qc = conv_silu(q_ref, 0)
    kc = conv_silu(k_ref, 1)
    vc = conv_silu(v_ref, 2)

    sm = sm_ref[...]
    g_all = -jnp.exp(alog_ref[...]) * _softplus(sm + dtb_ref[...])
    beta_all = _sigmoid(sm)

    ri = lax.broadcasted_iota(jnp.int32, (tile, tile), 0)
    ci = lax.broadcasted_iota(jnp.int32, (tile, tile), 1)
    lblk = jnp.where((ri // c_len == ci // c_len) & (ci <= ri), 1.0, 0.0)
    gc_all = _dot_exact_lhs(lblk, g_all)
    gc_t = gc_all.T
    gl_all = jnp.concatenate(
        [jnp.broadcast_to(gc_all[(c + 1) * c_len - 1:(c + 1) * c_len], (c_len, LANES))
         for c in range(n_chunks)], axis=0)
    eg_all = jnp.exp(gc_all)
    ekd_all = jnp.exp(gl_all - gc_all)
    egl_all = jnp.exp(gl_all)

    r64 = lax.broadcasted_iota(jnp.int32, (c_len, c_len), 0)
    c64 = lax.broadcasted_iota(jnp.int32, (c_len, c_len), 1)
    incl = r64 >= c64
    strict = r64 > c64
    eye = jnp.where(r64 == c64, 1.0, 0.0)

    def lane_bcast(x, lane):
        return jnp.broadcast_to(x[:, lane:lane + 1], (tile, HEAD_DIM))

    units = [(c, h) for c in range(n_chunks) for h in range(GDN_HEADS)]
    heads = []
    for h in range(GDN_HEADS):
        hl = slice(h * HEAD_DIM, (h + 1) * HEAD_DIM)
        qh = qc[:, hl]
        kh = kc[:, hl]
        qh = qh * lax.rsqrt(jnp.sum(qh * qh, axis=-1, keepdims=True) + EPS) * (HEAD_DIM ** -0.5)
        kh = kh * lax.rsqrt(jnp.sum(kh * kh, axis=-1, keepdims=True) + EPS)
        bb = lane_bcast(beta_all, LANE_BETA + h)
        eg = lane_bcast(eg_all, LANE_A + h)
        kb = kh * bb
        heads.append(dict(qd=qh * eg, k=kh, kb=kb, q=qh, vb=vc[:, hl] * bb, kbe=kb * eg,
                          kd=kh * lane_bcast(ekd_all, LANE_A + h),
                          egl=lane_bcast(egl_all, LANE_A + h)))

    a_mats, attns = [], []
    for c, h in units:
        rows = slice(c * c_len, (c + 1) * c_len)
        hd = heads[h]
        gcol = gc_all[rows, LANE_A + h:LANE_A + h + 1]
        grow = gc_t[LANE_A + h:LANE_A + h + 1, rows]
        decay = jnp.exp(jnp.where(incl, gcol - grow, NEG_INF))
        gram = _dot_nt(jnp.concatenate([hd["kb"][rows], hd["q"][rows]], axis=0), hd["k"][rows])
        a_mats.append(jnp.where(strict, gram[:c_len] * decay, 0.0))
        attns.append(jnp.where(incl, gram[c_len:] * decay, 0.0))
    xps = [-a for a in a_mats]
    t_mats = [eye + x for x in xps]
    for _ in range(5):
        xps = [_dot_hp(x, x) for x in xps]
        t_mats = [t + _dot_hp(t, x) for t, x in zip(t_mats, xps)]
    for i, (c, h) in enumerate(units):
        rows = slice(c * c_len, (c + 1) * c_len)
        hd = heads[h]
        uw = _dot(t_mats[i], jnp.concatenate([hd["vb"][rows], hd["kbe"][rows]], axis=1))
        aw = _dot(attns[i], uw)
        kw = _dot_tn(hd["kd"][rows], uw)
        pq_ref[i, :HEAD_DIM] = kw[:, HEAD_DIM:]
        pq_ref[i, HEAD_DIM:] = hd["qd"][rows] - aw[:, HEAD_DIM:]
        p2_ref[i] = kw[:, :HEAD_DIM]
        oi_ref[i] = aw[:, :HEAD_DIM]

    for i, (c, h) in enumerate(units):
        rows = slice(c * c_len, (c + 1) * c_len)
        state = state_ref[h]
        r = _dot(pq_ref[i], state)
        os_ref[rows, h * HEAD_DIM:(h + 1) * HEAD_DIM] = r[HEAD_DIM:] + oi_ref[i]
        egl = heads[h]["egl"][c * c_len:c * c_len + 1]
        state_ref[h] = state * egl - r[:HEAD_DIM] + p2_ref[i]

    for h in range(GDN_HEADS):
        hl = slice(h * HEAD_DIM, (h + 1) * HEAD_DIM)
        o = os_ref[:, hl]
        o = o * lax.rsqrt(jnp.mean(o * o, axis=-1, keepdims=True) + EPS) * onw_ref[...]
        o_ref[:, hl] = o * _silu(z_ref[:, hl])


def _gdn(proj, gdn_conv_w, alog_p, dtb_p, out_norm_w, bsz, seq):
    tile = 256
    per_b = seq // tile
    n_units = tile // GDN_CHUNK * GDN_HEADS
    row = lambda b, s: b * per_b + s
    wide = GDN_WIDTH // LANES * LANES
    return pl.pallas_call(
        functools.partial(_gdn_kernel, tile=tile),
        grid=(bsz, per_b),
        in_specs=[pl.BlockSpec((tile, wide), lambda b, s: (row(b, s), COL_GQ // wide)),
                  pl.BlockSpec((tile, wide), lambda b, s: (row(b, s), COL_GK // wide)),
                  pl.BlockSpec((tile, wide), lambda b, s: (row(b, s), COL_GV // wide)),
                  pl.BlockSpec((tile, wide), lambda b, s: (row(b, s), COL_GZ // wide)),
                  pl.BlockSpec((tile, LANES), lambda b, s: (row(b, s), COL_SMALL // LANES)),
                  pl.BlockSpec((GDN_CONV, 3 * GDN_WIDTH), lambda b, s: (0, 0)),
                  pl.BlockSpec((1, LANES), lambda b, s: (0, 0)),
                  pl.BlockSpec((1, LANES), lambda b, s: (0, 0)),
                  pl.BlockSpec((1, HEAD_DIM), lambda b, s: (0, 0))],
        out_specs=pl.BlockSpec((tile, GDN_WIDTH), lambda b, s: (row(b, s), 0)),
        out_shape=jax.ShapeDtypeStruct((bsz * seq, GDN_WIDTH), F32),
        scratch_shapes=[pltpu.VMEM((3, SUBLANES, GDN_WIDTH), F32),
                        pltpu.VMEM((GDN_HEADS, HEAD_DIM, HEAD_DIM), F32),
                        pltpu.VMEM((n_units, HEAD_DIM + GDN_CHUNK, HEAD_DIM), F32),
                        pltpu.VMEM((n_units, HEAD_DIM, HEAD_DIM), F32),
                        pltpu.VMEM((n_units, GDN_CHUNK, HEAD_DIM), F32),
                        pltpu.VMEM((tile, GDN_WIDTH), F32)],
        compiler_params=_params(("arbitrary", "arbitrary")),
        name="gdn",
    )(proj, proj, proj, proj, proj, gdn_conv_w, alog_p, dtb_p, out_norm_w.reshape(1, HEAD_DIM))


def _rms(x, w):
    return x * lax.rsqrt(jnp.mean(x * x, axis=-1, keepdims=True) + EPS) * w


def _nsaprep_kernel(q_ref, ksl_ref, vsl_ref, kwn_ref, vwn_ref, qw_ref, kslw_ref, kwnw_ref,
                    qo_ref, kslo_ref, vslo_ref, kwno_ref, vwno_ref, *, seq):
    tm = q_ref.shape[0]
    for h in range(NSA_HEADS):
        hl = slice(h * HEAD_DIM, (h + 1) * HEAD_DIM)
        qo_ref[:, hl] = (_rms(q_ref[:, hl], qw_ref[...]) * (HEAD_DIM ** -0.5 * LOG2E)).astype(BF16)
    kslo_ref[:, :HEAD_DIM] = _rms(ksl_ref[...], kslw_ref[...]).astype(BF16)
    pos = (pl.program_id(0) * tm + lax.broadcasted_iota(jnp.int32, (tm, LANES), 0)) % seq
    lane = lax.broadcasted_iota(jnp.int32, (tm, LANES), 1)
    kslo_ref[:, HEAD_DIM:] = jnp.where(lane == pos // SEL_BLOCK, NEG_INF, 0.0).astype(BF16)
    kwno_ref[...] = _rms(kwn_ref[...], kwnw_ref[...]).astype(BF16)
    vslo_ref[:, :HEAD_DIM] = vsl_ref[...].astype(BF16)
    vslo_ref[:, HEAD_DIM:] = jnp.ones((tm, LANES), BF16)
    vwno_ref[...] = vwn_ref[...].astype(BF16)


def _nsaprep(proj, q_w, ksl_w, kwn_w, seq):
    t = proj.shape[0]
    tm = 1024
    assert seq // SEL_BLOCK <= LANES
    col = lambda c: pl.BlockSpec((tm, HEAD_DIM), lambda i: (i, c // HEAD_DIM))
    vec = pl.BlockSpec((1, HEAD_DIM), lambda i: (0, 0))
    kv_out = pl.BlockSpec((tm, HEAD_DIM), lambda i: (i, 0))
    kv_shape = jax.ShapeDtypeStruct((t, HEAD_DIM), BF16)
    ext_out = pl.BlockSpec((tm, HEAD_DIM + LANES), lambda i: (i, 0))
    ext_shape = jax.ShapeDtypeStruct((t, HEAD_DIM + LANES), BF16)
    return pl.pallas_call(
        functools.partial(_nsaprep_kernel, seq=seq),
        grid=(t // tm,),
        in_specs=[pl.BlockSpec((tm, NSA_WIDTH), lambda i: (i, COL_NQ // NSA_WIDTH)),
                  col(COL_KSL), col(COL_VSL), col(COL_KWN), col(COL_VWN), vec, vec, vec],
        out_specs=[pl.BlockSpec((tm, NSA_WIDTH), lambda i: (i, 0)), ext_out, ext_out, kv_out, kv_out],
        out_shape=[jax.ShapeDtypeStruct((t, NSA_WIDTH), BF16), ext_shape, ext_shape, kv_shape, kv_shape],
        compiler_params=_params(("arbitrary",)),
        name="nsaprep",
    )(proj, proj, proj, proj, proj, q_w.reshape(1, -1), ksl_w.reshape(1, -1), kwn_w.reshape(1, -1))


def _compress_kernel(kc_ref, vc_ref, kpos_ref, kw1_ref, kw2_ref, knw_ref, vpos_ref, vw1_ref, vw2_ref,
                     ko_ref, vo_ref, *, n_cmp):
    ncp = ko_ref.shape[0]
    half = CMP_BLOCK // 2

    def one(x_ref, pos_ref, w1_ref, w2_ref):
        p = jnp.zeros((ncp, HEAD_DIM), F32)
        q = jnp.zeros((ncp, HEAD_DIM), F32)
        for j in range(half):
            a = x_ref[pl.ds(j, ncp, stride=CMP_STRIDE), :]
            p = p + _dot(a + pos_ref[j:j + 1, :], w1_ref[j * HEAD_DIM:(j + 1) * HEAD_DIM, :])
            q = q + _dot(a + pos_ref[half + j:half + j + 1, :],
                         w1_ref[(half + j) * HEAD_DIM:(half + j + 1) * HEAD_DIM, :])
        hid = _silu(p + pltpu.roll(q, ncp - 1, axis=0))
        out = _dot(hid, w2_ref[...])
        valid = lax.broadcasted_iota(jnp.int32, (ncp, HEAD_DIM), 0) < n_cmp
        return jnp.where(valid, out, 0.0)

    k = one(kc_ref, kpos_ref, kw1_ref, kw2_ref)
    ko_ref[...] = jnp.where(lax.broadcasted_iota(jnp.int32, k.shape, 0) < n_cmp,
                            _rms(k, knw_ref[...]), 0.0).astype(BF16)
    vo_ref[...] = one(vc_ref, vpos_ref, vw1_ref, vw2_ref).astype(BF16)


def _compress(proj, kpos, kw1, kw2, knw, vpos, vw1, vw2, bsz, seq):
    ncp = seq // CMP_STRIDE
    n_cmp = (seq - CMP_BLOCK) // CMP_STRIDE + 1
    full = lambda a: pl.BlockSpec(a.shape, lambda b: (0,) * a.ndim)
    knw2 = knw.reshape(1, HEAD_DIM)
    out = pl.BlockSpec((None, ncp, HEAD_DIM), lambda b: (b, 0, 0))
    shp = jax.ShapeDtypeStruct((bsz, ncp, HEAD_DIM), BF16)
    return pl.pallas_call(
        functools.partial(_compress_kernel, n_cmp=n_cmp),
        grid=(bsz,),
        in_specs=[pl.BlockSpec((seq, HEAD_DIM), lambda b: (b, COL_KC // HEAD_DIM)),
                  pl.BlockSpec((seq, HEAD_DIM), lambda b: (b, COL_VC // HEAD_DIM)),
                  full(kpos), full(kw1), full(kw2), full(knw2), full(vpos), full(vw1), full(vw2)],
        out_specs=[out, out],
        out_shape=[shp, shp],
        compiler_params=_params(("arbitrary",)),
        name="compress",
    )(proj, proj, kpos, kw1, kw2, knw2, vpos, vw1, vw2)


def _nsa_kernel(q_ref, kc_ref, vc_ref, ks_ref, vs_ref, kw_ref, vw_ref, sm_ref, ovt_ref, o_ref,
                m_ref, acc_ref, *, tq, tk, n_pick):
    qi = pl.program_id(1)
    t0 = qi * tq
    nh = NSA_HEADS
    ncp = kc_ref.shape[0]
    nsp = ovt_ref.shape[0]
    q = q_ref[...]
    q2 = jnp.concatenate([q[:, h * HEAD_DIM:(h + 1) * HEAD_DIM] for h in range(nh)], axis=0)
    gates = _sigmoid(sm_ref[...])

    def gate(h, g):
        c = LANE_GATE + N_GATES * h + g
        return gates[:, c:c + 1]

    def head(x, h):
        return x[h * tq:(h + 1) * tq]

    t_c = t0 + lax.broadcasted_iota(jnp.int32, (tq, ncp), 0)
    cend = lax.broadcasted_iota(jnp.int32, (tq, ncp), 1) * CMP_STRIDE + (CMP_BLOCK - 1)
    mask_c = cend <= t_c
    s_c = _dot_nt(q2, kc_ref[...])
    vcmp = vc_ref[...]
    o_c = []
    psum = jnp.zeros((tq, ncp), F32)
    for h in range(nh):
        s = jnp.where(mask_c, head(s_c, h), NEG_INF)
        m = jnp.max(s, axis=-1, keepdims=True)
        p = jnp.where(mask_c, jnp.exp2(s - m), 0.0)
        l = jnp.sum(p, axis=-1, keepdims=True)
        p = p * jnp.where(l > 0.0, 1.0 / l, 0.0)
        psum = psum + p
        o_c.append(_dot(p, vcmp))
    ph, pl_ = _split2(psum)
    ovt = ovt_ref[...]
    dnt = functools.partial(lax.dot_general, dimension_numbers=(((1,), (1,)), ((), ())),
                            preferred_element_type=F32)
    imp_t = dnt(ovt, ph) + dnt(ovt, pl_)

    wlen = WINDOW + tq
    start = pl.multiple_of(jnp.maximum(t0 - WINDOW, 0), tq)
    kw = kw_ref[pl.ds(start, wlen), :]
    vw = vw_ref[pl.ds(start, wlen), :]
    s_w = _dot_nt(q2, kw)
    kpos_w = start + lax.broadcasted_iota(jnp.int32, (tq, wlen), 1)
    t_w = t0 + lax.broadcasted_iota(jnp.int32, (tq, wlen), 0)
    mask_w = (kpos_w <= t_w) & (kpos_w > t_w - WINDOW)
    o_w = []
    for h in range(nh):
        s = jnp.where(mask_w, head(s_w, h), NEG_INF)
        m = jnp.max(s, axis=-1, keepdims=True)
        p = jnp.where(mask_w, jnp.exp2(s - m), 0.0)
        l = jnp.sum(p, axis=-1, keepdims=True)
        o_w.append(_dot(p, vw) * jnp.where(l > 0.0, 1.0 / l, 0.0))

    jj = lax.broadcasted_iota(jnp.int32, (nsp, tq), 0)
    cur = (t0 + lax.broadcasted_iota(jnp.int32, (nsp, tq), 1)) // SEL_BLOCK
    forced = (jj == 0) | (jj == cur) | (jj == cur - 1)
    valid = jj <= cur
    notsel = jnp.where(valid & forced, 0.0, 1.0)
    work = jnp.where(valid, jnp.where(forced, -BIG, imp_t), -BIG)
    jf = jj.astype(F32)
    for _ in range(n_pick - 3):
        mx = jnp.max(work, axis=0, keepdims=True)
        cand = jnp.where(work == mx, jnp.where(mx > -1e38, jf, float(nsp)), float(nsp))
        first = jnp.min(cand, axis=0, keepdims=True)
        pick = jf == first
        notsel = jnp.where(pick, 0.0, notsel)
        work = jnp.where(pick, -BIG, work)
    notsel_q = notsel.T.astype(BF16)
    q2x = jnp.concatenate([q2, jnp.concatenate([notsel_q] * nh, axis=0)], axis=1)

    m_ref[...] = jnp.full(m_ref.shape, NEG_INF, F32)
    acc_ref[...] = jnp.zeros(acc_ref.shape, F32)
    n_kt = (t0 + tq - 1) // tk + 1

    def scores(kt):
        k0 = pl.multiple_of(kt * tk, tk)
        return _dot_nt(q2x, ks_ref[pl.ds(k0, tk), :])

    def sweep(kt, s, causal):
        k0 = pl.multiple_of(kt * tk, tk)
        if causal:
            kpos = k0 + lax.broadcasted_iota(jnp.int32, (tq, tk), 1)
            t_s = t0 + lax.broadcasted_iota(jnp.int32, (tq, tk), 0)
            bias = jnp.where(kpos <= t_s, 0.0, NEG_INF)
        ps, alphas = [], []
        for h in range(nh):
            rows = slice(h * tq, (h + 1) * tq)
            sh = head(s, h) + bias if causal else head(s, h)
            m_prev = m_ref[rows]
            m_next = jnp.maximum(m_prev, jnp.max(sh, axis=-1, keepdims=True))
            alphas.append(jnp.exp2(m_prev - m_next))
            ps.append(jnp.exp2(sh - pltpu.repeat(m_next, tk // LANES, axis=1)).astype(BF16))
            m_ref[rows] = m_next
        pv = jnp.dot(jnp.concatenate(ps, axis=0), vs_ref[pl.ds(k0, tk), :],
                     preferred_element_type=F32)
        alpha = jnp.concatenate(alphas, axis=0)
        acc_ref[...] = pltpu.repeat(alpha, 2, axis=1) * acc_ref[...] + pv

    def sweep_body(kt, s_cur):
        s_next = scores(kt + 1)
        sweep(kt, s_cur, False)
        return s_next

    s_last = lax.fori_loop(0, n_kt - 1, sweep_body, scores(0))
    sweep(n_kt - 1, s_last, True)

    for h in range(nh):
        rows = slice(h * tq, (h + 1) * tq)
        l_s = acc_ref[rows, HEAD_DIM:]
        o_s = acc_ref[rows, :HEAD_DIM] * jnp.where(l_s > 0.0, 1.0 / l_s, 0.0)
        o_ref[:, h * HEAD_DIM:(h + 1) * HEAD_DIM] = (gate(h, 0) * o_c[h] + gate(h, 1) * o_s
                                                     + gate(h, 2) * o_w[h])


def _nsa(qn, kcmp, vcmp, ksl, vsl, kwn, vwn, proj, overlap, bsz, seq):
    tq, tk = 128, 512
    per_b = seq // tq
    ncp = kcmp.shape[1]
    nsp = overlap.shape[0]
    n_pick = min(N_SELECT, seq // SEL_BLOCK)
    assert n_pick == N_SELECT and seq % tk == 0 and seq >= WINDOW + tq and nsp == LANES
    ext = HEAD_DIM + LANES
    seq_kv = lambda w: pl.BlockSpec((seq, w), lambda b, i: (b, 0))
    cmp_kv = lambda: pl.BlockSpec((None, ncp, HEAD_DIM), lambda b, i: (b, 0, 0))
    return pl.pallas_call(
        functools.partial(_nsa_kernel, tq=tq, tk=tk, n_pick=n_pick),
        grid=(bsz, per_b),
        in_specs=[pl.BlockSpec((tq, NSA_WIDTH), lambda b, i: (b * per_b + i, 0)),
                  cmp_kv(), cmp_kv(), seq_kv(ext), seq_kv(ext), seq_kv(HEAD_DIM), seq_kv(HEAD_DIM),
                  pl.BlockSpec((tq, LANES), lambda b, i: (b * per_b + i, COL_SMALL // LANES)),
                  pl.BlockSpec((nsp, ncp), lambda b, i: (0, 0))],
        out_specs=pl.BlockSpec((tq, NSA_WIDTH), lambda b, i: (b * per_b + i, 0)),
        out_shape=jax.ShapeDtypeStruct((bsz * seq, NSA_WIDTH), F32),
        scratch_shapes=[pltpu.VMEM((NSA_HEADS * tq, LANES), F32),
                        pltpu.VMEM((NSA_HEADS * tq, ext), F32)],
        compiler_params=_params(("arbitrary", "arbitrary")),
        name="nsa",
    )(qn, kcmp, vcmp, ksl, vsl, kwn, vwn, proj, overlap)


def _overlap_matrix(seq):
    n_cmp = (seq - CMP_BLOCK) // CMP_STRIDE + 1
    n_sel = seq // SEL_BLOCK
    ncp = seq // CMP_STRIDE
    nsp = -(-n_sel // LANES) * LANES
    cs = np.arange(n_cmp) * CMP_STRIDE
    ss = np.arange(n_sel) * SEL_BLOCK
    ov = np.minimum(cs[:, None] + CMP_BLOCK, ss[None, :] + SEL_BLOCK) - np.maximum(cs[:, None], ss[None, :])
    out = np.zeros((nsp, ncp), np.float32)
    out[:n_sel, :n_cmp] = (np.clip(ov, 0, None).astype(np.float32) / CMP_BLOCK).T
    return jnp.asarray(out, dtype=BF16)


def _outproj_kernel(x_ref, og_ref, on_ref, mod_ref, wo_ref, nw_ref, x1_ref, h2_ref):
    m = mod_ref[0]
    y = _dot(og_ref[...], wo_ref[:GDN_WIDTH, :]) + _dot(on_ref[...], wo_ref[GDN_WIDTH:, :])
    x1 = x_ref[...] + m[2:3] * y
    x1_ref[...] = x1
    h2_ref[...] = _rms_mod(x1, nw_ref[...], m[4:5], m[3:4]).astype(BF16)


def _outproj(x2, o_g, o_n, mod3, w_out_b, norm2_w, seq):
    t, d = x2.shape
    tm = 512
    per_b = seq // tm
    return pl.pallas_call(
        _outproj_kernel,
        grid=(t // tm,),
        in_specs=[pl.BlockSpec((tm, d), lambda i: (i, 0)),
                  pl.BlockSpec((tm, GDN_WIDTH), lambda i: (i, 0)),
                  pl.BlockSpec((tm, NSA_WIDTH), lambda i: (i, 0)),
                  pl.BlockSpec((1, 6, d), lambda i: (i // per_b, 0, 0)),
                  pl.BlockSpec((d, d), lambda i: (0, 0)),
                  pl.BlockSpec((1, d), lambda i: (0, 0))],
        out_specs=[pl.BlockSpec((tm, d), lambda i: (i, 0)), pl.BlockSpec((tm, d), lambda i: (i, 0))],
        out_shape=[jax.ShapeDtypeStruct((t, d), F32), jax.ShapeDtypeStruct((t, d), BF16)],
        compiler_params=_params(("arbitrary",)),
        name="outproj",
    )(x2, o_g, o_n, mod3, w_out_b, norm2_w.reshape(1, d))


def _ffn_kernel(h2_ref, x1_ref, mod_ref, wa_ref, wb_ref, cwa_ref, cwb_ref, ba_ref, bb_ref, wd_ref,
                o_ref, tail_a, tail_b, acc_ref, *, tm):
    @pl.when(pl.program_id(1) == 0)
    def _():
        tail_a[...] = jnp.zeros_like(tail_a)
        tail_b[...] = jnp.zeros_like(tail_b)

    h2 = h2_ref[...]
    acc_ref[...] = jnp.zeros_like(acc_ref)
    n_chunks = wa_ref.shape[0]

    def conv(u, tail_ref, cw, bias, c):
        tail = tail_ref[c]
        y = (cw[FFN_CONV - 1:FFN_CONV] * u + bias)
        for j in range(FFN_CONV - 1):
            y = y + cw[j:j + 1] * _shift_rows(tail, u, FFN_CONV - 1 - j)
        tail_ref[c] = u[tm - SUBLANES:]
        return y

    def chunk(c, carry):
        ua = jnp.dot(h2, wa_ref[c], preferred_element_type=F32)
        ub = jnp.dot(h2, wb_ref[c], preferred_element_type=F32)
        a = conv(ua, tail_a, cwa_ref[c], ba_ref[c], c)
        b = conv(ub, tail_b, cwb_ref[c], bb_ref[c], c)
        act = (_silu(a) * b).astype(BF16)
        acc_ref[...] += jnp.dot(act, wd_ref[c], preferred_element_type=F32)
        return carry

    lax.fori_loop(0, n_chunks, chunk, 0)
    o_ref[...] = x1_ref[...] + mod_ref[0][5:6] * acc_ref[...]


def _ffn(h2, x1, mod3, wa, wb, cwa, cwb, ba, bb, wd, bsz, seq):
    t, d = x1.shape
    tm = 512
    per_b = seq // tm
    nc, _, cw = wa.shape
    full = lambda a: pl.BlockSpec(a.shape, lambda b, s: (0,) * a.ndim)
    return pl.pallas_call(
        functools.partial(_ffn_kernel, tm=tm),
        grid=(bsz, per_b),
        in_specs=[pl.BlockSpec((tm, d), lambda b, s: (b * per_b + s, 0)),
                  pl.BlockSpec((tm, d), lambda b, s: (b * per_b + s, 0)),
                  pl.BlockSpec((1, 6, d), lambda b, s: (b, 0, 0)),
                  full(wa), full(wb), full(cwa), full(cwb), full(ba), full(bb), full(wd)],
        out_specs=pl.BlockSpec((tm, d), lambda b, s: (b * per_b + s, 0)),
        out_shape=jax.ShapeDtypeStruct((t, d), F32),
        scratch_shapes=[pltpu.VMEM((nc, SUBLANES, cw), F32),
                        pltpu.VMEM((nc, SUBLANES, cw), F32),
                        pltpu.VMEM((tm, d), F32)],
        compiler_params=_params(("arbitrary", "arbitrary")),
        name="ffn",
    )(h2, x1, mod3, wa, wb, cwa, cwb, ba, bb, wd)


def _pad_lanes(a, n):
    return jnp.pad(a, ((0, 0), (0, n - a.shape[1])))


def _layer(x, c, ada_w, ada_b, norm1_w, w_in, gdn_conv_w, gdn_A_log, gdn_dt_bias, gdn_out_norm_w,
           nsa_q_norm_w, nsa_k_norm_cmp, nsa_k_norm_slc, nsa_k_norm_win, cmp_k_pos, cmp_k_w1, cmp_k_w2,
           cmp_v_pos, cmp_v_w1, cmp_v_w2, w_out, norm2_w, ffn_w_up, ffn_conv_w, ffn_conv_b, ffn_w_down):
    bsz, seq, d = x.shape
    x2 = x.reshape(bsz * seq, d)
    mod3 = _ada(c, ada_w, ada_b).reshape(bsz, 6, d)

    o = np.cumsum([0, GDN_WIDTH, GDN_WIDTH, GDN_WIDTH, GDN_WIDTH, GDN_HEADS, GDN_HEADS, NSA_WIDTH,
                   HEAD_DIM, HEAD_DIM, HEAD_DIM, HEAD_DIM, HEAD_DIM, HEAD_DIM, N_GATES * NSA_HEADS])
    small = jnp.concatenate([w_in[:, o[4]:o[6]], w_in[:, o[13]:o[14]]], axis=1)
    w_in_p = jnp.concatenate([w_in[:, :o[4]], w_in[:, o[6]:o[13]], _pad_lanes(small, LANES)],
                             axis=1).astype(BF16)
    proj = _inproj(x2, mod3, norm1_w, w_in_p, seq)

    o_g = _gdn(proj, gdn_conv_w, _pad_lanes(gdn_A_log.reshape(1, -1), LANES),
               _pad_lanes(gdn_dt_bias.reshape(1, -1), LANES), gdn_out_norm_w, bsz, seq)

    qn, ksl, vsl, kwn, vwn = _nsaprep(proj, nsa_q_norm_w, nsa_k_norm_slc, nsa_k_norm_win, seq)
    kcmp, vcmp = _compress(proj, cmp_k_pos, cmp_k_w1.astype(BF16), cmp_k_w2.astype(BF16), nsa_k_norm_cmp,
                           cmp_v_pos, cmp_v_w1.astype(BF16), cmp_v_w2.astype(BF16), bsz, seq)
    o_n = _nsa(qn, kcmp, vcmp, ksl, vsl, kwn, vwn, proj, _overlap_matrix(seq), bsz, seq)

    x1, h2 = _outproj(x2, o_g, o_n, mod3, w_out.astype(BF16), norm2_w, seq)

    d_ff = ffn_w_down.shape[0]
    cw = 256
    nc = d_ff // cw
    assert nc * cw == d_ff
    split_w = lambda w: jnp.transpose(w.reshape(d, nc, cw), (1, 0, 2)).astype(BF16)
    split_v = lambda v: jnp.transpose(v.reshape(v.shape[0], nc, cw), (1, 0, 2))
    out = _ffn(h2, x1, mod3, split_w(ffn_w_up[:, :d_ff]), split_w(ffn_w_up[:, d_ff:]),
               split_v(ffn_conv_w[:, :d_ff]), split_v(ffn_conv_w[:, d_ff:]),
               split_v(ffn_conv_b[None, :d_ff]), split_v(ffn_conv_b[None, d_ff:]),
               ffn_w_down.reshape(nc, cw, d).astype(BF16), bsz, seq)
    return out.reshape(bsz, seq, d)


def kernel(x, c, ada_w, ada_b, norm1_w, w_in, gdn_conv_w, gdn_A_log, gdn_dt_bias, gdn_out_norm_w, nsa_q_norm_w, nsa_k_norm_cmp, nsa_k_norm_slc, nsa_k_norm_win, cmp_k_pos, cmp_k_w1, cmp_k_w2, cmp_v_pos, cmp_v_w1, cmp_v_w2, w_out, norm2_w, ffn_w_up, ffn_conv_w, ffn_conv_b, ffn_w_down):
    for l in range(ada_w.shape[0]):
        x = _layer(x, c, ada_w[l], ada_b[l], norm1_w[l], w_in[l], gdn_conv_w[l], gdn_A_log[l],
                   gdn_dt_bias[l], gdn_out_norm_w[l], nsa_q_norm_w[l], nsa_k_norm_cmp[l],
                   nsa_k_norm_slc[l], nsa_k_norm_win[l], cmp_k_pos[l], cmp_k_w1[l], cmp_k_w2[l],
                   cmp_v_pos[l], cmp_v_w1[l], cmp_v_w2[l], w_out[l], norm2_w[l], ffn_w_up[l],
                   ffn_conv_w[l], ffn_conv_b[l], ffn_w_down[l])
    return x
```

```python
import functools

import numpy as np
import jax
import jax.numpy as jnp
from jax import lax
from jax.experimental import pallas as pl
from jax.experimental.pallas import tpu as pltpu

F32 = jnp.float32
BF16 = jnp.bfloat16

HEAD_DIM = 128
GDN_HEADS = 4
NSA_HEADS = 4
GDN_WIDTH = GDN_HEADS * HEAD_DIM
NSA_WIDTH = NSA_HEADS * HEAD_DIM
GDN_CONV = 4
GDN_CHUNK = 64
CMP_BLOCK = 32
CMP_STRIDE = 16
SEL_BLOCK = 64
N_SELECT = 16
WINDOW = 512
FFN_CONV = 3
N_GATES = 3
EPS = 1e-6
NEG_INF = -1e30
BIG = 3e38
LOG2E = 1.4426950408889634

LANES = 128
SUBLANES = 8
VMEM_LIMIT = 56 * 1024 * 1024

COL_GQ, COL_GK, COL_GV, COL_GZ = 0, 512, 1024, 1536
COL_NQ = 2048
COL_KC, COL_VC, COL_KSL, COL_VSL, COL_KWN, COL_VWN = 2560, 2688, 2816, 2944, 3072, 3200
COL_SMALL = 3328
N_PROJ = 3456
LANE_A, LANE_BETA, LANE_GATE = 0, 4, 8


def _params(sem):
    return pltpu.CompilerParams(dimension_semantics=sem, vmem_limit_bytes=VMEM_LIMIT)


def _dot(a, b):
    return jnp.dot(a.astype(BF16), b.astype(BF16), preferred_element_type=F32)


def _dot_nt(a, b):
    return lax.dot_general(a.astype(BF16), b.astype(BF16), (((1,), (1,)), ((), ())),
                           preferred_element_type=F32)


def _dot_tn(a, b):
    return lax.dot_general(a.astype(BF16), b.astype(BF16), (((0,), (0,)), ((), ())),
                           preferred_element_type=F32)


def _split2(a):
    hi = a.astype(BF16)
    lo = (a - hi.astype(F32)).astype(BF16)
    return hi, lo


def _split3(a):
    h1 = a.astype(BF16)
    r = a - h1.astype(F32)
    h2 = r.astype(BF16)
    h3 = (r - h2.astype(F32)).astype(BF16)
    return h1, h2, h3


def _dot_hp(a, b):
    ah, al = _split2(a)
    bh, bl = _split2(b)
    d = functools.partial(jnp.dot, preferred_element_type=F32)
    return d(ah, bh) + (d(ah, bl) + d(al, bh))


def _dot_exact_lhs(a_exact, b):
    a = a_exact.astype(BF16)
    b1, b2, b3 = _split3(b)
    d = functools.partial(jnp.dot, preferred_element_type=F32)
    return d(a, b1) + (d(a, b2) + d(a, b3))


def _sigmoid(x):
    return 1.0 / (1.0 + jnp.exp(-x))


def _silu(x):
    return x * _sigmoid(x)


def _softplus(x):
    return jnp.maximum(x, 0.0) + jnp.log1p(jnp.exp(-jnp.abs(x)))


def _ada_kernel(c_ref, w_ref, b_ref, o_ref):
    c = c_ref[...]
    o_ref[...] = _dot_hp(_silu(c), w_ref[...]) + b_ref[...]


def _ada(c, ada_w, ada_b):
    bsz, d = c.shape
    n = ada_w.shape[1]
    tn = 1024
    return pl.pallas_call(
        _ada_kernel,
        grid=(n // tn,),
        in_specs=[pl.BlockSpec((bsz, d), lambda j: (0, 0)),
                  pl.BlockSpec((d, tn), lambda j: (0, j)),
                  pl.BlockSpec((1, tn), lambda j: (0, j))],
        out_specs=pl.BlockSpec((bsz, tn), lambda j: (0, j)),
        out_shape=jax.ShapeDtypeStruct((bsz, n), F32),
        compiler_params=_params(("arbitrary",)),
        name="ada",
    )(c, ada_w, ada_b.reshape(1, n))


def _rms_mod(x, nw, scale, shift):
    y = x * lax.rsqrt(jnp.mean(x * x, axis=-1, keepdims=True) + EPS) * nw
    return y * (1.0 + scale) + shift


def _inproj_kernel(x_ref, mod_ref, nw_ref, w_ref, o_ref, *, tn):
    m = mod_ref[0]
    h = _rms_mod(x_ref[...], nw_ref[...], m[1:2], m[0:1]).astype(BF16)
    for j in range(o_ref.shape[1] // tn):
        o_ref[:, j * tn:(j + 1) * tn] = jnp.dot(h, w_ref[:, j * tn:(j + 1) * tn],
                                                 preferred_element_type=F32)


def _inproj(x2, mod3, norm1_w, w_in_p, seq):
    t, d = x2.shape
    tm = 512
    per_b = seq // tm
    return pl.pallas_call(
        functools.partial(_inproj_kernel, tn=1152),
        grid=(t // tm,),
        in_specs=[pl.BlockSpec((tm, d), lambda i: (i, 0)),
                  pl.BlockSpec((1, 6, d), lambda i: (i // per_b, 0, 0)),
                  pl.BlockSpec((1, d), lambda i: (0, 0)),
                  pl.BlockSpec((d, N_PROJ), lambda i: (0, 0))],
        out_specs=pl.BlockSpec((tm, N_PROJ), lambda i: (i, 0)),
        out_shape=jax.ShapeDtypeStruct((t, N_PROJ), F32),
        compiler_params=_params(("arbitrary",)),
        name="inproj",
    )(x2, mod3, norm1_w.reshape(1, d), w_in_p)


def _shift_rows(halo, x, k):
    xe = jnp.concatenate([halo, x], axis=0)
    return pltpu.roll(xe, k, axis=0)[halo.shape[0]:]


def _gdn_kernel(q_ref, k_ref, v_ref, z_ref, sm_ref, cw_ref, alog_ref, dtb_ref, onw_ref, o_ref,
                halo_ref, state_ref, pq_ref, p2_ref, oi_ref, os_ref, *, tile):
    c_len = GDN_CHUNK
    n_chunks = tile // c_len

    @pl.when(pl.program_id(1) == 0)
    def _():
        halo_ref[...] = jnp.zeros_like(halo_ref)
        state_ref[...] = jnp.zeros_like(state_ref)

    def conv_silu(x_ref, idx):
        x = x_ref[...]
        halo = halo_ref[idx]
        cw = cw_ref[:, idx * GDN_WIDTH:(idx + 1) * GDN_WIDTH]
        y = cw[GDN_CONV - 1:GDN_CONV] * x
        for j in range(GDN_CONV - 1):
            y = y + cw[j:j + 1] * _shift_rows(halo, x, GDN_CONV - 1 - j)
        halo_ref[idx] = x[tile - SUBLANES:]
        return _silu(y)

    qc = conv_silu(q_ref, 0)
    kc = conv_silu(k_ref, 1)
    vc = conv_silu(v_ref, 2)

    sm = sm_ref[...]
    g_all = -jnp.exp(alog_ref[...]) * _softplus(sm + dtb_ref[...])
    beta_all = _sigmoid(sm)

    ri = lax.broadcasted_iota(jnp.int32, (tile, tile), 0)
    ci = lax.broadcasted_iota(jnp.int32, (tile, tile), 1)
    lblk = jnp.where((ri // c_len == ci // c_len) & (ci <= ri), 1.0, 0.0)
    gc_all = _dot_exact_lhs(lblk, g_all)
    gc_t = gc_all.T
    gl_all = jnp.concatenate(
        [jnp.broadcast_to(gc_all[(c + 1) * c_len - 1:(c + 1) * c_len], (c_len, LANES))
         for c in range(n_chunks)], axis=0)
    eg_all = jnp.exp(gc_all)
    ekd_all = jnp.exp(gl_all - gc_all)
    egl_all = jnp.exp(gl_all)

    r64 = lax.broadcasted_iota(jnp.int32, (c_len, c_len), 0)
    c64 = lax.broadcasted_iota(jnp.int32, (c_len, c_len), 1)
    incl = r64 >= c64
    strict = r64 > c64
    eye = jnp.where(r64 == c64, 1.0, 0.0)

    def lane_bcast(x, lane):
        return jnp.broadcast_to(x[:, lane:lane + 1], (tile, HEAD_DIM))

    units = [(c, h) for c in range(n_chunks) for h in range(GDN_HEADS)]
    heads = []
    for h in range(GDN_HEADS):
        hl = slice(h * HEAD_DIM, (h + 1) * HEAD_DIM)
        qh = qc[:, hl]
        kh = kc[:, hl]
        qh = qh * lax.rsqrt(jnp.sum(qh * qh, axis=-1, keepdims=True) + EPS) * (HEAD_DIM ** -0.5)
        kh = kh * lax.rsqrt(jnp.sum(kh * kh, axis=-1, keepdims=True) + EPS)
        bb = lane_bcast(beta_all, LANE_BETA + h)
        eg = lane_bcast(eg_all, LANE_A + h)
        kb = kh * bb
        heads.append(dict(qd=qh * eg, k=kh, kb=kb, q=qh, vb=vc[:, hl] * bb, kbe=kb * eg,
                          kd=kh * lane_bcast(ekd_all, LANE_A + h),
                          egl=lane_bcast(egl_all, LANE_A + h)))

    a_mats, attns = [], []
    for c, h in units:
        rows = slice(c * c_len, (c + 1) * c_len)
        hd = heads[h]
        gcol = gc_all[rows, LANE_A + h:LANE_A + h + 1]
        grow = gc_t[LANE_A + h:LANE_A + h + 1, rows]
        decay = jnp.exp(jnp.where(incl, gcol - grow, NEG_INF))
        gram = _dot_nt(jnp.concatenate([hd["kb"][rows], hd["q"][rows]], axis=0), hd["k"][rows])
        a_mats.append(jnp.where(strict, gram[:c_len] * decay, 0.0))
        attns.append(jnp.where(incl, gram[c_len:] * decay, 0.0))
    def hp(a, b):
        d = functools.partial(jnp.dot, preferred_element_type=F32)
        return d(a[0], b[0]) + (d(a[0], b[1]) + d(a[1], b[0]))

    xps = [-a for a in a_mats]
    t_mats = [eye + x for x in xps]
    x_splits = [_split2(x) for x in xps]
    for level in range(5):
        xps = [hp(xs, xs) for xs in x_splits]
        x_splits = [_split2(x) for x in xps]
        t_mats = [t + hp(_split2(t), xs) for t, xs in zip(t_mats, x_splits)]
    for i, (c, h) in enumerate(units):
        rows = slice(c * c_len, (c + 1) * c_len)
        hd = heads[h]
        uw = _dot(t_mats[i], jnp.concatenate([hd["vb"][rows], hd["kbe"][rows]], axis=1))
        aw = _dot(attns[i], uw)
        kw = _dot_tn(hd["kd"][rows], uw)
        pq_ref[i, :HEAD_DIM] = kw[:, HEAD_DIM:]
        pq_ref[i, HEAD_DIM:] = hd["qd"][rows] - aw[:, HEAD_DIM:]
        p2_ref[i] = kw[:, :HEAD_DIM]
        oi_ref[i] = aw[:, :HEAD_DIM]

    for i, (c, h) in enumerate(units):
        rows = slice(c * c_len, (c + 1) * c_len)
        state = state_ref[h]
        r = _dot(pq_ref[i], state)
        os_ref[rows, h * HEAD_DIM:(h + 1) * HEAD_DIM] = r[HEAD_DIM:] + oi_ref[i]
        egl = heads[h]["egl"][c * c_len:c * c_len + 1]
        state_ref[h] = state * egl - r[:HEAD_DIM] + p2_ref[i]

    for h in range(GDN_HEADS):
        hl = slice(h * HEAD_DIM, (h + 1) * HEAD_DIM)
        o = os_ref[:, hl]
        o = o * lax.rsqrt(jnp.mean(o * o, axis=-1, keepdims=True) + EPS) * onw_ref[...]
        o_ref[:, hl] = o * _silu(z_ref[:, hl])


def _gdn(proj, gdn_conv_w, alog_p, dtb_p, out_norm_w, bsz, seq):
    tile = 256
    per_b = seq // tile
    n_units = tile // GDN_CHUNK * GDN_HEADS
    row = lambda b, s: b * per_b + s
    wide = GDN_WIDTH // LANES * LANES
    return pl.pallas_call(
        functools.partial(_gdn_kernel, tile=tile),
        grid=(bsz, per_b),
        in_specs=[pl.BlockSpec((tile, wide), lambda b, s: (row(b, s), COL_GQ // wide)),
                  pl.BlockSpec((tile, wide), lambda b, s: (row(b, s), COL_GK // wide)),
                  pl.BlockSpec((tile, wide), lambda b, s: (row(b, s), COL_GV // wide)),
                  pl.BlockSpec((tile, wide), lambda b, s: (row(b, s), COL_GZ // wide)),
                  pl.BlockSpec((tile, LANES), lambda b, s: (row(b, s), COL_SMALL // LANES)),
                  pl.BlockSpec((GDN_CONV, 3 * GDN_WIDTH), lambda b, s: (0, 0)),
                  pl.BlockSpec((1, LANES), lambda b, s: (0, 0)),
                  pl.BlockSpec((1, LANES), lambda b, s: (0, 0)),
                  pl.BlockSpec((1, HEAD_DIM), lambda b, s: (0, 0))],
        out_specs=pl.BlockSpec((tile, GDN_WIDTH), lambda b, s: (row(b, s), 0)),
        out_shape=jax.ShapeDtypeStruct((bsz * seq, GDN_WIDTH), F32),
        scratch_shapes=[pltpu.VMEM((3, SUBLANES, GDN_WIDTH), F32),
                        pltpu.VMEM((GDN_HEADS, HEAD_DIM, HEAD_DIM), F32),
                        pltpu.VMEM((n_units, HEAD_DIM + GDN_CHUNK, HEAD_DIM), F32),
                        pltpu.VMEM((n_units, HEAD_DIM, HEAD_DIM), F32),
                        pltpu.VMEM((n_units, GDN_CHUNK, HEAD_DIM), F32),
                        pltpu.VMEM((tile, GDN_WIDTH), F32)],
        compiler_params=_params(("arbitrary", "arbitrary")),
        name="gdn",
    )(proj, proj, proj, proj, proj, gdn_conv_w, alog_p, dtb_p, out_norm_w.reshape(1, HEAD_DIM))


def _rms(x, w):
    return x * lax.rsqrt(jnp.mean(x * x, axis=-1, keepdims=True) + EPS) * w


def _nsaprep_kernel(q_ref, ksl_ref, vsl_ref, kwn_ref, vwn_ref, qw_ref, kslw_ref, kwnw_ref,
                    qo_ref, kslo_ref, vslo_ref, kwno_ref, vwno_ref, *, seq):
    tm = q_ref.shape[0]
    for h in range(NSA_HEADS):
        hl = slice(h * HEAD_DIM, (h + 1) * HEAD_DIM)
        qo_ref[:, hl] = (_rms(q_ref[:, hl], qw_ref[...]) * (HEAD_DIM ** -0.5 * LOG2E)).astype(BF16)
    kslo_ref[:, :HEAD_DIM] = _rms(ksl_ref[...], kslw_ref[...]).astype(BF16)
    pos = (pl.program_id(0) * tm + lax.broadcasted_iota(jnp.int32, (tm, LANES), 0)) % seq
    lane = lax.broadcasted_iota(jnp.int32, (tm, LANES), 1)
    kslo_ref[:, HEAD_DIM:] = jnp.where(lane == pos // SEL_BLOCK, NEG_INF, 0.0).astype(BF16)
    kwno_ref[...] = _rms(kwn_ref[...], kwnw_ref[...]).astype(BF16)
    vslo_ref[:, :HEAD_DIM] = vsl_ref[...].astype(BF16)
    vslo_ref[:, HEAD_DIM:] = jnp.ones((tm, LANES), BF16)
    vwno_ref[:, :HEAD_DIM] = vwn_ref[...].astype(BF16)
    vwno_ref[:, HEAD_DIM:] = jnp.ones((tm, LANES), BF16)


def _nsaprep(proj, q_w, ksl_w, kwn_w, seq):
    t = proj.shape[0]
    tm = 1024
    assert seq // SEL_BLOCK <= LANES
    col = lambda c: pl.BlockSpec((tm, HEAD_DIM), lambda i: (i, c // HEAD_DIM))
    vec = pl.BlockSpec((1, HEAD_DIM), lambda i: (0, 0))
    kv_out = pl.BlockSpec((tm, HEAD_DIM), lambda i: (i, 0))
    kv_shape = jax.ShapeDtypeStruct((t, HEAD_DIM), BF16)
    ext_out = pl.BlockSpec((tm, HEAD_DIM + LANES), lambda i: (i, 0))
    ext_shape = jax.ShapeDtypeStruct((t, HEAD_DIM + LANES), BF16)
    return pl.pallas_call(
        functools.partial(_nsaprep_kernel, seq=seq),
        grid=(t // tm,),
        in_specs=[pl.BlockSpec((tm, NSA_WIDTH), lambda i: (i, COL_NQ // NSA_WIDTH)),
                  col(COL_KSL), col(COL_VSL), col(COL_KWN), col(COL_VWN), vec, vec, vec],
        out_specs=[pl.BlockSpec((tm, NSA_WIDTH), lambda i: (i, 0)), ext_out, ext_out, kv_out, ext_out],
        out_shape=[jax.ShapeDtypeStruct((t, NSA_WIDTH), BF16), ext_shape, ext_shape, kv_shape, ext_shape],
        compiler_params=_params(("arbitrary",)),
        name="nsaprep",
    )(proj, proj, proj, proj, proj, q_w.reshape(1, -1), ksl_w.reshape(1, -1), kwn_w.reshape(1, -1))


def _compress_kernel(kc_ref, vc_ref, kpos_ref, kw1_ref, kw2_ref, knw_ref, vpos_ref, vw1_ref, vw2_ref,
                     ko_ref, vo_ref, *, n_cmp):
    ncp = ko_ref.shape[0]
    half = CMP_BLOCK // 2

    def one(x_ref, pos_ref, w1_ref, w2_ref):
        p = jnp.zeros((ncp, HEAD_DIM), F32)
        q = jnp.zeros((ncp, HEAD_DIM), F32)
        for j in range(half):
            a = x_ref[pl.ds(j, ncp, stride=CMP_STRIDE), :]
            p = p + _dot(a + pos_ref[j:j + 1, :], w1_ref[j * HEAD_DIM:(j + 1) * HEAD_DIM, :])
            q = q + _dot(a + pos_ref[half + j:half + j + 1, :],
                         w1_ref[(half + j) * HEAD_DIM:(half + j + 1) * HEAD_DIM, :])
        hid = _silu(p + pltpu.roll(q, ncp - 1, axis=0))
        out = _dot(hid, w2_ref[...])
        valid = lax.broadcasted_iota(jnp.int32, (ncp, HEAD_DIM), 0) < n_cmp
        return jnp.where(valid, out, 0.0)

    k = one(kc_ref, kpos_ref, kw1_ref, kw2_ref)
    ko_ref[...] = jnp.where(lax.broadcasted_iota(jnp.int32, k.shape, 0) < n_cmp,
                            _rms(k, knw_ref[...]), 0.0).astype(BF16)
    vo_ref[...] = one(vc_ref, vpos_ref, vw1_ref, vw2_ref).astype(BF16)


def _compress(proj, kpos, kw1, kw2, knw, vpos, vw1, vw2, bsz, seq):
    ncp = seq // CMP_STRIDE
    n_cmp = (seq - CMP_BLOCK) // CMP_STRIDE + 1
    full = lambda a: pl.BlockSpec(a.shape, lambda b: (0,) * a.ndim)
    knw2 = knw.reshape(1, HEAD_DIM)
    out = pl.BlockSpec((None, ncp, HEAD_DIM), lambda b: (b, 0, 0))
    shp = jax.ShapeDtypeStruct((bsz, ncp, HEAD_DIM), BF16)
    return pl.pallas_call(
        functools.partial(_compress_kernel, n_cmp=n_cmp),
        grid=(bsz,),
        in_specs=[pl.BlockSpec((seq, HEAD_DIM), lambda b: (b, COL_KC // HEAD_DIM)),
                  pl.BlockSpec((seq, HEAD_DIM), lambda b: (b, COL_VC // HEAD_DIM)),
                  full(kpos), full(kw1), full(kw2), full(knw2), full(vpos), full(vw1), full(vw2)],
        out_specs=[out, out],
        out_shape=[shp, shp],
        compiler_params=_params(("arbitrary",)),
        name="compress",
    )(proj, proj, kpos, kw1, kw2, knw2, vpos, vw1, vw2)


def _nsa_kernel(q_ref, kc_ref, vc_ref, ks_ref, vs_ref, kw_ref, vw_ref, sm_ref, ovt_ref, o_ref,
                m_ref, acc_ref, *, tq, tk, n_pick):
    qi = pl.program_id(1)
    t0 = qi * tq
    nh = NSA_HEADS
    ncp = kc_ref.shape[0]
    nsp = ovt_ref.shape[0]
    q = q_ref[...]
    q2 = jnp.concatenate([q[:, h * HEAD_DIM:(h + 1) * HEAD_DIM] for h in range(nh)], axis=0)
    gates = _sigmoid(sm_ref[...])

    def gate(h, g):
        c = LANE_GATE + N_GATES * h + g
        return gates[:, c:c + 1]

    def head(x, h):
        return x[h * tq:(h + 1) * tq]

    t_c = t0 + lax.broadcasted_iota(jnp.int32, (tq, ncp), 0)
    cend = lax.broadcasted_iota(jnp.int32, (tq, ncp), 1) * CMP_STRIDE + (CMP_BLOCK - 1)
    bias_c = jnp.where(cend <= t_c, 0.0, NEG_INF)
    sees_any = (t0 + lax.broadcasted_iota(jnp.int32, (tq, 1), 0)) >= CMP_BLOCK - 1
    s_c = _dot_nt(q2, kc_ref[...])
    vcmp = vc_ref[...]
    o_c = []
    psum = jnp.zeros((tq, ncp), F32)
    for h in range(nh):
        s = head(s_c, h) + bias_c
        p = jnp.exp2(s - jnp.max(s, axis=-1, keepdims=True))
        l = jnp.sum(p, axis=-1, keepdims=True)
        p = p * jnp.where(sees_any, 1.0 / l, 0.0)
        psum = psum + p
        o_c.append(_dot(p, vcmp))
    ph, pl_ = _split2(psum)
    ovt = ovt_ref[...]
    dnt = functools.partial(lax.dot_general, dimension_numbers=(((1,), (1,)), ((), ())),
                            preferred_element_type=F32)
    imp_t = dnt(ovt, ph) + dnt(ovt, pl_)

    wlen = WINDOW + tq
    start = pl.multiple_of(jnp.maximum(t0 - WINDOW, 0), tq)
    kw = kw_ref[pl.ds(start, wlen), :]
    vw = vw_ref[pl.ds(start, wlen), :]
    s_w = _dot_nt(q2, kw)
    kpos_w = start + lax.broadcasted_iota(jnp.int32, (tq, wlen), 1)
    t_w = t0 + lax.broadcasted_iota(jnp.int32, (tq, wlen), 0)
    bias_w = jnp.where(kpos_w <= t_w, jnp.where(kpos_w > t_w - WINDOW, 0.0, NEG_INF), NEG_INF)
    o_w = []
    for h in range(nh):
        s = head(s_w, h) + bias_w
        p = jnp.exp2(s - jnp.max(s, axis=-1, keepdims=True))
        pv = _dot(p, vw)
        o_w.append(pv[:, :HEAD_DIM] / pv[:, HEAD_DIM:])

    jj = lax.broadcasted_iota(jnp.int32, (nsp, tq), 0)
    cur = (t0 + lax.broadcasted_iota(jnp.int32, (nsp, tq), 1)) // SEL_BLOCK
    forced = (jj == 0) | (jj == cur) | (jj == cur - 1)
    valid = jj <= cur
    notsel = jnp.where(valid & forced, 0.0, 1.0)
    work = jnp.where(valid, jnp.where(forced, -BIG, imp_t), -BIG)
    jf = jj.astype(F32)
    for _ in range(n_pick - 3):
        mx = jnp.max(work, axis=0, keepdims=True)
        cand = jnp.where(work == mx, jnp.where(mx > -1e38, jf, float(nsp)), float(nsp))
        first = jnp.min(cand, axis=0, keepdims=True)
        pick = jf == first
        notsel = jnp.where(pick, 0.0, notsel)
        work = jnp.where(pick, -BIG, work)
    notsel_q = notsel.T.astype(BF16)
    q2x = jnp.concatenate([q2, jnp.concatenate([notsel_q] * nh, axis=0)], axis=1)

    m_ref[...] = jnp.full(m_ref.shape, NEG_INF, F32)
    acc_ref[...] = jnp.zeros(acc_ref.shape, F32)
    n_kt = (t0 + tq - 1) // tk + 1

    def scores(kt):
        k0 = pl.multiple_of(kt * tk, tk)
        return _dot_nt(q2x, ks_ref[pl.ds(k0, tk), :])

    def probs(kt, s, causal):
        if causal:
            kpos = kt * tk + lax.broadcasted_iota(jnp.int32, (tq, tk), 1)
            t_s = t0 + lax.broadcasted_iota(jnp.int32, (tq, tk), 0)
            bias = jnp.where(kpos <= t_s, 0.0, NEG_INF)
        ps, alphas = [], []
        for h in range(nh):
            rows = slice(h * tq, (h + 1) * tq)
            sh = head(s, h) + bias if causal else head(s, h)
            m_prev = m_ref[rows]
            m_next = jnp.maximum(m_prev, jnp.max(sh, axis=-1, keepdims=True))
            alphas.append(jnp.exp2(m_prev - m_next))
            ps.append(jnp.exp2(sh - jnp.concatenate([m_next] * (tk // LANES), axis=1)).astype(BF16))
            m_ref[rows] = m_next
        return jnp.concatenate(ps, axis=0), jnp.concatenate(alphas, axis=0)

    def accumulate(kt, p, alpha):
        k0 = pl.multiple_of(kt * tk, tk)
        pv = jnp.dot(p, vs_ref[pl.ds(k0, tk), :], preferred_element_type=F32)
        acc_ref[...] = jnp.concatenate([alpha, alpha], axis=1) * acc_ref[...] + pv

    def sweep_body(kt, s_cur):
        s_next = scores(kt + 1)
        accumulate(kt, *probs(kt, s_cur, False))
        return s_next

    s_last = lax.fori_loop(0, n_kt - 1, sweep_body, scores(0))
    accumulate(n_kt - 1, *probs(n_kt - 1, s_last, True))

    for h in range(nh):
        rows = slice(h * tq, (h + 1) * tq)
        l_s = acc_ref[rows, HEAD_DIM:]
        o_s = acc_ref[rows, :HEAD_DIM] * jnp.where(l_s > 0.0, 1.0 / l_s, 0.0)
        o_ref[:, h * HEAD_DIM:(h + 1) * HEAD_DIM] = (gate(h, 0) * o_c[h] + gate(h, 1) * o_s
                                                     + gate(h, 2) * o_w[h])


def _nsa(qn, kcmp, vcmp, ksl, vsl, kwn, vwn, proj, overlap, bsz, seq):
    tq, tk = 256, 512
    per_b = seq // tq
    ncp = kcmp.shape[1]
    nsp = overlap.shape[0]
    n_pick = min(N_SELECT, seq // SEL_BLOCK)
    assert n_pick == N_SELECT and seq % tk == 0 and seq >= WINDOW + tq and nsp == LANES
    ext = HEAD_DIM + LANES
    seq_kv = lambda w: pl.BlockSpec((seq, w), lambda b, i: (b, 0))
    cmp_kv = lambda: pl.BlockSpec((None, ncp, HEAD_DIM), lambda b, i: (b, 0, 0))
    return pl.pallas_call(
        functools.partial(_nsa_kernel, tq=tq, tk=tk, n_pick=n_pick),
        grid=(bsz, per_b),
        in_specs=[pl.BlockSpec((tq, NSA_WIDTH), lambda b, i: (b * per_b + i, 0)),
                  cmp_kv(), cmp_kv(), seq_kv(ext), seq_kv(ext), seq_kv(HEAD_DIM), seq_kv(ext),
                  pl.BlockSpec((tq, LANES), lambda b, i: (b * per_b + i, COL_SMALL // LANES)),
                  pl.BlockSpec((nsp, ncp), lambda b, i: (0, 0))],
        out_specs=pl.BlockSpec((tq, NSA_WIDTH), lambda b, i: (b * per_b + i, 0)),
        out_shape=jax.ShapeDtypeStruct((bsz * seq, NSA_WIDTH), F32),
        scratch_shapes=[pltpu.VMEM((NSA_HEADS * tq, LANES), F32),
                        pltpu.VMEM((NSA_HEADS * tq, ext), F32)],
        compiler_params=_params(("arbitrary", "arbitrary")),
        name="nsa",
    )(qn, kcmp, vcmp, ksl, vsl, kwn, vwn, proj, overlap)


def _overlap_matrix(seq):
    n_cmp = (seq - CMP_BLOCK) // CMP_STRIDE + 1
    n_sel = seq // SEL_BLOCK
    ncp = seq // CMP_STRIDE
    nsp = -(-n_sel // LANES) * LANES
    cs = np.arange(n_cmp) * CMP_STRIDE
    ss = np.arange(n_sel) * SEL_BLOCK
    ov = np.minimum(cs[:, None] + CMP_BLOCK, ss[None, :] + SEL_BLOCK) - np.maximum(cs[:, None], ss[None, :])
    out = np.zeros((nsp, ncp), np.float32)
    out[:n_sel, :n_cmp] = (np.clip(ov, 0, None).astype(np.float32) / CMP_BLOCK).T
    return jnp.asarray(out, dtype=BF16)


def _outproj_kernel(x_ref, og_ref, on_ref, mod_ref, wo_ref, x1_ref):
    y = _dot(og_ref[...], wo_ref[:GDN_WIDTH, :]) + _dot(on_ref[...], wo_ref[GDN_WIDTH:, :])
    x1_ref[...] = x_ref[...] + mod_ref[0][2:3] * y


def _outproj(x2, o_g, o_n, mod3, w_out_b, seq):
    t, d = x2.shape
    tm = 512
    per_b = seq // tm
    return pl.pallas_call(
        _outproj_kernel,
        grid=(t // tm,),
        in_specs=[pl.BlockSpec((tm, d), lambda i: (i, 0)),
                  pl.BlockSpec((tm, GDN_WIDTH), lambda i: (i, 0)),
                  pl.BlockSpec((tm, NSA_WIDTH), lambda i: (i, 0)),
                  pl.BlockSpec((1, 6, d), lambda i: (i // per_b, 0, 0)),
                  pl.BlockSpec((d, d), lambda i: (0, 0))],
        out_specs=pl.BlockSpec((tm, d), lambda i: (i, 0)),
        out_shape=jax.ShapeDtypeStruct((t, d), F32),
        compiler_params=_params(("arbitrary",)),
        name="outproj",
    )(x2, o_g, o_n, mod3, w_out_b)


def _ffn_kernel(x1_ref, mod_ref, nw_ref, wu_ref, cw_ref, cb_ref, wd_ref, o_ref, up_ref, act_ref,
                *, tm, d_ff, cw, down_parts):
    head = SUBLANES

    @pl.when(pl.program_id(1) == 0)
    def _():
        up_ref[:, :head, :] = jnp.zeros((2, head, d_ff), F32)

    m = mod_ref[0]
    x1 = x1_ref[...]
    h2 = _rms_mod(x1, nw_ref[...], m[4:5], m[3:4]).astype(BF16)
    n_chunks = d_ff // cw
    y = None
    done = 0
    for c in range(n_chunks):
        cols = slice(c * cw, (c + 1) * cw)
        halves = []
        for g in range(2):
            wcols = slice(g * d_ff + c * cw, g * d_ff + (c + 1) * cw)
            up_ref[g, head:, cols] = jnp.dot(h2, wu_ref[:, wcols], preferred_element_type=F32)
            z = cb_ref[:, wcols]
            for j in range(FFN_CONV):
                tap = up_ref[g, pl.ds(head - (FFN_CONV - 1) + j, tm), cols]
                z = z + cw_ref[j:j + 1, wcols] * tap
            up_ref[g, :head, cols] = up_ref[g, tm:tm + head, cols]
            halves.append(z)
        act_ref[:, cols] = (_silu(halves[0]) * halves[1]).astype(BF16)
        if (c + 1) * down_parts // n_chunks > c * down_parts // n_chunks:
            part = slice(done * cw, (c + 1) * cw)
            d = jnp.dot(act_ref[:, part], wd_ref[part, :], preferred_element_type=F32)
            y = d if y is None else y + d
            done = c + 1
    o_ref[...] = x1 + m[5:6] * y


def _ffn(x1, mod3, norm2_w, w_up_b, conv_w, conv_b, w_down_b, bsz, seq):
    t, d = x1.shape
    d_ff = w_down_b.shape[0]
    tm, cw = 512, 256
    assert d_ff % cw == 0
    per_b = seq // tm
    full = lambda a: pl.BlockSpec(a.shape, lambda b, s: (0,) * a.ndim)
    conv_b2 = conv_b.reshape(1, -1)
    norm2_w2 = norm2_w.reshape(1, d)
    return pl.pallas_call(
        functools.partial(_ffn_kernel, tm=tm, d_ff=d_ff, cw=cw, down_parts=1),
        grid=(bsz, per_b),
        in_specs=[pl.BlockSpec((tm, d), lambda b, s: (b * per_b + s, 0)),
                  pl.BlockSpec((1, 6, d), lambda b, s: (b, 0, 0)),
                  full(norm2_w2), full(w_up_b), full(conv_w), full(conv_b2), full(w_down_b)],
        out_specs=pl.BlockSpec((tm, d), lambda b, s: (b * per_b + s, 0)),
        out_shape=jax.ShapeDtypeStruct((t, d), F32),
        scratch_shapes=[pltpu.VMEM((2, SUBLANES + tm, d_ff), F32),
                        pltpu.VMEM((tm, d_ff), BF16)],
        compiler_params=_params(("arbitrary", "arbitrary")),
        name="ffn",
    )(x1, mod3, norm2_w2, w_up_b, conv_w, conv_b2, w_down_b)


def _pad_lanes(a, n):
    return jnp.pad(a, ((0, 0), (0, n - a.shape[1])))


def _layer(x, c, ada_w, ada_b, norm1_w, w_in, gdn_conv_w, gdn_A_log, gdn_dt_bias, gdn_out_norm_w,
           nsa_q_norm_w, nsa_k_norm_cmp, nsa_k_norm_slc, nsa_k_norm_win, cmp_k_pos, cmp_k_w1, cmp_k_w2,
           cmp_v_pos, cmp_v_w1, cmp_v_w2, w_out, norm2_w, ffn_w_up, ffn_conv_w, ffn_conv_b, ffn_w_down):
    bsz, seq, d = x.shape
    x2 = x.reshape(bsz * seq, d)
    mod3 = _ada(c, ada_w, ada_b).reshape(bsz, 6, d)

    o = np.cumsum([0, GDN_WIDTH, GDN_WIDTH, GDN_WIDTH, GDN_WIDTH, GDN_HEADS, GDN_HEADS, NSA_WIDTH,
                   HEAD_DIM, HEAD_DIM, HEAD_DIM, HEAD_DIM, HEAD_DIM, HEAD_DIM, N_GATES * NSA_HEADS])
    small = jnp.concatenate([w_in[:, o[4]:o[6]], w_in[:, o[13]:o[14]]], axis=1)
    w_in_p = jnp.concatenate([w_in[:, :o[4]], w_in[:, o[6]:o[13]], _pad_lanes(small, LANES)],
                             axis=1).astype(BF16)
    proj = _inproj(x2, mod3, norm1_w, w_in_p, seq)

    o_g = _gdn(proj, gdn_conv_w, _pad_lanes(gdn_A_log.reshape(1, -1), LANES),
               _pad_lanes(gdn_dt_bias.reshape(1, -1), LANES), gdn_out_norm_w, bsz, seq)

    qn, ksl, vsl, kwn, vwn = _nsaprep(proj, nsa_q_norm_w, nsa_k_norm_slc, nsa_k_norm_win, seq)
    kcmp, vcmp = _compress(proj, cmp_k_pos, cmp_k_w1.astype(BF16), cmp_k_w2.astype(BF16), nsa_k_norm_cmp,
                           cmp_v_pos, cmp_v_w1.astype(BF16), cmp_v_w2.astype(BF16), bsz, seq)
    o_n = _nsa(qn, kcmp, vcmp, ksl, vsl, kwn, vwn, proj, _overlap_matrix(seq), bsz, seq)

    x1 = _outproj(x2, o_g, o_n, mod3, w_out.astype(BF16), seq)
    out = _ffn(x1, mod3, norm2_w, ffn_w_up.astype(BF16), ffn_conv_w, ffn_conv_b, ffn_w_down.astype(BF16),
               bsz, seq)
    return out.reshape(bsz, seq, d)


def kernel(x, c, ada_w, ada_b, norm1_w, w_in, gdn_conv_w, gdn_A_log, gdn_dt_bias, gdn_out_norm_w, nsa_q_norm_w, nsa_k_norm_cmp, nsa_k_norm_slc, nsa_k_norm_win, cmp_k_pos, cmp_k_w1, cmp_k_w2, cmp_v_pos, cmp_v_w1, cmp_v_w2, w_out, norm2_w, ffn_w_up, ffn_conv_w, ffn_conv_b, ffn_w_down):
    for l in range(ada_w.shape[0]):
        x = _layer(x, c, ada_w[l], ada_b[l], norm1_w[l], w_in[l], gdn_conv_w[l], gdn_A_log[l],
                   gdn_dt_bias[l], gdn_out_norm_w[l], nsa_q_norm_w[l], nsa_k_norm_cmp[l],
                   nsa_k_norm_slc[l], nsa_k_norm_win[l], cmp_k_pos[l], cmp_k_w1[l], cmp_k_w2[l],
                   cmp_v_pos[l], cmp_v_w1[l], cmp_v_w2[l], w_out[l], norm2_w[l], ffn_w_up[l],
                   ffn_conv_w[l], ffn_conv_b[l], ffn_w_down[l])
    return x
```

```python
import functools

import numpy as np
import jax
import jax.numpy as jnp
from jax import lax
from jax.experimental import pallas as pl
from jax.experimental.pallas import tpu as pltpu

F32 = jnp.float32
BF16 = jnp.bfloat16

HEAD_DIM = 128
GDN_HEADS = 4
NSA_HEADS = 4
GDN_WIDTH = GDN_HEADS * HEAD_DIM
NSA_WIDTH = NSA_HEADS * HEAD_DIM
GDN_CONV = 4
GDN_CHUNK = 64
CMP_BLOCK = 32
CMP_STRIDE = 16
SEL_BLOCK = 64
N_SELECT = 16
WINDOW = 512
FFN_CONV = 3
N_GATES = 3
EPS = 1e-6
NEG_INF = -1e30
BIG = 3e38
LOG2E = 1.4426950408889634

LANES = 128
SUBLANES = 8
VMEM_LIMIT = 56 * 1024 * 1024

COL_GQ, COL_GK, COL_GV, COL_GZ = 0, 512, 1024, 1536
COL_NQ = 2048
COL_KC, COL_VC, COL_KSL, COL_VSL, COL_KWN, COL_VWN = 2560, 2688, 2816, 2944, 3072, 3200
COL_SMALL = 3328
N_PROJ = 3456
LANE_A, LANE_BETA, LANE_GATE = 0, 4, 8


def _params(sem):
    return pltpu.CompilerParams(dimension_semantics=sem, vmem_limit_bytes=VMEM_LIMIT)


def _dot(a, b):
    return jnp.dot(a.astype(BF16), b.astype(BF16), preferred_element_type=F32)


def _dot_nt(a, b):
    return lax.dot_general(a.astype(BF16), b.astype(BF16), (((1,), (1,)), ((), ())),
                           preferred_element_type=F32)


def _dot_tn(a, b):
    return lax.dot_general(a.astype(BF16), b.astype(BF16), (((0,), (0,)), ((), ())),
                           preferred_element_type=F32)


def _split2(a):
    hi = a.astype(BF16)
    lo = (a - hi.astype(F32)).astype(BF16)
    return hi, lo


def _split3(a):
    h1 = a.astype(BF16)
    r = a - h1.astype(F32)
    h2 = r.astype(BF16)
    h3 = (r - h2.astype(F32)).astype(BF16)
    return h1, h2, h3


def _dot_hp(a, b):
    ah, al = _split2(a)
    bh, bl = _split2(b)
    d = functools.partial(jnp.dot, preferred_element_type=F32)
    return d(ah, bh) + (d(ah, bl) + d(al, bh))


def _dot_exact_lhs(a_exact, b):
    a = a_exact.astype(BF16)
    b1, b2, b3 = _split3(b)
    d = functools.partial(jnp.dot, preferred_element_type=F32)
    return d(a, b1) + (d(a, b2) + d(a, b3))


def _sigmoid(x):
    return 1.0 / (1.0 + jnp.exp(-x))


def _silu(x):
    return x * _sigmoid(x)


def _softplus(x):
    return jnp.maximum(x, 0.0) + jnp.log1p(jnp.exp(-jnp.abs(x)))


def _ada_kernel(c_ref, w_ref, b_ref, o_ref):
    c = c_ref[...]
    o_ref[...] = _dot_hp(_silu(c), w_ref[...]) + b_ref[...]


def _ada(c, ada_w, ada_b):
    bsz, d = c.shape
    n = ada_w.shape[1]
    tn = 1024
    return pl.pallas_call(
        _ada_kernel,
        grid=(n // tn,),
        in_specs=[pl.BlockSpec((bsz, d), lambda j: (0, 0)),
                  pl.BlockSpec((d, tn), lambda j: (0, j)),
                  pl.BlockSpec((1, tn), lambda j: (0, j))],
        out_specs=pl.BlockSpec((bsz, tn), lambda j: (0, j)),
        out_shape=jax.ShapeDtypeStruct((bsz, n), F32),
        compiler_params=_params(("arbitrary",)),
        name="ada",
    )(c, ada_w, ada_b.reshape(1, n))


def _rms_mod(x, nw, scale, shift):
    y = x * lax.rsqrt(jnp.mean(x * x, axis=-1, keepdims=True) + EPS) * nw
    return y * (1.0 + scale) + shift


def _rms(x, w):
    return x * lax.rsqrt(jnp.mean(x * x, axis=-1, keepdims=True) + EPS) * w


def _inproj_kernel(x_ref, mod_ref, nw_ref, w_ref, cw_ref, alog_ref, dtb_ref, qw_ref, kslw_ref, kwnw_ref,
                   gp_ref, sm_ref, cp_ref, qn_ref, kext_ref, vext_ref, kwn_ref, vwext_ref, pre_ref,
                   *, tm, seq):
    head = SUBLANES
    i = pl.program_id(0)

    @pl.when(i % (seq // tm) == 0)
    def _():
        pre_ref[:head, :] = jnp.zeros((head, pre_ref.shape[1]), F32)

    m = mod_ref[0]
    h = _rms_mod(x_ref[...], nw_ref[...], m[1:2], m[0:1]).astype(BF16)

    def proj(c0, width):
        return jnp.dot(h, w_ref[:, c0:c0 + width], preferred_element_type=F32)

    for idx in range(3):
        cols = slice(idx * GDN_WIDTH, (idx + 1) * GDN_WIDTH)
        pre_ref[head:, cols] = proj(COL_GQ + idx * GDN_WIDTH, GDN_WIDTH)
        y = cw_ref[GDN_CONV - 1:GDN_CONV, cols] * pre_ref[head:, cols]
        for j in range(GDN_CONV - 1):
            y = y + cw_ref[j:j + 1, cols] * pre_ref[pl.ds(head - (GDN_CONV - 1) + j, tm), cols]
        pre_ref[:head, cols] = pre_ref[tm:tm + head, cols]
        y = _silu(y)
        for hh in range(GDN_HEADS):
            hl = slice(hh * HEAD_DIM, (hh + 1) * HEAD_DIM)
            yh = y[:, hl]
            if idx == 0:
                yh = yh * (lax.rsqrt(jnp.sum(yh * yh, axis=-1, keepdims=True) + EPS) * HEAD_DIM ** -0.5)
            elif idx == 1:
                yh = yh * lax.rsqrt(jnp.sum(yh * yh, axis=-1, keepdims=True) + EPS)
            gp_ref[:, idx * GDN_WIDTH + hh * HEAD_DIM:idx * GDN_WIDTH + (hh + 1) * HEAD_DIM] = yh
    gp_ref[:, COL_GZ:COL_GZ + GDN_WIDTH] = _silu(proj(COL_GZ, GDN_WIDTH))

    nq = proj(COL_NQ, NSA_WIDTH)
    for hh in range(NSA_HEADS):
        hl = slice(hh * HEAD_DIM, (hh + 1) * HEAD_DIM)
        qn_ref[:, hl] = (_rms(nq[:, hl], qw_ref[...]) * (HEAD_DIM ** -0.5 * LOG2E)).astype(BF16)
    r = proj(COL_KC, 4 * HEAD_DIM)
    cp_ref[...] = r[:, :2 * HEAD_DIM]
    ones = jnp.ones((tm, LANES), BF16)
    pos = (i * tm + lax.broadcasted_iota(jnp.int32, (tm, LANES), 0)) % seq
    lane = lax.broadcasted_iota(jnp.int32, (tm, LANES), 1)
    kext_ref[:, :HEAD_DIM] = _rms(r[:, 2 * HEAD_DIM:3 * HEAD_DIM], kslw_ref[...]).astype(BF16)
    kext_ref[:, HEAD_DIM:] = jnp.where(lane == pos // SEL_BLOCK, NEG_INF, 0.0).astype(BF16)
    vext_ref[:, :HEAD_DIM] = r[:, 3 * HEAD_DIM:].astype(BF16)
    vext_ref[:, HEAD_DIM:] = ones
    r = proj(COL_KWN, 2 * HEAD_DIM + LANES)
    kwn_ref[...] = _rms(r[:, :HEAD_DIM], kwnw_ref[...]).astype(BF16)
    vwext_ref[:, :HEAD_DIM] = r[:, HEAD_DIM:2 * HEAD_DIM].astype(BF16)
    vwext_ref[:, HEAD_DIM:] = ones
    small = r[:, 2 * HEAD_DIM:]
    log_decay = -jnp.exp(alog_ref[...]) * _softplus(small + dtb_ref[...])
    sm_ref[...] = jnp.where(lane < LANE_BETA, log_decay, _sigmoid(small))


def _inproj(x2, mod3, norm1_w, w_in_p, gdn_conv_w, alog_p, dtb_p, q_w, ksl_w, kwn_w, seq):
    t, d = x2.shape
    tm = 512
    per_b = seq // tm
    assert seq // SEL_BLOCK <= LANES
    ext = HEAD_DIM + LANES
    full = lambda a: pl.BlockSpec(a.shape, lambda i: (0,) * a.ndim)
    row = lambda w: pl.BlockSpec((tm, w), lambda i: (i, 0))
    vecs = [q_w.reshape(1, -1), ksl_w.reshape(1, -1), kwn_w.reshape(1, -1)]
    out_widths = [(4 * GDN_WIDTH, F32), (LANES, F32), (2 * HEAD_DIM, F32), (NSA_WIDTH, BF16),
                  (ext, BF16), (ext, BF16), (HEAD_DIM, BF16), (ext, BF16)]
    return pl.pallas_call(
        functools.partial(_inproj_kernel, tm=tm, seq=seq),
        grid=(t // tm,),
        in_specs=[row(d),
                  pl.BlockSpec((1, 6, d), lambda i: (i // per_b, 0, 0)),
                  pl.BlockSpec((1, d), lambda i: (0, 0)),
                  full(w_in_p), full(gdn_conv_w), full(alog_p), full(dtb_p)] + [full(v) for v in vecs],
        out_specs=[row(w) for w, _ in out_widths],
        out_shape=[jax.ShapeDtypeStruct((t, w), dt) for w, dt in out_widths],
        scratch_shapes=[pltpu.VMEM((SUBLANES + tm, 3 * GDN_WIDTH), F32)],
        compiler_params=_params(("arbitrary",)),
        name="inproj",
    )(x2, mod3, norm1_w.reshape(1, d), w_in_p, gdn_conv_w, alog_p, dtb_p, *vecs)


def _gdn_kernel(q_ref, k_ref, v_ref, z_ref, sm_ref, onw_ref, o_ref,
                state_ref, pq_ref, p2_ref, oi_ref, os_ref, *, tile):
    c_len = GDN_CHUNK
    n_chunks = tile // c_len

    @pl.when(pl.program_id(1) == 0)
    def _():
        state_ref[...] = jnp.zeros_like(state_ref)

    qc = q_ref[...]
    kc = k_ref[...]
    vc = v_ref[...]
    g_all = sm_ref[...]
    beta_all = g_all

    ri = lax.broadcasted_iota(jnp.int32, (tile, tile), 0)
    ci = lax.broadcasted_iota(jnp.int32, (tile, tile), 1)
    lblk = jnp.where((ri // c_len == ci // c_len) & (ci <= ri), 1.0, 0.0)
    gc_all = _dot_exact_lhs(lblk, g_all)
    gc_t = gc_all.T
    gl_all = jnp.concatenate(
        [jnp.broadcast_to(gc_all[(c + 1) * c_len - 1:(c + 1) * c_len], (c_len, LANES))
         for c in range(n_chunks)], axis=0)
    eg_all = jnp.exp(gc_all)
    ekd_all = jnp.exp(gl_all - gc_all)
    egl_all = jnp.exp(gl_all)

    r64 = lax.broadcasted_iota(jnp.int32, (c_len, c_len), 0)
    c64 = lax.broadcasted_iota(jnp.int32, (c_len, c_len), 1)
    incl = r64 >= c64
    strict = r64 > c64
    eye = jnp.where(r64 == c64, 1.0, 0.0)

    def lane_bcast(x, lane):
        return jnp.broadcast_to(x[:, lane:lane + 1], (tile, HEAD_DIM))

    units = [(c, h) for c in range(n_chunks) for h in range(GDN_HEADS)]
    heads = []
    for h in range(GDN_HEADS):
        hl = slice(h * HEAD_DIM, (h + 1) * HEAD_DIM)
        qh = qc[:, hl]
        kh = kc[:, hl]
        bb = lane_bcast(beta_all, LANE_BETA + h)
        eg = lane_bcast(eg_all, LANE_A + h)
        kb = kh * bb
        heads.append(dict(qd=qh * eg, k=kh, kb=kb, q=qh, vb=vc[:, hl] * bb, kbe=kb * eg,
                          kd=kh * lane_bcast(ekd_all, LANE_A + h),
                          egl=lane_bcast(egl_all, LANE_A + h)))

    a_mats, attns = [], []
    for c, h in units:
        rows = slice(c * c_len, (c + 1) * c_len)
        hd = heads[h]
        gcol = gc_all[rows, LANE_A + h:LANE_A + h + 1]
        grow = gc_t[LANE_A + h:LANE_A + h + 1, rows]
        decay = jnp.exp(jnp.where(incl, gcol - grow, NEG_INF))
        gram = _dot_nt(jnp.concatenate([hd["kb"][rows], hd["q"][rows]], axis=0), hd["k"][rows])
        a_mats.append(jnp.where(strict, gram[:c_len] * decay, 0.0))
        attns.append(jnp.where(incl, gram[c_len:] * decay, 0.0))
    def hp(a, b):
        d = functools.partial(jnp.dot, preferred_element_type=F32)
        return d(a[0], b[0]) + (d(a[0], b[1]) + d(a[1], b[0]))

    xps = [-a for a in a_mats]
    t_mats = [eye + x for x in xps]
    x_splits = [_split2(x) for x in xps]
    for level in range(c_len.bit_length() - 2):
        xps = [hp(xs, xs) for xs in x_splits]
        x_splits = [_split2(x) for x in xps]
        t_mats = [t + hp(_split2(t), xs) for t, xs in zip(t_mats, x_splits)]
    for i, (c, h) in enumerate(units):
        rows = slice(c * c_len, (c + 1) * c_len)
        hd = heads[h]
        uw = _dot(t_mats[i], jnp.concatenate([hd["vb"][rows], hd["kbe"][rows]], axis=1))
        aw = _dot(attns[i], uw)
        kw = _dot_tn(hd["kd"][rows], uw)
        pq_ref[i, :HEAD_DIM] = kw[:, HEAD_DIM:]
        pq_ref[i, HEAD_DIM:] = hd["qd"][rows] - aw[:, HEAD_DIM:]
        p2_ref[i] = kw[:, :HEAD_DIM]
        oi_ref[i] = aw[:, :HEAD_DIM]

    for i, (c, h) in enumerate(units):
        rows = slice(c * c_len, (c + 1) * c_len)
        state = state_ref[h]
        r = _dot(pq_ref[i], state)
        os_ref[rows, h * HEAD_DIM:(h + 1) * HEAD_DIM] = r[HEAD_DIM:] + oi_ref[i]
        egl = heads[h]["egl"][c * c_len:c * c_len + 1]
        state_ref[h] = state * egl - r[:HEAD_DIM] + p2_ref[i]

    for h in range(GDN_HEADS):
        hl = slice(h * HEAD_DIM, (h + 1) * HEAD_DIM)
        o = os_ref[:, hl]
        o = o * lax.rsqrt(jnp.mean(o * o, axis=-1, keepdims=True) + EPS) * onw_ref[...]
        o_ref[:, hl] = o * z_ref[:, hl]


def _gdn(gp, sm, out_norm_w, bsz, seq):
    tile = 256
    per_b = seq // tile
    n_units = tile // GDN_CHUNK * GDN_HEADS
    col = lambda c: pl.BlockSpec((tile, GDN_WIDTH), lambda b, s: (b * per_b + s, c))
    return pl.pallas_call(
        functools.partial(_gdn_kernel, tile=tile),
        grid=(bsz, per_b),
        in_specs=[col(0), col(1), col(2), col(3),
                  pl.BlockSpec((tile, LANES), lambda b, s: (b * per_b + s, 0)),
                  pl.BlockSpec((1, HEAD_DIM), lambda b, s: (0, 0))],
        out_specs=pl.BlockSpec((tile, GDN_WIDTH), lambda b, s: (b * per_b + s, 0)),
        out_shape=jax.ShapeDtypeStruct((bsz * seq, GDN_WIDTH), F32),
        scratch_shapes=[pltpu.VMEM((GDN_HEADS, HEAD_DIM, HEAD_DIM), F32),
                        pltpu.VMEM((n_units, HEAD_DIM + GDN_CHUNK, HEAD_DIM), F32),
                        pltpu.VMEM((n_units, HEAD_DIM, HEAD_DIM), F32),
                        pltpu.VMEM((n_units, GDN_CHUNK, HEAD_DIM), F32),
                        pltpu.VMEM((tile, GDN_WIDTH), F32)],
        compiler_params=_params(("arbitrary", "arbitrary")),
        name="gdn",
    )(gp, gp, gp, gp, sm, out_norm_w.reshape(1, HEAD_DIM))


def _compress_kernel(kc_ref, vc_ref, kpos_ref, kw1_ref, kw2_ref, knw_ref, vpos_ref, vw1_ref, vw2_ref,
                     ko_ref, vo_ref, *, n_cmp):
    ncp = ko_ref.shape[0]
    half = CMP_BLOCK // 2

    def one(x_ref, pos_ref, w1_ref, w2_ref):
        p = jnp.zeros((ncp, HEAD_DIM), F32)
        q = jnp.zeros((ncp, HEAD_DIM), F32)
        for j in range(half):
            a = x_ref[pl.ds(j, ncp, stride=CMP_STRIDE), :]
            p = p + _dot(a + pos_ref[j:j + 1, :], w1_ref[j * HEAD_DIM:(j + 1) * HEAD_DIM, :])
            q = q + _dot(a + pos_ref[half + j:half + j + 1, :],
                         w1_ref[(half + j) * HEAD_DIM:(half + j + 1) * HEAD_DIM, :])
        hid = _silu(p + pltpu.roll(q, ncp - 1, axis=0))
        out = _dot(hid, w2_ref[...])
        valid = lax.broadcasted_iota(jnp.int32, (ncp, HEAD_DIM), 0) < n_cmp
        return jnp.where(valid, out, 0.0)

    k = one(kc_ref, kpos_ref, kw1_ref, kw2_ref)
    ko_ref[...] = jnp.where(lax.broadcasted_iota(jnp.int32, k.shape, 0) < n_cmp,
                            _rms(k, knw_ref[...]), 0.0).astype(BF16)
    vo_ref[...] = one(vc_ref, vpos_ref, vw1_ref, vw2_ref).astype(BF16)


def _compress(cp, kpos, kw1, kw2, knw, vpos, vw1, vw2, bsz, seq):
    ncp = seq // CMP_STRIDE
    n_cmp = (seq - CMP_BLOCK) // CMP_STRIDE + 1
    full = lambda a: pl.BlockSpec(a.shape, lambda b: (0,) * a.ndim)
    knw2 = knw.reshape(1, HEAD_DIM)
    out = pl.BlockSpec((None, ncp, HEAD_DIM), lambda b: (b, 0, 0))
    shp = jax.ShapeDtypeStruct((bsz, ncp, HEAD_DIM), BF16)
    return pl.pallas_call(
        functools.partial(_compress_kernel, n_cmp=n_cmp),
        grid=(bsz,),
        in_specs=[pl.BlockSpec((seq, HEAD_DIM), lambda b: (b, 0)),
                  pl.BlockSpec((seq, HEAD_DIM), lambda b: (b, 1)),
                  full(kpos), full(kw1), full(kw2), full(knw2), full(vpos), full(vw1), full(vw2)],
        out_specs=[out, out],
        out_shape=[shp, shp],
        compiler_params=_params(("arbitrary",)),
        name="compress",
    )(cp, cp, kpos, kw1, kw2, knw2, vpos, vw1, vw2)


def _nsa_kernel(q_ref, kc_ref, vc_ref, ks_ref, vs_ref, kw_ref, vw_ref, sm_ref, ovt_ref, o_ref,
                m_ref, acc_ref, *, tq, tk, n_pick):
    qi = pl.program_id(1)
    t0 = qi * tq
    nh = NSA_HEADS
    ncp = kc_ref.shape[0]
    nsp = ovt_ref.shape[0]
    q = q_ref[...]
    q2 = jnp.concatenate([q[:, h * HEAD_DIM:(h + 1) * HEAD_DIM] for h in range(nh)], axis=0)
    gates = sm_ref[...]

    def gate(h, g):
        c = LANE_GATE + N_GATES * h + g
        return gates[:, c:c + 1]

    def head(x, h):
        return x[h * tq:(h + 1) * tq]

    t_c = t0 + lax.broadcasted_iota(jnp.int32, (tq, ncp), 0)
    cend = lax.broadcasted_iota(jnp.int32, (tq, ncp), 1) * CMP_STRIDE + (CMP_BLOCK - 1)
    bias_c = jnp.where(cend <= t_c, 0.0, NEG_INF)
    sees_any = (t0 + lax.broadcasted_iota(jnp.int32, (tq, 1), 0)) >= CMP_BLOCK - 1
    s_c = _dot_nt(q2, kc_ref[...])
    vcmp = vc_ref[...]
    o_c = []
    psum = jnp.zeros((tq, ncp), F32)
    for h in range(nh):
        s = head(s_c, h) + bias_c
        p = jnp.exp2(s - jnp.max(s, axis=-1, keepdims=True))
        l = jnp.sum(p, axis=-1, keepdims=True)
        p = p * jnp.where(sees_any, 1.0 / l, 0.0)
        psum = psum + p
        o_c.append(_dot(p, vcmp))
    ph, pl_ = _split2(psum)
    ovt = ovt_ref[...]
    dnt = functools.partial(lax.dot_general, dimension_numbers=(((1,), (1,)), ((), ())),
                            preferred_element_type=F32)
    imp_t = dnt(ovt, ph) + dnt(ovt, pl_)

    wlen = WINDOW + tq
    start = pl.multiple_of(jnp.maximum(t0 - WINDOW, 0), tq)
    kw = kw_ref[pl.ds(start, wlen), :]
    vw = vw_ref[pl.ds(start, wlen), :]
    s_w = _dot_nt(q2, kw)
    kpos_w = start + lax.broadcasted_iota(jnp.int32, (tq, wlen), 1)
    t_w = t0 + lax.broadcasted_iota(jnp.int32, (tq, wlen), 0)
    bias_w = jnp.where(kpos_w <= t_w, jnp.where(kpos_w > t_w - WINDOW, 0.0, NEG_INF), NEG_INF)
    o_w = []
    for h in range(nh):
        s = head(s_w, h) + bias_w
        p = jnp.exp2(s - jnp.max(s, axis=-1, keepdims=True))
        pv = _dot(p, vw)
        o_w.append(pv[:, :HEAD_DIM] / pv[:, HEAD_DIM:])

    jj = lax.broadcasted_iota(jnp.int32, (nsp, tq), 0)
    cur = (t0 + lax.broadcasted_iota(jnp.int32, (nsp, tq), 1)) // SEL_BLOCK
    forced = (jj == 0) | (jj == cur) | (jj == cur - 1)
    valid = jj <= cur
    notsel = jnp.where(valid & forced, 0.0, 1.0)
    work = jnp.where(valid, jnp.where(forced, -BIG, imp_t), -BIG)
    jf = jj.astype(F32)
    for _ in range(n_pick - 3):
        mx = jnp.max(work, axis=0, keepdims=True)
        cand = jnp.where(work == mx, jnp.where(mx > -1e38, jf, float(nsp)), float(nsp))
        first = jnp.min(cand, axis=0, keepdims=True)
        pick = jf == first
        notsel = jnp.where(pick, 0.0, notsel)
        work = jnp.where(pick, -BIG, work)
    notsel_q = notsel.T.astype(BF16)
    q2x = jnp.concatenate([q2, jnp.concatenate([notsel_q] * nh, axis=0)], axis=1)

    m_ref[...] = jnp.full(m_ref.shape, NEG_INF, F32)
    acc_ref[...] = jnp.zeros(acc_ref.shape, F32)
    n_kt = (t0 + tq - 1) // tk + 1

    def scores(kt):
        k0 = pl.multiple_of(kt * tk, tk)
        return _dot_nt(q2x, ks_ref[pl.ds(k0, tk), :])

    def probs(kt, s, causal):
        if causal:
            kpos = kt * tk + lax.broadcasted_iota(jnp.int32, (tq, tk), 1)
            t_s = t0 + lax.broadcasted_iota(jnp.int32, (tq, tk), 0)
            bias = jnp.where(kpos <= t_s, 0.0, NEG_INF)
        ps, alphas = [], []
        for h in range(nh):
            rows = slice(h * tq, (h + 1) * tq)
            sh = head(s, h) + bias if causal else head(s, h)
            m_prev = m_ref[rows]
            m_next = jnp.maximum(m_prev, jnp.max(sh, axis=-1, keepdims=True))
            alphas.append(jnp.exp2(m_prev - m_next))
            ps.append(jnp.exp2(sh - jnp.concatenate([m_next] * (tk // LANES), axis=1)).astype(BF16))
            m_ref[rows] = m_next
        return jnp.concatenate(ps, axis=0), jnp.concatenate(alphas, axis=0)

    def accumulate(kt, p, alpha):
        k0 = pl.multiple_of(kt * tk, tk)
        pv = jnp.dot(p, vs_ref[pl.ds(k0, tk), :], preferred_element_type=F32)
        acc_ref[...] = jnp.concatenate([alpha, alpha], axis=1) * acc_ref[...] + pv

    def sweep_body(kt, s_cur):
        s_next = scores(kt + 1)
        accumulate(kt, *probs(kt, s_cur, False))
        return s_next

    s_last = lax.fori_loop(0, n_kt - 1, sweep_body, scores(0))
    accumulate(n_kt - 1, *probs(n_kt - 1, s_last, True))

    for h in range(nh):
        rows = slice(h * tq, (h + 1) * tq)
        l_s = acc_ref[rows, HEAD_DIM:]
        o_s = acc_ref[rows, :HEAD_DIM] * jnp.where(l_s > 0.0, 1.0 / l_s, 0.0)
        o_ref[:, h * HEAD_DIM:(h + 1) * HEAD_DIM] = (gate(h, 0) * o_c[h] + gate(h, 1) * o_s
                                                     + gate(h, 2) * o_w[h])


def _nsa(qn, kcmp, vcmp, ksl, vsl, kwn, vwn, sm, overlap, bsz, seq):
    tq, tk = 256, 512
    per_b = seq // tq
    ncp = kcmp.shape[1]
    nsp = overlap.shape[0]
    n_pick = min(N_SELECT, seq // SEL_BLOCK)
    assert n_pick == N_SELECT and seq % tk == 0 and seq >= WINDOW + tq and nsp == LANES
    ext = HEAD_DIM + LANES
    seq_kv = lambda w: pl.BlockSpec((seq, w), lambda b, i: (b, 0))
    cmp_kv = lambda: pl.BlockSpec((None, ncp, HEAD_DIM), lambda b, i: (b, 0, 0))
    return pl.pallas_call(
        functools.partial(_nsa_kernel, tq=tq, tk=tk, n_pick=n_pick),
        grid=(bsz, per_b),
        in_specs=[pl.BlockSpec((tq, NSA_WIDTH), lambda b, i: (b * per_b + i, 0)),
                  cmp_kv(), cmp_kv(), seq_kv(ext), seq_kv(ext), seq_kv(HEAD_DIM), seq_kv(ext),
                  pl.BlockSpec((tq, LANES), lambda b, i: (b * per_b + i, 0)),
                  pl.BlockSpec((nsp, ncp), lambda b, i: (0, 0))],
        out_specs=pl.BlockSpec((tq, NSA_WIDTH), lambda b, i: (b * per_b + i, 0)),
        out_shape=jax.ShapeDtypeStruct((bsz * seq, NSA_WIDTH), F32),
        scratch_shapes=[pltpu.VMEM((NSA_HEADS * tq, LANES), F32),
                        pltpu.VMEM((NSA_HEADS * tq, ext), F32)],
        compiler_params=_params(("arbitrary", "arbitrary")),
        name="nsa",
    )(qn, kcmp, vcmp, ksl, vsl, kwn, vwn, sm, overlap)


def _overlap_matrix(seq):
    n_cmp = (seq - CMP_BLOCK) // CMP_STRIDE + 1
    n_sel = seq // SEL_BLOCK
    ncp = seq // CMP_STRIDE
    nsp = -(-n_sel // LANES) * LANES
    cs = np.arange(n_cmp) * CMP_STRIDE
    ss = np.arange(n_sel) * SEL_BLOCK
    ov = np.minimum(cs[:, None] + CMP_BLOCK, ss[None, :] + SEL_BLOCK) - np.maximum(cs[:, None], ss[None, :])
    out = np.zeros((nsp, ncp), np.float32)
    out[:n_sel, :n_cmp] = (np.clip(ov, 0, None).astype(np.float32) / CMP_BLOCK).T
    return jnp.asarray(out, dtype=BF16)


def _outproj_kernel(x_ref, og_ref, on_ref, mod_ref, wo_ref, x1_ref):
    y = _dot(og_ref[...], wo_ref[:GDN_WIDTH, :]) + _dot(on_ref[...], wo_ref[GDN_WIDTH:, :])
    x1_ref[...] = x_ref[...] + mod_ref[0][2:3] * y


def _outproj(x2, o_g, o_n, mod3, w_out_b, seq):
    t, d = x2.shape
    tm = 512
    per_b = seq // tm
    return pl.pallas_call(
        _outproj_kernel,
        grid=(t // tm,),
        in_specs=[pl.BlockSpec((tm, d), lambda i: (i, 0)),
                  pl.BlockSpec((tm, GDN_WIDTH), lambda i: (i, 0)),
                  pl.BlockSpec((tm, NSA_WIDTH), lambda i: (i, 0)),
                  pl.BlockSpec((1, 6, d), lambda i: (i // per_b, 0, 0)),
                  pl.BlockSpec((d, d), lambda i: (0, 0))],
        out_specs=pl.BlockSpec((tm, d), lambda i: (i, 0)),
        out_shape=jax.ShapeDtypeStruct((t, d), F32),
        compiler_params=_params(("arbitrary",)),
        name="outproj",
    )(x2, o_g, o_n, mod3, w_out_b)


def _ffn_kernel(x1_ref, mod_ref, nw_ref, wu_ref, cw_ref, cb_ref, wd_ref, o_ref, up_ref, act_ref,
                *, tm, d_ff, cw, down_parts):
    head = SUBLANES

    @pl.when(pl.program_id(1) == 0)
    def _():
        up_ref[:, :head, :] = jnp.zeros((2, head, d_ff), F32)

    m = mod_ref[0]
    x1 = x1_ref[...]
    h2 = _rms_mod(x1, nw_ref[...], m[4:5], m[3:4]).astype(BF16)
    n_chunks = d_ff // cw
    y = None
    done = 0
    for c in range(n_chunks):
        cols = slice(c * cw, (c + 1) * cw)
        halves = []
        for g in range(2):
            wcols = slice(g * d_ff + c * cw, g * d_ff + (c + 1) * cw)
            up_ref[g, head:, cols] = jnp.dot(h2, wu_ref[:, wcols], preferred_element_type=F32)
            z = cb_ref[:, wcols]
            for j in range(FFN_CONV):
                tap = up_ref[g, pl.ds(head - (FFN_CONV - 1) + j, tm), cols]
                z = z + cw_ref[j:j + 1, wcols] * tap
            up_ref[g, :head, cols] = up_ref[g, tm:tm + head, cols]
            halves.append(z)
        act_ref[:, cols] = (_silu(halves[0]) * halves[1]).astype(BF16)
        if (c + 1) * down_parts // n_chunks > c * down_parts // n_chunks:
            part = slice(done * cw, (c + 1) * cw)
            d = jnp.dot(act_ref[:, part], wd_ref[part, :], preferred_element_type=F32)
            y = d if y is None else y + d
            done = c + 1
    o_ref[...] = x1 + m[5:6] * y


def _ffn(x1, mod3, norm2_w, w_up_b, conv_w, conv_b, w_down_b, bsz, seq):
    t, d = x1.shape
    d_ff = w_down_b.shape[0]
    tm, cw = 512, 256
    assert d_ff % cw == 0
    per_b = seq // tm
    full = lambda a: pl.BlockSpec(a.shape, lambda b, s: (0,) * a.ndim)
    conv_b2 = conv_b.reshape(1, -1)
    norm2_w2 = norm2_w.reshape(1, d)
    return pl.pallas_call(
        functools.partial(_ffn_kernel, tm=tm, d_ff=d_ff, cw=cw, down_parts=1),
        grid=(bsz, per_b),
        in_specs=[pl.BlockSpec((tm, d), lambda b, s: (b * per_b + s, 0)),
                  pl.BlockSpec((1, 6, d), lambda b, s: (b, 0, 0)),
                  full(norm2_w2), full(w_up_b), full(conv_w), full(conv_b2), full(w_down_b)],
        out_specs=pl.BlockSpec((tm, d), lambda b, s: (b * per_b + s, 0)),
        out_shape=jax.ShapeDtypeStruct((t, d), F32),
        scratch_shapes=[pltpu.VMEM((2, SUBLANES + tm, d_ff), F32),
                        pltpu.VMEM((tm, d_ff), BF16)],
        compiler_params=_params(("arbitrary", "arbitrary")),
        name="ffn",
    )(x1, mod3, norm2_w2, w_up_b, conv_w, conv_b2, w_down_b)


def _pad_lanes(a, n):
    return jnp.pad(a, ((0, 0), (0, n - a.shape[1])))


def _layer(x, c, ada_w, ada_b, norm1_w, w_in, gdn_conv_w, gdn_A_log, gdn_dt_bias, gdn_out_norm_w,
           nsa_q_norm_w, nsa_k_norm_cmp, nsa_k_norm_slc, nsa_k_norm_win, cmp_k_pos, cmp_k_w1, cmp_k_w2,
           cmp_v_pos, cmp_v_w1, cmp_v_w2, w_out, norm2_w, ffn_w_up, ffn_conv_w, ffn_conv_b, ffn_w_down):
    bsz, seq, d = x.shape
    x2 = x.reshape(bsz * seq, d)
    mod3 = _ada(c, ada_w, ada_b).reshape(bsz, 6, d)

    o = np.cumsum([0, GDN_WIDTH, GDN_WIDTH, GDN_WIDTH, GDN_WIDTH, GDN_HEADS, GDN_HEADS, NSA_WIDTH,
                   HEAD_DIM, HEAD_DIM, HEAD_DIM, HEAD_DIM, HEAD_DIM, HEAD_DIM, N_GATES * NSA_HEADS])
    small = jnp.concatenate([w_in[:, o[4]:o[6]], w_in[:, o[13]:o[14]]], axis=1)
    w_in_p = jnp.concatenate([w_in[:, :o[4]], w_in[:, o[6]:o[13]], _pad_lanes(small, LANES)],
                             axis=1).astype(BF16)
    gp, sm, cp, qn, ksl, vsl, kwn, vwn = _inproj(
        x2, mod3, norm1_w, w_in_p, gdn_conv_w, _pad_lanes(gdn_A_log.reshape(1, -1), LANES),
        _pad_lanes(gdn_dt_bias.reshape(1, -1), LANES), nsa_q_norm_w, nsa_k_norm_slc, nsa_k_norm_win, seq)

    o_g = _gdn(gp, sm, gdn_out_norm_w, bsz, seq)

    kcmp, vcmp = _compress(cp, cmp_k_pos, cmp_k_w1.astype(BF16), cmp_k_w2.astype(BF16), nsa_k_norm_cmp,
                           cmp_v_pos, cmp_v_w1.astype(BF16), cmp_v_w2.astype(BF16), bsz, seq)
    o_n = _nsa(qn, kcmp, vcmp, ksl, vsl, kwn, vwn, sm, _overlap_matrix(seq), bsz, seq)

    x1 = _outproj(x2, o_g, o_n, mod3, w_out.astype(BF16), seq)
    out = _ffn(x1, mod3, norm2_w, ffn_w_up.astype(BF16), ffn_conv_w, ffn_conv_b, ffn_w_down.astype(BF16),
               bsz, seq)
    return out.reshape(bsz, seq, d)


def kernel(x, c, ada_w, ada_b, norm1_w, w_in, gdn_conv_w, gdn_A_log, gdn_dt_bias, gdn_out_norm_w, nsa_q_norm_w, nsa_k_norm_cmp, nsa_k_norm_slc, nsa_k_norm_win, cmp_k_pos, cmp_k_w1, cmp_k_w2, cmp_v_pos, cmp_v_w1, cmp_v_w2, w_out, norm2_w, ffn_w_up, ffn_conv_w, ffn_conv_b, ffn_w_down):
    for l in range(ada_w.shape[0]):
        x = _layer(x, c, ada_w[l], ada_b[l], norm1_w[l], w_in[l], gdn_conv_w[l], gdn_A_log[l],
                   gdn_dt_bias[l], gdn_out_norm_w[l], nsa_q_norm_w[l], nsa_k_norm_cmp[l],
                   nsa_k_norm_slc[l], nsa_k_norm_win[l], cmp_k_pos[l], cmp_k_w1[l], cmp_k_w2[l],
                   cmp_v_pos[l], cmp_v_w1[l], cmp_v_w2[l], w_out[l], norm2_w[l], ffn_w_up[l],
                   ffn_conv_w[l], ffn_conv_b[l], ffn_w_down[l])
    return x
```

```python
import functools

import numpy as np
import jax
import jax.numpy as jnp
from jax import lax
from jax.experimental import pallas as pl
from jax.experimental.pallas import tpu as pltpu

F32 = jnp.float32
BF16 = jnp.bfloat16

HEAD_DIM = 128
GDN_HEADS = 4
NSA_HEADS = 4
GDN_WIDTH = GDN_HEADS * HEAD_DIM
NSA_WIDTH = NSA_HEADS * HEAD_DIM
GDN_CONV = 4
GDN_CHUNK = 64
CMP_BLOCK = 32
CMP_STRIDE = 16
SEL_BLOCK = 64
N_SELECT = 16
WINDOW = 512
FFN_CONV = 3
N_GATES = 3
EPS = 1e-6
NEG_INF = -1e30
BIG = 3e38
LOG2E = 1.4426950408889634

LANES = 128
SUBLANES = 8
VMEM_LIMIT = 56 * 1024 * 1024

COL_GQ, COL_GK, COL_GV, COL_GZ = 0, 512, 1024, 1536
COL_NQ = 2048
COL_KC, COL_VC, COL_KSL, COL_VSL, COL_KWN, COL_VWN = 2560, 2688, 2816, 2944, 3072, 3200
COL_SMALL = 3328
N_PROJ = 3456
LANE_A, LANE_BETA, LANE_GATE = 0, 4, 8


def _params(sem):
    return pltpu.CompilerParams(dimension_semantics=sem, vmem_limit_bytes=VMEM_LIMIT)


def _dot(a, b):
    return jnp.dot(a.astype(BF16), b.astype(BF16), preferred_element_type=F32)


def _dot_nt(a, b):
    return lax.dot_general(a.astype(BF16), b.astype(BF16), (((1,), (1,)), ((), ())),
                           preferred_element_type=F32)


def _dot_tn(a, b):
    return lax.dot_general(a.astype(BF16), b.astype(BF16), (((0,), (0,)), ((), ())),
                           preferred_element_type=F32)


def _split2(a):
    hi = a.astype(BF16)
    lo = (a - hi.astype(F32)).astype(BF16)
    return hi, lo


def _split3(a):
    h1 = a.astype(BF16)
    r = a - h1.astype(F32)
    h2 = r.astype(BF16)
    h3 = (r - h2.astype(F32)).astype(BF16)
    return h1, h2, h3


def _dot_hp(a, b):
    ah, al = _split2(a)
    bh, bl = _split2(b)
    d = functools.partial(jnp.dot, preferred_element_type=F32)
    return d(ah, bh) + (d(ah, bl) + d(al, bh))


def _dot_exact_lhs(a_exact, b):
    a = a_exact.astype(BF16)
    b1, b2, b3 = _split3(b)
    d = functools.partial(jnp.dot, preferred_element_type=F32)
    return d(a, b1) + (d(a, b2) + d(a, b3))


def _sigmoid(x):
    return 1.0 / (1.0 + jnp.exp(-x))


def _silu(x):
    return x * _sigmoid(x)


def _softplus(x):
    return jnp.maximum(x, 0.0) + jnp.log1p(jnp.exp(-jnp.abs(x)))


def _ada_kernel(c_ref, w_ref, b_ref, o_ref):
    c = c_ref[...]
    o_ref[...] = _dot_hp(_silu(c), w_ref[...]) + b_ref[...]


def _ada(c, ada_w, ada_b):
    bsz, d = c.shape
    n = ada_w.shape[1]
    tn = 1024
    return pl.pallas_call(
        _ada_kernel,
        grid=(n // tn,),
        in_specs=[pl.BlockSpec((bsz, d), lambda j: (0, 0)),
                  pl.BlockSpec((d, tn), lambda j: (0, j)),
                  pl.BlockSpec((1, tn), lambda j: (0, j))],
        out_specs=pl.BlockSpec((bsz, tn), lambda j: (0, j)),
        out_shape=jax.ShapeDtypeStruct((bsz, n), F32),
        compiler_params=_params(("arbitrary",)),
        name="ada",
    )(c, ada_w, ada_b.reshape(1, n))


def _rms_mod(x, nw, scale, shift):
    y = x * lax.rsqrt(jnp.mean(x * x, axis=-1, keepdims=True) + EPS) * nw
    return y * (1.0 + scale) + shift


def _rms(x, w):
    return x * lax.rsqrt(jnp.mean(x * x, axis=-1, keepdims=True) + EPS) * w


def _inproj_kernel(x_ref, mod_ref, nw_ref, w_in_ref, cw_ref, alog_ref, dtb_ref, qw_ref, kslw_ref,
                   kwnw_ref, gp_ref, sm_ref, cp_ref, qn_ref, kext_ref, vext_ref, kwn_ref, vwext_ref,
                   w_ref, *pre_refs, tm, seq):
    head = SUBLANES
    i = pl.program_id(0)

    @pl.when(i == 0)
    def _():
        n_small = 2 * GDN_HEADS
        n_mid = COL_SMALL - COL_NQ
        n_gate = N_GATES * NSA_HEADS
        rb = LANES
        for r0 in range(0, w_in_ref.shape[0], rb):
            rows = slice(r0, r0 + rb)
            w_ref[rows, :COL_NQ] = w_in_ref[rows, :COL_NQ].astype(BF16)
            w_ref[rows, COL_NQ:COL_SMALL] = w_in_ref[rows, COL_NQ + n_small:COL_NQ + n_small + n_mid].astype(BF16)
            tail = jnp.concatenate(
                [w_in_ref[rows, COL_NQ:COL_NQ + n_small],
                 w_in_ref[rows, COL_NQ + n_small + n_mid:COL_NQ + n_small + n_mid + n_gate],
                 jnp.zeros((rb, LANES - n_small - n_gate), F32)], axis=1)
            w_ref[rows, COL_SMALL:] = tail.astype(BF16)

    @pl.when(i % (seq // tm) == 0)
    def _():
        for pre_ref in pre_refs:
            pre_ref[:head, :] = jnp.zeros((head, pre_ref.shape[1]), F32)

    m = mod_ref[0]
    h = _rms_mod(x_ref[...], nw_ref[...], m[1:2], m[0:1]).astype(BF16)

    def proj(c0, width):
        return jnp.dot(h, w_ref[:, c0:c0 + width], preferred_element_type=F32)

    for idx in range(3):
        cols = slice(idx * GDN_WIDTH, (idx + 1) * GDN_WIDTH)
        pre_ref = pre_refs[idx]
        pre_ref[head:, :] = proj(COL_GQ + idx * GDN_WIDTH, GDN_WIDTH)
        y = cw_ref[GDN_CONV - 1:GDN_CONV, cols] * pre_ref[head:, :]
        for j in range(GDN_CONV - 1):
            y = y + cw_ref[j:j + 1, cols] * pre_ref[pl.ds(head - (GDN_CONV - 1) + j, tm), :]
        pre_ref[:head, :] = pre_ref[tm:tm + head, :]
        y = _silu(y)
        for hh in range(GDN_HEADS):
            hl = slice(hh * HEAD_DIM, (hh + 1) * HEAD_DIM)
            yh = y[:, hl]
            if idx == 0:
                yh = yh * (lax.rsqrt(jnp.sum(yh * yh, axis=-1, keepdims=True) + EPS) * HEAD_DIM ** -0.5)
            elif idx == 1:
                yh = yh * lax.rsqrt(jnp.sum(yh * yh, axis=-1, keepdims=True) + EPS)
            gp_ref[:, idx * GDN_WIDTH + hh * HEAD_DIM:idx * GDN_WIDTH + (hh + 1) * HEAD_DIM] = yh
    gp_ref[:, COL_GZ:COL_GZ + GDN_WIDTH] = _silu(proj(COL_GZ, GDN_WIDTH))

    nq = proj(COL_NQ, NSA_WIDTH)
    for hh in range(NSA_HEADS):
        hl = slice(hh * HEAD_DIM, (hh + 1) * HEAD_DIM)
        qn_ref[:, hl] = (_rms(nq[:, hl], qw_ref[...]) * (HEAD_DIM ** -0.5 * LOG2E)).astype(BF16)
    r = proj(COL_KC, 4 * HEAD_DIM)
    cp_ref[...] = r[:, :2 * HEAD_DIM]
    ones = jnp.ones((tm, LANES), BF16)
    pos = (i * tm + lax.broadcasted_iota(jnp.int32, (tm, LANES), 0)) % seq
    lane = lax.broadcasted_iota(jnp.int32, (tm, LANES), 1)
    kext_ref[:, :HEAD_DIM] = _rms(r[:, 2 * HEAD_DIM:3 * HEAD_DIM], kslw_ref[...]).astype(BF16)
    kext_ref[:, HEAD_DIM:] = jnp.where(lane == pos // SEL_BLOCK, NEG_INF, 0.0).astype(BF16)
    vext_ref[:, :HEAD_DIM] = r[:, 3 * HEAD_DIM:].astype(BF16)
    vext_ref[:, HEAD_DIM:] = ones
    r = proj(COL_KWN, 2 * HEAD_DIM + LANES)
    kwn_ref[...] = _rms(r[:, :HEAD_DIM], kwnw_ref[...]).astype(BF16)
    vwext_ref[:, :HEAD_DIM] = r[:, HEAD_DIM:2 * HEAD_DIM].astype(BF16)
    vwext_ref[:, HEAD_DIM:] = ones
    small = r[:, 2 * HEAD_DIM:]
    log_decay = -jnp.exp(alog_ref[...]) * _softplus(small + dtb_ref[...])
    sm_ref[...] = jnp.where(lane < LANE_BETA, log_decay, _sigmoid(small))


def _inproj(x2, mod3, norm1_w, w_in, gdn_conv_w, alog_p, dtb_p, q_w, ksl_w, kwn_w, seq):
    t, d = x2.shape
    assert w_in.shape[1] == COL_SMALL + 2 * GDN_HEADS + N_GATES * NSA_HEADS
    tm = 512
    per_b = seq // tm
    assert seq // SEL_BLOCK <= LANES
    ext = HEAD_DIM + LANES
    full = lambda a: pl.BlockSpec(a.shape, lambda i: (0,) * a.ndim)
    row = lambda w: pl.BlockSpec((tm, w), lambda i: (i, 0))
    vecs = [q_w.reshape(1, -1), ksl_w.reshape(1, -1), kwn_w.reshape(1, -1)]
    out_widths = [(4 * GDN_WIDTH, F32), (LANES, F32), (2 * HEAD_DIM, F32), (NSA_WIDTH, BF16),
                  (ext, BF16), (ext, BF16), (HEAD_DIM, BF16), (ext, BF16)]
    return pl.pallas_call(
        functools.partial(_inproj_kernel, tm=tm, seq=seq),
        grid=(t // tm,),
        in_specs=[row(d),
                  pl.BlockSpec((1, 6, d), lambda i: (i // per_b, 0, 0)),
                  pl.BlockSpec((1, d), lambda i: (0, 0)),
                  full(w_in), full(gdn_conv_w), full(alog_p), full(dtb_p)] + [full(v) for v in vecs],
        out_specs=[row(w) for w, _ in out_widths],
        out_shape=[jax.ShapeDtypeStruct((t, w), dt) for w, dt in out_widths],
        scratch_shapes=[pltpu.VMEM((d, N_PROJ), BF16)] + [pltpu.VMEM((SUBLANES + tm, GDN_WIDTH), F32)] * 3,
        compiler_params=_params(("arbitrary",)),
        name="inproj",
    )(x2, mod3, norm1_w.reshape(1, d), w_in, gdn_conv_w, alog_p, dtb_p, *vecs)


def _gdn_kernel(q_ref, k_ref, v_ref, z_ref, sm_ref, onw_ref, o_ref,
                state_ref, pq_ref, p2_ref, oi_ref, os_ref, *, tile):
    c_len = GDN_CHUNK
    n_chunks = tile // c_len

    @pl.when(pl.program_id(1) == 0)
    def _():
        state_ref[...] = jnp.zeros_like(state_ref)

    qc = q_ref[...]
    kc = k_ref[...]
    vc = v_ref[...]
    g_all = sm_ref[...]
    beta_all = g_all

    ri = lax.broadcasted_iota(jnp.int32, (tile, tile), 0)
    ci = lax.broadcasted_iota(jnp.int32, (tile, tile), 1)
    lblk = jnp.where((ri // c_len == ci // c_len) & (ci <= ri), 1.0, 0.0)
    gc_all = _dot_exact_lhs(lblk, g_all)
    gc_t = gc_all.T
    gl_all = jnp.concatenate(
        [jnp.broadcast_to(gc_all[(c + 1) * c_len - 1:(c + 1) * c_len], (c_len, LANES))
         for c in range(n_chunks)], axis=0)
    eg_all = jnp.exp(gc_all)
    ekd_all = jnp.exp(gl_all - gc_all)
    egl_all = jnp.exp(gl_all)

    r64 = lax.broadcasted_iota(jnp.int32, (c_len, c_len), 0)
    c64 = lax.broadcasted_iota(jnp.int32, (c_len, c_len), 1)
    incl = r64 >= c64
    strict = r64 > c64
    eye = jnp.where(r64 == c64, 1.0, 0.0)

    def lane_bcast(x, lane):
        return jnp.broadcast_to(x[:, lane:lane + 1], (tile, HEAD_DIM))

    units = [(c, h) for c in range(n_chunks) for h in range(GDN_HEADS)]
    heads = []
    for h in range(GDN_HEADS):
        hl = slice(h * HEAD_DIM, (h + 1) * HEAD_DIM)
        qh = qc[:, hl]
        kh = kc[:, hl]
        bb = lane_bcast(beta_all, LANE_BETA + h)
        eg = lane_bcast(eg_all, LANE_A + h)
        kb = kh * bb
        heads.append(dict(qd=qh * eg, k=kh, kb=kb, q=qh, vb=vc[:, hl] * bb, kbe=kb * eg,
                          kd=kh * lane_bcast(ekd_all, LANE_A + h),
                          egl=lane_bcast(egl_all, LANE_A + h)))

    a_mats, attns = [], []
    for c, h in units:
        rows = slice(c * c_len, (c + 1) * c_len)
        hd = heads[h]
        gcol = gc_all[rows, LANE_A + h:LANE_A + h + 1]
        grow = gc_t[LANE_A + h:LANE_A + h + 1, rows]
        decay = jnp.exp(jnp.where(incl, gcol - grow, NEG_INF))
        gram = _dot_nt(jnp.concatenate([hd["kb"][rows], hd["q"][rows]], axis=0), hd["k"][rows])
        a_mats.append(jnp.where(strict, gram[:c_len] * decay, 0.0))
        attns.append(jnp.where(incl, gram[c_len:] * decay, 0.0))
    def hp(a, b):
        d = functools.partial(jnp.dot, preferred_element_type=F32)
        return d(a[0], b[0]) + (d(a[0], b[1]) + d(a[1], b[0]))

    xps = [-a for a in a_mats]
    t_mats = [eye + x for x in xps]
    x_splits = [_split2(x) for x in xps]
    for level in range(c_len.bit_length() - 2):
        xps = [hp(xs, xs) for xs in x_splits]
        x_splits = [_split2(x) for x in xps]
        t_mats = [t + hp(_split2(t), xs) for t, xs in zip(t_mats, x_splits)]
    for i, (c, h) in enumerate(units):
        rows = slice(c * c_len, (c + 1) * c_len)
        hd = heads[h]
        uw = _dot(t_mats[i], jnp.concatenate([hd["vb"][rows], hd["kbe"][rows]], axis=1))
        aw = _dot(attns[i], uw)
        kw = _dot_tn(hd["kd"][rows], uw)
        pq_ref[i, :HEAD_DIM] = kw[:, HEAD_DIM:]
        pq_ref[i, HEAD_DIM:] = hd["qd"][rows] - aw[:, HEAD_DIM:]
        p2_ref[i] = kw[:, :HEAD_DIM]
        oi_ref[i] = aw[:, :HEAD_DIM]

    for i, (c, h) in enumerate(units):
        rows = slice(c * c_len, (c + 1) * c_len)
        state = state_ref[h]
        r = _dot(pq_ref[i], state)
        os_ref[rows, h * HEAD_DIM:(h + 1) * HEAD_DIM] = r[HEAD_DIM:] + oi_ref[i]
        egl = heads[h]["egl"][c * c_len:c * c_len + 1]
        state_ref[h] = state * egl - r[:HEAD_DIM] + p2_ref[i]

    for h in range(GDN_HEADS):
        hl = slice(h * HEAD_DIM, (h + 1) * HEAD_DIM)
        o = os_ref[:, hl]
        o = o * lax.rsqrt(jnp.mean(o * o, axis=-1, keepdims=True) + EPS) * onw_ref[...]
        o_ref[:, hl] = o * z_ref[:, hl]


def _gdn(gp, sm, out_norm_w, bsz, seq):
    tile = 256
    per_b = seq // tile
    n_units = tile // GDN_CHUNK * GDN_HEADS
    col = lambda c: pl.BlockSpec((tile, GDN_WIDTH), lambda b, s: (b * per_b + s, c))
    return pl.pallas_call(
        functools.partial(_gdn_kernel, tile=tile),
        grid=(bsz, per_b),
        in_specs=[col(0), col(1), col(2), col(3),
                  pl.BlockSpec((tile, LANES), lambda b, s: (b * per_b + s, 0)),
                  pl.BlockSpec((1, HEAD_DIM), lambda b, s: (0, 0))],
        out_specs=pl.BlockSpec((tile, GDN_WIDTH), lambda b, s: (b * per_b + s, 0)),
        out_shape=jax.ShapeDtypeStruct((bsz * seq, GDN_WIDTH), F32),
        scratch_shapes=[pltpu.VMEM((GDN_HEADS, HEAD_DIM, HEAD_DIM), F32),
                        pltpu.VMEM((n_units, HEAD_DIM + GDN_CHUNK, HEAD_DIM), F32),
                        pltpu.VMEM((n_units, HEAD_DIM, HEAD_DIM), F32),
                        pltpu.VMEM((n_units, GDN_CHUNK, HEAD_DIM), F32),
                        pltpu.VMEM((tile, GDN_WIDTH), F32)],
        compiler_params=_params(("arbitrary", "arbitrary")),
        name="gdn",
    )(gp, gp, gp, gp, sm, out_norm_w.reshape(1, HEAD_DIM))


def _compress_kernel(kc_ref, vc_ref, kpos_ref, kw1_ref, kw2_ref, knw_ref, vpos_ref, vw1_ref, vw2_ref,
                     ko_ref, vo_ref, *, n_cmp):
    ncp = ko_ref.shape[0]
    half = CMP_BLOCK // 2

    def one(x_ref, pos_ref, w1_ref, w2_ref):
        p = jnp.zeros((ncp, HEAD_DIM), F32)
        q = jnp.zeros((ncp, HEAD_DIM), F32)
        for j in range(half):
            a = x_ref[pl.ds(j, ncp, stride=CMP_STRIDE), :]
            p = p + _dot(a + pos_ref[j:j + 1, :], w1_ref[j * HEAD_DIM:(j + 1) * HEAD_DIM, :])
            q = q + _dot(a + pos_ref[half + j:half + j + 1, :],
                         w1_ref[(half + j) * HEAD_DIM:(half + j + 1) * HEAD_DIM, :])
        hid = _silu(p + pltpu.roll(q, ncp - 1, axis=0))
        out = _dot(hid, w2_ref[...])
        valid = lax.broadcasted_iota(jnp.int32, (ncp, HEAD_DIM), 0) < n_cmp
        return jnp.where(valid, out, 0.0)

    k = one(kc_ref, kpos_ref, kw1_ref, kw2_ref)
    ko_ref[...] = jnp.where(lax.broadcasted_iota(jnp.int32, k.shape, 0) < n_cmp,
                            _rms(k, knw_ref[...]), 0.0).astype(BF16)
    vo_ref[...] = one(vc_ref, vpos_ref, vw1_ref, vw2_ref).astype(BF16)


def _compress(cp, kpos, kw1, kw2, knw, vpos, vw1, vw2, bsz, seq):
    ncp = seq // CMP_STRIDE
    n_cmp = (seq - CMP_BLOCK) // CMP_STRIDE + 1
    full = lambda a: pl.BlockSpec(a.shape, lambda b: (0,) * a.ndim)
    knw2 = knw.reshape(1, HEAD_DIM)
    out = pl.BlockSpec((None, ncp, HEAD_DIM), lambda b: (b, 0, 0))
    shp = jax.ShapeDtypeStruct((bsz, ncp, HEAD_DIM), BF16)
    return pl.pallas_call(
        functools.partial(_compress_kernel, n_cmp=n_cmp),
        grid=(bsz,),
        in_specs=[pl.BlockSpec((seq, HEAD_DIM), lambda b: (b, 0)),
                  pl.BlockSpec((seq, HEAD_DIM), lambda b: (b, 1)),
                  full(kpos), full(kw1), full(kw2), full(knw2), full(vpos), full(vw1), full(vw2)],
        out_specs=[out, out],
        out_shape=[shp, shp],
        compiler_params=_params(("arbitrary",)),
        name="compress",
    )(cp, cp, kpos, kw1, kw2, knw2, vpos, vw1, vw2)


def _nsa_kernel(q_ref, kc_ref, vc_ref, ks_ref, vs_ref, kw_ref, vw_ref, sm_ref, ovt_ref, o_ref,
                m_ref, acc_ref, *, tq, tk, n_pick):
    qi = pl.program_id(1)
    t0 = qi * tq
    nh = NSA_HEADS
    ncp = kc_ref.shape[0]
    nsp = ovt_ref.shape[0]
    q = q_ref[...]
    q2 = jnp.concatenate([q[:, h * HEAD_DIM:(h + 1) * HEAD_DIM] for h in range(nh)], axis=0)
    gates = sm_ref[...]

    def gate(h, g):
        c = LANE_GATE + N_GATES * h + g
        return gates[:, c:c + 1]

    def head(x, h):
        return x[h * tq:(h + 1) * tq]

    t_c = t0 + lax.broadcasted_iota(jnp.int32, (tq, ncp), 0)
    cend = lax.broadcasted_iota(jnp.int32, (tq, ncp), 1) * CMP_STRIDE + (CMP_BLOCK - 1)
    bias_c = jnp.where(cend <= t_c, 0.0, NEG_INF)
    sees_any = (t0 + lax.broadcasted_iota(jnp.int32, (tq, 1), 0)) >= CMP_BLOCK - 1
    s_c = _dot_nt(q2, kc_ref[...])
    vcmp = vc_ref[...]
    o_c = []
    psum = jnp.zeros((tq, ncp), F32)
    for h in range(nh):
        s = head(s_c, h) + bias_c
        p = jnp.exp2(s - jnp.max(s, axis=-1, keepdims=True))
        l = jnp.sum(p, axis=-1, keepdims=True)
        p = p * jnp.where(sees_any, 1.0 / l, 0.0)
        psum = psum + p
        o_c.append(_dot(p, vcmp))
    ph, pl_ = _split2(psum)
    ovt = ovt_ref[...]
    dnt = functools.partial(lax.dot_general, dimension_numbers=(((1,), (1,)), ((), ())),
                            preferred_element_type=F32)
    imp_t = dnt(ovt, ph) + dnt(ovt, pl_)

    wlen = WINDOW + tq
    start = pl.multiple_of(jnp.maximum(t0 - WINDOW, 0), tq)
    kw = kw_ref[pl.ds(start, wlen), :]
    vw = vw_ref[pl.ds(start, wlen), :]
    s_w = _dot_nt(q2, kw)
    kpos_w = start + lax.broadcasted_iota(jnp.int32, (tq, wlen), 1)
    t_w = t0 + lax.broadcasted_iota(jnp.int32, (tq, wlen), 0)
    bias_w = jnp.where(kpos_w <= t_w, jnp.where(kpos_w > t_w - WINDOW, 0.0, NEG_INF), NEG_INF)
    o_w = []
    for h in range(nh):
        s = head(s_w, h) + bias_w
        p = jnp.exp2(s - jnp.max(s, axis=-1, keepdims=True))
        pv = _dot(p, vw)
        o_w.append(pv[:, :HEAD_DIM] / pv[:, HEAD_DIM:])

    jj = lax.broadcasted_iota(jnp.int32, (nsp, tq), 0)
    cur = (t0 + lax.broadcasted_iota(jnp.int32, (nsp, tq), 1)) // SEL_BLOCK
    forced = (jj == 0) | (jj == cur) | (jj == cur - 1)
    valid = jj <= cur
    notsel = jnp.where(valid & forced, 0.0, 1.0)
    work = jnp.where(valid, jnp.where(forced, -BIG, imp_t), -BIG)
    jf = jj.astype(F32)
    for _ in range(n_pick - 3):
        mx = jnp.max(work, axis=0, keepdims=True)
        cand = jnp.where(work == mx, jnp.where(mx > -1e38, jf, float(nsp)), float(nsp))
        first = jnp.min(cand, axis=0, keepdims=True)
        pick = jf == first
        notsel = jnp.where(pick, 0.0, notsel)
        work = jnp.where(pick, -BIG, work)
    notsel_q = notsel.T.astype(BF16)
    q2x = jnp.concatenate([q2, jnp.concatenate([notsel_q] * nh, axis=0)], axis=1)

    m_ref[...] = jnp.full(m_ref.shape, NEG_INF, F32)
    acc_ref[...] = jnp.zeros(acc_ref.shape, F32)
    n_kt = (t0 + tq - 1) // tk + 1

    def scores(kt):
        k0 = pl.multiple_of(kt * tk, tk)
        return _dot_nt(q2x, ks_ref[pl.ds(k0, tk), :])

    def probs(kt, s, causal):
        if causal:
            kpos = kt * tk + lax.broadcasted_iota(jnp.int32, (tq, tk), 1)
            t_s = t0 + lax.broadcasted_iota(jnp.int32, (tq, tk), 0)
            bias = jnp.where(kpos <= t_s, 0.0, NEG_INF)
        ps, alphas = [], []
        for h in range(nh):
            rows = slice(h * tq, (h + 1) * tq)
            sh = head(s, h) + bias if causal else head(s, h)
            m_prev = m_ref[rows]
            m_next = jnp.maximum(m_prev, jnp.max(sh, axis=-1, keepdims=True))
            alphas.append(jnp.exp2(m_prev - m_next))
            ps.append(jnp.exp2(sh - jnp.concatenate([m_next] * (tk // LANES), axis=1)).astype(BF16))
            m_ref[rows] = m_next
        return jnp.concatenate(ps, axis=0), jnp.concatenate(alphas, axis=0)

    def accumulate(kt, p, alpha):
        k0 = pl.multiple_of(kt * tk, tk)
        pv = jnp.dot(p, vs_ref[pl.ds(k0, tk), :], preferred_element_type=F32)
        acc_ref[...] = jnp.concatenate([alpha, alpha], axis=1) * acc_ref[...] + pv

    def sweep_body(kt, s_cur):
        s_next = scores(kt + 1)
        accumulate(kt, *probs(kt, s_cur, False))
        return s_next

    s_last = lax.fori_loop(0, n_kt - 1, sweep_body, scores(0))
    accumulate(n_kt - 1, *probs(n_kt - 1, s_last, True))

    for h in range(nh):
        rows = slice(h * tq, (h + 1) * tq)
        l_s = acc_ref[rows, HEAD_DIM:]
        o_s = acc_ref[rows, :HEAD_DIM] * jnp.where(l_s > 0.0, 1.0 / l_s, 0.0)
        o_ref[:, h * HEAD_DIM:(h + 1) * HEAD_DIM] = (gate(h, 0) * o_c[h] + gate(h, 1) * o_s
                                                     + gate(h, 2) * o_w[h])


def _nsa(qn, kcmp, vcmp, ksl, vsl, kwn, vwn, sm, overlap, bsz, seq):
    tq, tk = 256, 512
    per_b = seq // tq
    ncp = kcmp.shape[1]
    nsp = overlap.shape[0]
    n_pick = min(N_SELECT, seq // SEL_BLOCK)
    assert n_pick == N_SELECT and seq % tk == 0 and seq >= WINDOW + tq and nsp == LANES
    ext = HEAD_DIM + LANES
    seq_kv = lambda w: pl.BlockSpec((seq, w), lambda b, i: (b, 0))
    cmp_kv = lambda: pl.BlockSpec((None, ncp, HEAD_DIM), lambda b, i: (b, 0, 0))
    return pl.pallas_call(
        functools.partial(_nsa_kernel, tq=tq, tk=tk, n_pick=n_pick),
        grid=(bsz, per_b),
        in_specs=[pl.BlockSpec((tq, NSA_WIDTH), lambda b, i: (b * per_b + i, 0)),
                  cmp_kv(), cmp_kv(), seq_kv(ext), seq_kv(ext), seq_kv(HEAD_DIM), seq_kv(ext),
                  pl.BlockSpec((tq, LANES), lambda b, i: (b * per_b + i, 0)),
                  pl.BlockSpec((nsp, ncp), lambda b, i: (0, 0))],
        out_specs=pl.BlockSpec((tq, NSA_WIDTH), lambda b, i: (b * per_b + i, 0)),
        out_shape=jax.ShapeDtypeStruct((bsz * seq, NSA_WIDTH), F32),
        scratch_shapes=[pltpu.VMEM((NSA_HEADS * tq, LANES), F32),
                        pltpu.VMEM((NSA_HEADS * tq, ext), F32)],
        compiler_params=_params(("arbitrary", "arbitrary")),
        name="nsa",
    )(qn, kcmp, vcmp, ksl, vsl, kwn, vwn, sm, overlap)


def _overlap_matrix(seq):
    n_cmp = (seq - CMP_BLOCK) // CMP_STRIDE + 1
    n_sel = seq // SEL_BLOCK
    ncp = seq // CMP_STRIDE
    nsp = -(-n_sel // LANES) * LANES
    cs = np.arange(n_cmp) * CMP_STRIDE
    ss = np.arange(n_sel) * SEL_BLOCK
    ov = np.minimum(cs[:, None] + CMP_BLOCK, ss[None, :] + SEL_BLOCK) - np.maximum(cs[:, None], ss[None, :])
    out = np.zeros((nsp, ncp), np.float32)
    out[:n_sel, :n_cmp] = (np.clip(ov, 0, None).astype(np.float32) / CMP_BLOCK).T
    return jnp.asarray(out, dtype=BF16)


def _ffn_kernel(x_ref, og_ref, on_ref, mod_ref, wo_ref, nw_ref, wu_ref, cw_ref, cb_ref, wd_ref, o_ref,
                up_ref, act_ref, *, tm, d_ff, cw, down_parts):
    head = SUBLANES

    @pl.when(pl.program_id(1) == 0)
    def _():
        up_ref[:, :head, :] = jnp.zeros((2, head, d_ff), F32)

    m = mod_ref[0]
    mixed = _dot(og_ref[...], wo_ref[:GDN_WIDTH, :]) + _dot(on_ref[...], wo_ref[GDN_WIDTH:, :])
    x1 = x_ref[...] + m[2:3] * mixed
    h2 = _rms_mod(x1, nw_ref[...], m[4:5], m[3:4]).astype(BF16)
    n_chunks = d_ff // cw
    y = None
    done = 0
    for c in range(n_chunks):
        cols = slice(c * cw, (c + 1) * cw)
        halves = []
        for g in range(2):
            wcols = slice(g * d_ff + c * cw, g * d_ff + (c + 1) * cw)
            up_ref[g, head:, cols] = jnp.dot(h2, wu_ref[:, wcols], preferred_element_type=F32)
            z = cb_ref[:, wcols]
            for j in range(FFN_CONV):
                tap = up_ref[g, pl.ds(head - (FFN_CONV - 1) + j, tm), cols]
                z = z + cw_ref[j:j + 1, wcols] * tap
            up_ref[g, :head, cols] = up_ref[g, tm:tm + head, cols]
            halves.append(z)
        act_ref[:, cols] = (_silu(halves[0]) * halves[1]).astype(BF16)
        if (c + 1) * down_parts // n_chunks > c * down_parts // n_chunks:
            part = slice(done * cw, (c + 1) * cw)
            d = jnp.dot(act_ref[:, part], wd_ref[part, :], preferred_element_type=F32)
            y = d if y is None else y + d
            done = c + 1
    o_ref[...] = x1 + m[5:6] * y


def _ffn(x2, o_g, o_n, mod3, w_out_b, norm2_w, w_up_b, conv_w, conv_b, w_down_b, bsz, seq):
    t, d = x2.shape
    d_ff = w_down_b.shape[0]
    tm, cw = 512, 256
    assert d_ff % cw == 0
    per_b = seq // tm
    full = lambda a: pl.BlockSpec(a.shape, lambda b, s: (0,) * a.ndim)
    conv_b2 = conv_b.reshape(1, -1)
    norm2_w2 = norm2_w.reshape(1, d)
    return pl.pallas_call(
        functools.partial(_ffn_kernel, tm=tm, d_ff=d_ff, cw=cw, down_parts=1),
        grid=(bsz, per_b),
        in_specs=[pl.BlockSpec((tm, d), lambda b, s: (b * per_b + s, 0)),
                  pl.BlockSpec((tm, GDN_WIDTH), lambda b, s: (b * per_b + s, 0)),
                  pl.BlockSpec((tm, NSA_WIDTH), lambda b, s: (b * per_b + s, 0)),
                  pl.BlockSpec((1, 6, d), lambda b, s: (b, 0, 0)),
                  full(w_out_b), full(norm2_w2), full(w_up_b), full(conv_w), full(conv_b2),
                  full(w_down_b)],
        out_specs=pl.BlockSpec((tm, d), lambda b, s: (b * per_b + s, 0)),
        out_shape=jax.ShapeDtypeStruct((t, d), F32),
        scratch_shapes=[pltpu.VMEM((2, SUBLANES + tm, d_ff), F32),
                        pltpu.VMEM((tm, d_ff), BF16)],
        compiler_params=_params(("arbitrary", "arbitrary")),
        name="ffn",
    )(x2, o_g, o_n, mod3, w_out_b, norm2_w2, w_up_b, conv_w, conv_b2, w_down_b)


def _pad_lanes(a, n):
    return jnp.pad(a, ((0, 0), (0, n - a.shape[1])))


def _layer(x, c, ada_w, ada_b, norm1_w, w_in, gdn_conv_w, gdn_A_log, gdn_dt_bias, gdn_out_norm_w,
           nsa_q_norm_w, nsa_k_norm_cmp, nsa_k_norm_slc, nsa_k_norm_win, cmp_k_pos, cmp_k_w1, cmp_k_w2,
           cmp_v_pos, cmp_v_w1, cmp_v_w2, w_out, norm2_w, ffn_w_up, ffn_conv_w, ffn_conv_b, ffn_w_down):
    bsz, seq, d = x.shape
    x2 = x.reshape(bsz * seq, d)
    mod3 = _ada(c, ada_w, ada_b).reshape(bsz, 6, d)

    gp, sm, cp, qn, ksl, vsl, kwn, vwn = _inproj(
        x2, mod3, norm1_w, w_in, gdn_conv_w, _pad_lanes(gdn_A_log.reshape(1, -1), LANES),
        _pad_lanes(gdn_dt_bias.reshape(1, -1), LANES), nsa_q_norm_w, nsa_k_norm_slc, nsa_k_norm_win, seq)

    o_g = _gdn(gp, sm, gdn_out_norm_w, bsz, seq)

    kcmp, vcmp = _compress(cp, cmp_k_pos, cmp_k_w1.astype(BF16), cmp_k_w2.astype(BF16), nsa_k_norm_cmp,
                           cmp_v_pos, cmp_v_w1.astype(BF16), cmp_v_w2.astype(BF16), bsz, seq)
    o_n = _nsa(qn, kcmp, vcmp, ksl, vsl, kwn, vwn, sm, _overlap_matrix(seq), bsz, seq)

    out = _ffn(x2, o_g, o_n, mod3, w_out.astype(BF16), norm2_w, ffn_w_up.astype(BF16), ffn_conv_w,
               ffn_conv_b, ffn_w_down.astype(BF16), bsz, seq)
    return out.reshape(bsz, seq, d)


def kernel(x, c, ada_w, ada_b, norm1_w, w_in, gdn_conv_w, gdn_A_log, gdn_dt_bias, gdn_out_norm_w, nsa_q_norm_w, nsa_k_norm_cmp, nsa_k_norm_slc, nsa_k_norm_win, cmp_k_pos, cmp_k_w1, cmp_k_w2, cmp_v_pos, cmp_v_w1, cmp_v_w2, w_out, norm2_w, ffn_w_up, ffn_conv_w, ffn_conv_b, ffn_w_down):
    for l in range(ada_w.shape[0]):
        x = _layer(x, c, ada_w[l], ada_b[l], norm1_w[l], w_in[l], gdn_conv_w[l], gdn_A_log[l],
                   gdn_dt_bias[l], gdn_out_norm_w[l], nsa_q_norm_w[l], nsa_k_norm_cmp[l],
                   nsa_k_norm_slc[l], nsa_k_norm_win[l], cmp_k_pos[l], cmp_k_w1[l], cmp_k_w2[l],
                   cmp_v_pos[l], cmp_v_w1[l], cmp_v_w2[l], w_out[l], norm2_w[l], ffn_w_up[l],
                   ffn_conv_w[l], ffn_conv_b[l], ffn_w_down[l])
    return x
```

```python
import functools

import numpy as np
import jax
import jax.numpy as jnp
from jax import lax
from jax.experimental import pallas as pl
from jax.experimental.pallas import tpu as pltpu

F32 = jnp.float32
BF16 = jnp.bfloat16

HEAD_DIM = 128
GDN_HEADS = 4
NSA_HEADS = 4
GDN_WIDTH = GDN_HEADS * HEAD_DIM
NSA_WIDTH = NSA_HEADS * HEAD_DIM
GDN_CONV = 4
GDN_CHUNK = 64
CMP_BLOCK = 32
CMP_STRIDE = 16
SEL_BLOCK = 64
N_SELECT = 16
WINDOW = 512
FFN_CONV = 3
N_GATES = 3
EPS = 1e-6
NEG_INF = -1e30
BIG = 3e38
LOG2E = 1.4426950408889634

LANES = 128
SUBLANES = 8
VMEM_LIMIT = 56 * 1024 * 1024

COL_GQ, COL_GK, COL_GV, COL_GZ = 0, 512, 1024, 1536
COL_NQ = 2048
COL_KC, COL_VC, COL_KSL, COL_VSL, COL_KWN, COL_VWN = 2560, 2688, 2816, 2944, 3072, 3200
COL_SMALL = 3328
N_PROJ = 3456
LANE_A, LANE_BETA, LANE_GATE = 0, 4, 8


def _params(sem):
    return pltpu.CompilerParams(dimension_semantics=sem, vmem_limit_bytes=VMEM_LIMIT)


def _dot(a, b):
    return jnp.dot(a.astype(BF16), b.astype(BF16), preferred_element_type=F32)


def _dot_nt(a, b):
    return lax.dot_general(a.astype(BF16), b.astype(BF16), (((1,), (1,)), ((), ())),
                           preferred_element_type=F32)


def _dot_tn(a, b):
    return lax.dot_general(a.astype(BF16), b.astype(BF16), (((0,), (0,)), ((), ())),
                           preferred_element_type=F32)


def _split2(a):
    hi = a.astype(BF16)
    lo = (a - hi.astype(F32)).astype(BF16)
    return hi, lo


def _split3(a):
    h1 = a.astype(BF16)
    r = a - h1.astype(F32)
    h2 = r.astype(BF16)
    h3 = (r - h2.astype(F32)).astype(BF16)
    return h1, h2, h3


def _dot_hp(a, b):
    ah, al = _split2(a)
    bh, bl = _split2(b)
    d = functools.partial(jnp.dot, preferred_element_type=F32)
    return d(ah, bh) + (d(ah, bl) + d(al, bh))


def _dot_exact_lhs(a_exact, b):
    a = a_exact.astype(BF16)
    b1, b2, b3 = _split3(b)
    d = functools.partial(jnp.dot, preferred_element_type=F32)
    return d(a, b1) + (d(a, b2) + d(a, b3))


def _sigmoid(x):
    return 1.0 / (1.0 + jnp.exp(-x))


def _silu(x):
    return x * _sigmoid(x)


def _softplus(x):
    return jnp.maximum(x, 0.0) + jnp.log1p(jnp.exp(-jnp.abs(x)))


def _ada_kernel(c_ref, w_ref, b_ref, o_ref):
    c = c_ref[...]
    o_ref[...] = _dot_hp(_silu(c), w_ref[...]) + b_ref[...]


def _ada(c, ada_w, ada_b):
    bsz, d = c.shape
    n = ada_w.shape[1]
    tn = 1024
    return pl.pallas_call(
        _ada_kernel,
        grid=(n // tn,),
        in_specs=[pl.BlockSpec((bsz, d), lambda j: (0, 0)),
                  pl.BlockSpec((d, tn), lambda j: (0, j)),
                  pl.BlockSpec((1, tn), lambda j: (0, j))],
        out_specs=pl.BlockSpec((bsz, tn), lambda j: (0, j)),
        out_shape=jax.ShapeDtypeStruct((bsz, n), F32),
        compiler_params=_params(("arbitrary",)),
        name="ada",
    )(c, ada_w, ada_b.reshape(1, n))


def _rms_mod(x, nw, scale, shift):
    y = x * lax.rsqrt(jnp.mean(x * x, axis=-1, keepdims=True) + EPS) * nw
    return y * (1.0 + scale) + shift


def _rms(x, w):
    return x * lax.rsqrt(jnp.mean(x * x, axis=-1, keepdims=True) + EPS) * w


def _inproj_kernel(x_ref, mod_ref, nw_ref, w_in_ref, cw_ref, alog_ref, dtb_ref, qw_ref, kslw_ref,
                   kwnw_ref, gp_ref, sm_ref, cp_ref, qn_ref, kext_ref, vext_ref, kwn_ref, vwext_ref,
                   w_ref, *pre_refs, tm, seq):
    head = SUBLANES
    i = pl.program_id(0)

    @pl.when(i == 0)
    def _():
        n_small = 2 * GDN_HEADS
        n_mid = COL_SMALL - COL_NQ
        n_gate = N_GATES * NSA_HEADS
        rb = LANES
        for r0 in range(0, w_in_ref.shape[0], rb):
            rows = slice(r0, r0 + rb)
            w_ref[rows, :COL_NQ] = w_in_ref[rows, :COL_NQ].astype(BF16)
            w_ref[rows, COL_NQ:COL_SMALL] = w_in_ref[rows, COL_NQ + n_small:COL_NQ + n_small + n_mid].astype(BF16)
            tail = jnp.concatenate(
                [w_in_ref[rows, COL_NQ:COL_NQ + n_small],
                 w_in_ref[rows, COL_NQ + n_small + n_mid:COL_NQ + n_small + n_mid + n_gate],
                 jnp.zeros((rb, LANES - n_small - n_gate), F32)], axis=1)
            w_ref[rows, COL_SMALL:] = tail.astype(BF16)

    @pl.when(i % (seq // tm) == 0)
    def _():
        for pre_ref in pre_refs:
            pre_ref[:head, :] = jnp.zeros((head, pre_ref.shape[1]), F32)

    m = mod_ref[0]
    h = _rms_mod(x_ref[...], nw_ref[...], m[1:2], m[0:1]).astype(BF16)

    def proj(c0, width):
        return jnp.dot(h, w_ref[:, c0:c0 + width], preferred_element_type=F32)

    for idx in range(3):
        cols = slice(idx * GDN_WIDTH, (idx + 1) * GDN_WIDTH)
        pre_ref = pre_refs[idx]
        pre_ref[head:, :] = proj(COL_GQ + idx * GDN_WIDTH, GDN_WIDTH)
        y = cw_ref[GDN_CONV - 1:GDN_CONV, cols] * pre_ref[head:, :]
        for j in range(GDN_CONV - 1):
            y = y + cw_ref[j:j + 1, cols] * pre_ref[pl.ds(head - (GDN_CONV - 1) + j, tm), :]
        pre_ref[:head, :] = pre_ref[tm:tm + head, :]
        y = _silu(y)
        for hh in range(GDN_HEADS):
            hl = slice(hh * HEAD_DIM, (hh + 1) * HEAD_DIM)
            yh = y[:, hl]
            if idx == 0:
                yh = yh * (lax.rsqrt(jnp.sum(yh * yh, axis=-1, keepdims=True) + EPS) * HEAD_DIM ** -0.5)
            elif idx == 1:
                yh = yh * lax.rsqrt(jnp.sum(yh * yh, axis=-1, keepdims=True) + EPS)
            gp_ref[:, idx * GDN_WIDTH + hh * HEAD_DIM:idx * GDN_WIDTH + (hh + 1) * HEAD_DIM] = yh
    gp_ref[:, COL_GZ:COL_GZ + GDN_WIDTH] = _silu(proj(COL_GZ, GDN_WIDTH))

    nq = proj(COL_NQ, NSA_WIDTH)
    for hh in range(NSA_HEADS):
        hl = slice(hh * HEAD_DIM, (hh + 1) * HEAD_DIM)
        qn_ref[:, hl] = (_rms(nq[:, hl], qw_ref[...]) * (HEAD_DIM ** -0.5 * LOG2E)).astype(BF16)
    r = proj(COL_KC, 4 * HEAD_DIM)
    cp_ref[...] = r[:, :2 * HEAD_DIM]
    ones = jnp.ones((tm, LANES), BF16)
    pos = (i * tm + lax.broadcasted_iota(jnp.int32, (tm, LANES), 0)) % seq
    lane = lax.broadcasted_iota(jnp.int32, (tm, LANES), 1)
    kext_ref[:, :HEAD_DIM] = _rms(r[:, 2 * HEAD_DIM:3 * HEAD_DIM], kslw_ref[...]).astype(BF16)
    kext_ref[:, HEAD_DIM:] = jnp.where(lane == pos // SEL_BLOCK, NEG_INF, 0.0).astype(BF16)
    vext_ref[:, :HEAD_DIM] = r[:, 3 * HEAD_DIM:].astype(BF16)
    vext_ref[:, HEAD_DIM:] = ones
    r = proj(COL_KWN, 2 * HEAD_DIM + LANES)
    kwn_ref[...] = _rms(r[:, :HEAD_DIM], kwnw_ref[...]).astype(BF16)
    vwext_ref[:, :HEAD_DIM] = r[:, HEAD_DIM:2 * HEAD_DIM].astype(BF16)
    vwext_ref[:, HEAD_DIM:] = ones
    small = r[:, 2 * HEAD_DIM:]
    log_decay = -jnp.exp(alog_ref[...]) * _softplus(small + dtb_ref[...])
    sm_ref[...] = jnp.where(lane < LANE_BETA, log_decay, _sigmoid(small))


def _inproj(x2, mod3, norm1_w, w_in, gdn_conv_w, alog_p, dtb_p, q_w, ksl_w, kwn_w, seq):
    t, d = x2.shape
    assert w_in.shape[1] == COL_SMALL + 2 * GDN_HEADS + N_GATES * NSA_HEADS
    tm = 512
    per_b = seq // tm
    assert seq // SEL_BLOCK <= LANES
    ext = HEAD_DIM + LANES
    full = lambda a: pl.BlockSpec(a.shape, lambda i: (0,) * a.ndim)
    row = lambda w: pl.BlockSpec((tm, w), lambda i: (i, 0))
    vecs = [q_w.reshape(1, -1), ksl_w.reshape(1, -1), kwn_w.reshape(1, -1)]
    out_widths = [(4 * GDN_WIDTH, F32), (LANES, F32), (2 * HEAD_DIM, F32), (NSA_WIDTH, BF16),
                  (ext, BF16), (ext, BF16), (HEAD_DIM, BF16), (ext, BF16)]
    return pl.pallas_call(
        functools.partial(_inproj_kernel, tm=tm, seq=seq),
        grid=(t // tm,),
        in_specs=[row(d),
                  pl.BlockSpec((1, 6, d), lambda i: (i // per_b, 0, 0)),
                  pl.BlockSpec((1, d), lambda i: (0, 0)),
                  full(w_in), full(gdn_conv_w), full(alog_p), full(dtb_p)] + [full(v) for v in vecs],
        out_specs=[row(w) for w, _ in out_widths],
        out_shape=[jax.ShapeDtypeStruct((t, w), dt) for w, dt in out_widths],
        scratch_shapes=[pltpu.VMEM((d, N_PROJ), BF16)] + [pltpu.VMEM((SUBLANES + tm, GDN_WIDTH), F32)] * 3,
        compiler_params=_params(("arbitrary",)),
        name="inproj",
    )(x2, mod3, norm1_w.reshape(1, d), w_in, gdn_conv_w, alog_p, dtb_p, *vecs)


def _gdn_kernel(q_ref, k_ref, v_ref, z_ref, sm_ref, onw_ref, o_ref,
                state_ref, pq_ref, p2_ref, oi_ref, os_ref, *, tile):
    c_len = GDN_CHUNK
    n_chunks = tile // c_len

    @pl.when(pl.program_id(1) == 0)
    def _():
        state_ref[...] = jnp.zeros_like(state_ref)

    qc = q_ref[...]
    kc = k_ref[...]
    vc = v_ref[...]
    g_all = sm_ref[...]
    beta_all = g_all

    ri = lax.broadcasted_iota(jnp.int32, (tile, tile), 0)
    ci = lax.broadcasted_iota(jnp.int32, (tile, tile), 1)
    lblk = jnp.where((ri // c_len == ci // c_len) & (ci <= ri), 1.0, 0.0)
    gc_all = _dot_exact_lhs(lblk, g_all)
    gc_t = gc_all.T
    gl_all = jnp.concatenate(
        [jnp.broadcast_to(gc_all[(c + 1) * c_len - 1:(c + 1) * c_len], (c_len, LANES))
         for c in range(n_chunks)], axis=0)
    eg_all = jnp.exp(gc_all)
    ekd_all = jnp.exp(gl_all - gc_all)
    egl_all = jnp.exp(gl_all)

    r64 = lax.broadcasted_iota(jnp.int32, (c_len, c_len), 0)
    c64 = lax.broadcasted_iota(jnp.int32, (c_len, c_len), 1)
    incl = r64 >= c64
    strict = r64 > c64
    eye = jnp.where(r64 == c64, 1.0, 0.0)

    def lane_bcast(x, lane):
        return jnp.broadcast_to(x[:, lane:lane + 1], (tile, HEAD_DIM))

    units = [(c, h) for c in range(n_chunks) for h in range(GDN_HEADS)]
    heads = []
    for h in range(GDN_HEADS):
        hl = slice(h * HEAD_DIM, (h + 1) * HEAD_DIM)
        qh = qc[:, hl]
        kh = kc[:, hl]
        bb = lane_bcast(beta_all, LANE_BETA + h)
        eg = lane_bcast(eg_all, LANE_A + h)
        kb = kh * bb
        heads.append(dict(qd=qh * eg, k=kh, kb=kb, q=qh, vb=vc[:, hl] * bb, kbe=kb * eg,
                          kd=kh * lane_bcast(ekd_all, LANE_A + h),
                          egl=lane_bcast(egl_all, LANE_A + h)))

    a_mats, attns = [], []
    for c, h in units:
        rows = slice(c * c_len, (c + 1) * c_len)
        hd = heads[h]
        gcol = gc_all[rows, LANE_A + h:LANE_A + h + 1]
        grow = gc_t[LANE_A + h:LANE_A + h + 1, rows]
        decay = jnp.exp(jnp.where(incl, gcol - grow, NEG_INF))
        gram = _dot_nt(jnp.concatenate([hd["kb"][rows], hd["q"][rows]], axis=0), hd["k"][rows])
        a_mats.append(jnp.where(strict, gram[:c_len] * decay, 0.0))
        attns.append(jnp.where(incl, gram[c_len:] * decay, 0.0))
    pack = LANES // c_len
    bd_mask = (lax.broadcasted_iota(jnp.int32, (LANES, LANES), 0) // c_len
               == lax.broadcasted_iota(jnp.int32, (LANES, LANES), 1) // c_len)
    eye_cat = jnp.concatenate([eye] * pack, axis=1)

    def block_diag(x):
        zero = jnp.zeros((LANES, LANES), BF16)
        hi, lo = (jnp.where(bd_mask, jnp.concatenate([part] * pack, axis=0), zero) for part in _split2(x))
        return jnp.concatenate([hi, hi], axis=0), lo

    def hp(a, b):
        ah, al = _split2(a)
        d = functools.partial(jnp.dot, preferred_element_type=F32)
        return d(jnp.concatenate([ah, al], axis=1), b[0]) + d(ah, b[1])

    n_levels = c_len.bit_length() - 2
    n_groups = len(units) // pack
    xs = [-jnp.concatenate(a_mats[g * pack:(g + 1) * pack], axis=1) for g in range(n_groups)]
    ts = [eye_cat + x for x in xs]
    xs = [hp(x, block_diag(x)) for x in xs]
    for level in range(1, n_levels):
        rs = [hp(jnp.concatenate([x, t], axis=0), block_diag(x)) for x, t in zip(xs, ts)]
        xs = [r[:c_len] for r in rs]
        ts = [t + r[c_len:] for t, r in zip(ts, rs)]
    ts = [t + hp(t, block_diag(x)) for x, t in zip(xs, ts)]
    t_mats = [t[:, j * c_len:(j + 1) * c_len] for t in ts for j in range(pack)]
    for i, (c, h) in enumerate(units):
        rows = slice(c * c_len, (c + 1) * c_len)
        hd = heads[h]
        uw = _dot(t_mats[i], jnp.concatenate([hd["vb"][rows], hd["kbe"][rows]], axis=1))
        aw = _dot(attns[i], uw)
        kw = _dot_tn(hd["kd"][rows], uw)
        pq_ref[i, :HEAD_DIM] = kw[:, HEAD_DIM:]
        pq_ref[i, HEAD_DIM:] = hd["qd"][rows] - aw[:, HEAD_DIM:]
        p2_ref[i] = kw[:, :HEAD_DIM]
        oi_ref[i] = aw[:, :HEAD_DIM]

    for i, (c, h) in enumerate(units):
        rows = slice(c * c_len, (c + 1) * c_len)
        state = state_ref[h]
        r = _dot(pq_ref[i], state)
        os_ref[rows, h * HEAD_DIM:(h + 1) * HEAD_DIM] = r[HEAD_DIM:] + oi_ref[i]
        egl = heads[h]["egl"][c * c_len:c * c_len + 1]
        state_ref[h] = state * egl - r[:HEAD_DIM] + p2_ref[i]

    for h in range(GDN_HEADS):
        hl = slice(h * HEAD_DIM, (h + 1) * HEAD_DIM)
        o = os_ref[:, hl]
        o = o * lax.rsqrt(jnp.mean(o * o, axis=-1, keepdims=True) + EPS) * onw_ref[...]
        o_ref[:, hl] = o * z_ref[:, hl]


def _gdn(gp, sm, out_norm_w, bsz, seq):
    tile = 256
    per_b = seq // tile
    n_units = tile // GDN_CHUNK * GDN_HEADS
    col = lambda c: pl.BlockSpec((tile, GDN_WIDTH), lambda b, s: (b * per_b + s, c))
    return pl.pallas_call(
        functools.partial(_gdn_kernel, tile=tile),
        grid=(bsz, per_b),
        in_specs=[col(0), col(1), col(2), col(3),
                  pl.BlockSpec((tile, LANES), lambda b, s: (b * per_b + s, 0)),
                  pl.BlockSpec((1, HEAD_DIM), lambda b, s: (0, 0))],
        out_specs=pl.BlockSpec((tile, GDN_WIDTH), lambda b, s: (b * per_b + s, 0)),
        out_shape=jax.ShapeDtypeStruct((bsz * seq, GDN_WIDTH), F32),
        scratch_shapes=[pltpu.VMEM((GDN_HEADS, HEAD_DIM, HEAD_DIM), F32),
                        pltpu.VMEM((n_units, HEAD_DIM + GDN_CHUNK, HEAD_DIM), F32),
                        pltpu.VMEM((n_units, HEAD_DIM, HEAD_DIM), F32),
                        pltpu.VMEM((n_units, GDN_CHUNK, HEAD_DIM), F32),
                        pltpu.VMEM((tile, GDN_WIDTH), F32)],
        compiler_params=_params(("arbitrary", "arbitrary")),
        name="gdn",
    )(gp, gp, gp, gp, sm, out_norm_w.reshape(1, HEAD_DIM))


def _compress_kernel(kc_ref, vc_ref, kpos_ref, kw1_ref, kw2_ref, knw_ref, vpos_ref, vw1_ref, vw2_ref,
                     ko_ref, vo_ref, *, n_cmp):
    ncp = ko_ref.shape[0]
    half = CMP_BLOCK // 2

    def one(x_ref, pos_ref, w1_ref, w2_ref):
        p = jnp.zeros((ncp, HEAD_DIM), F32)
        q = jnp.zeros((ncp, HEAD_DIM), F32)
        for j in range(half):
            a = x_ref[pl.ds(j, ncp, stride=CMP_STRIDE), :]
            p = p + _dot(a + pos_ref[j:j + 1, :], w1_ref[j * HEAD_DIM:(j + 1) * HEAD_DIM, :])
            q = q + _dot(a + pos_ref[half + j:half + j + 1, :],
                         w1_ref[(half + j) * HEAD_DIM:(half + j + 1) * HEAD_DIM, :])
        hid = _silu(p + pltpu.roll(q, ncp - 1, axis=0))
        out = _dot(hid, w2_ref[...])
        valid = lax.broadcasted_iota(jnp.int32, (ncp, HEAD_DIM), 0) < n_cmp
        return jnp.where(valid, out, 0.0)

    k = one(kc_ref, kpos_ref, kw1_ref, kw2_ref)
    ko_ref[...] = jnp.where(lax.broadcasted_iota(jnp.int32, k.shape, 0) < n_cmp,
                            _rms(k, knw_ref[...]), 0.0).astype(BF16)
    vo_ref[...] = one(vc_ref, vpos_ref, vw1_ref, vw2_ref).astype(BF16)


def _compress(cp, kpos, kw1, kw2, knw, vpos, vw1, vw2, bsz, seq):
    ncp = seq // CMP_STRIDE
    n_cmp = (seq - CMP_BLOCK) // CMP_STRIDE + 1
    full = lambda a: pl.BlockSpec(a.shape, lambda b: (0,) * a.ndim)
    knw2 = knw.reshape(1, HEAD_DIM)
    out = pl.BlockSpec((None, ncp, HEAD_DIM), lambda b: (b, 0, 0))
    shp = jax.ShapeDtypeStruct((bsz, ncp, HEAD_DIM), BF16)
    return pl.pallas_call(
        functools.partial(_compress_kernel, n_cmp=n_cmp),
        grid=(bsz,),
        in_specs=[pl.BlockSpec((seq, HEAD_DIM), lambda b: (b, 0)),
                  pl.BlockSpec((seq, HEAD_DIM), lambda b: (b, 1)),
                  full(kpos), full(kw1), full(kw2), full(knw2), full(vpos), full(vw1), full(vw2)],
        out_specs=[out, out],
        out_shape=[shp, shp],
        compiler_params=_params(("arbitrary",)),
        name="compress",
    )(cp, cp, kpos, kw1, kw2, knw2, vpos, vw1, vw2)


def _nsa_kernel(q_ref, kc_ref, vc_ref, ks_ref, vs_ref, kw_ref, vw_ref, sm_ref, ovt_ref, o_ref,
                m_ref, acc_ref, *, tq, tk, n_pick):
    qi = pl.program_id(1)
    t0 = qi * tq
    nh = NSA_HEADS
    ncp = kc_ref.shape[0]
    nsp = ovt_ref.shape[0]
    q = q_ref[...]
    q2 = jnp.concatenate([q[:, h * HEAD_DIM:(h + 1) * HEAD_DIM] for h in range(nh)], axis=0)
    gates = sm_ref[...]

    def gate(h, g):
        c = LANE_GATE + N_GATES * h + g
        return gates[:, c:c + 1]

    def head(x, h):
        return x[h * tq:(h + 1) * tq]

    t_c = t0 + lax.broadcasted_iota(jnp.int32, (tq, ncp), 0)
    cend = lax.broadcasted_iota(jnp.int32, (tq, ncp), 1) * CMP_STRIDE + (CMP_BLOCK - 1)
    bias_c = jnp.where(cend <= t_c, 0.0, NEG_INF)
    sees_any = (t0 + lax.broadcasted_iota(jnp.int32, (tq, 1), 0)) >= CMP_BLOCK - 1
    s_c = _dot_nt(q2, kc_ref[...])
    vcmp = vc_ref[...]
    o_c = []
    psum = jnp.zeros((tq, ncp), F32)
    for h in range(nh):
        s = head(s_c, h) + bias_c
        p = jnp.exp2(s - jnp.max(s, axis=-1, keepdims=True))
        l = jnp.sum(p, axis=-1, keepdims=True)
        p = p * jnp.where(sees_any, 1.0 / l, 0.0)
        psum = psum + p
        o_c.append(_dot(p, vcmp))
    ph, pl_ = _split2(psum)
    ovt = ovt_ref[...]
    dnt = functools.partial(lax.dot_general, dimension_numbers=(((1,), (1,)), ((), ())),
                            preferred_element_type=F32)
    imp_t = dnt(ovt, ph) + dnt(ovt, pl_)

    wlen = WINDOW + tq
    start = pl.multiple_of(jnp.maximum(t0 - WINDOW, 0), tq)
    kw = kw_ref[pl.ds(start, wlen), :]
    vw = vw_ref[pl.ds(start, wlen), :]
    s_w = _dot_nt(q2, kw)
    kpos_w = start + lax.broadcasted_iota(jnp.int32, (tq, wlen), 1)
    t_w = t0 + lax.broadcasted_iota(jnp.int32, (tq, wlen), 0)
    bias_w = jnp.where(kpos_w <= t_w, jnp.where(kpos_w > t_w - WINDOW, 0.0, NEG_INF), NEG_INF)
    o_w = []
    for h in range(nh):
        s = head(s_w, h) + bias_w
        p = jnp.exp2(s - jnp.max(s, axis=-1, keepdims=True))
        pv = _dot(p, vw)
        o_w.append(pv[:, :HEAD_DIM] / pv[:, HEAD_DIM:])

    jj = lax.broadcasted_iota(jnp.int32, (nsp, tq), 0)
    cur = (t0 + lax.broadcasted_iota(jnp.int32, (nsp, tq), 1)) // SEL_BLOCK
    forced = (jj == 0) | (jj == cur) | (jj == cur - 1)
    valid = jj <= cur
    notsel = jnp.where(valid & forced, 0.0, 1.0)
    work = jnp.where(valid, jnp.where(forced, -BIG, imp_t), -BIG)
    jf = jj.astype(F32)
    for _ in range(n_pick - 3):
        mx = jnp.max(work, axis=0, keepdims=True)
        cand = jnp.where(work == mx, jnp.where(mx > -1e38, jf, float(nsp)), float(nsp))
        first = jnp.min(cand, axis=0, keepdims=True)
        pick = jf == first
        notsel = jnp.where(pick, 0.0, notsel)
        work = jnp.where(pick, -BIG, work)
    notsel_q = notsel.T.astype(BF16)
    q2x = jnp.concatenate([q2, jnp.concatenate([notsel_q] * nh, axis=0)], axis=1)

    m_ref[...] = jnp.full(m_ref.shape, NEG_INF, F32)
    acc_ref[...] = jnp.zeros(acc_ref.shape, F32)
    n_kt = (t0 + tq - 1) // tk + 1

    def scores(kt):
        k0 = pl.multiple_of(kt * tk, tk)
        return _dot_nt(q2x, ks_ref[pl.ds(k0, tk), :])

    def probs(kt, s, causal):
        if causal:
            kpos = kt * tk + lax.broadcasted_iota(jnp.int32, (tq, tk), 1)
            t_s = t0 + lax.broadcasted_iota(jnp.int32, (tq, tk), 0)
            bias = jnp.where(kpos <= t_s, 0.0, NEG_INF)
        ps, alphas = [], []
        for h in range(nh):
            rows = slice(h * tq, (h + 1) * tq)
            sh = head(s, h) + bias if causal else head(s, h)
            m_prev = m_ref[rows]
            m_next = jnp.maximum(m_prev, jnp.max(sh, axis=-1, keepdims=True))
            alphas.append(jnp.exp2(m_prev - m_next))
            ps.append(jnp.exp2(sh - jnp.concatenate([m_next] * (tk // LANES), axis=1)).astype(BF16))
            m_ref[rows] = m_next
        return jnp.concatenate(ps, axis=0), jnp.concatenate(alphas, axis=0)

    def accumulate(kt, p, alpha):
        k0 = pl.multiple_of(kt * tk, tk)
        pv = jnp.dot(p, vs_ref[pl.ds(k0, tk), :], preferred_element_type=F32)
        acc_ref[...] = jnp.concatenate([alpha, alpha], axis=1) * acc_ref[...] + pv

    def sweep_body(kt, s_cur):
        s_next = scores(kt + 1)
        accumulate(kt, *probs(kt, s_cur, False))
        return s_next

    s_last = lax.fori_loop(0, n_kt - 1, sweep_body, scores(0))
    accumulate(n_kt - 1, *probs(n_kt - 1, s_last, True))

    for h in range(nh):
        rows = slice(h * tq, (h + 1) * tq)
        l_s = acc_ref[rows, HEAD_DIM:]
        o_s = acc_ref[rows, :HEAD_DIM] * jnp.where(l_s > 0.0, 1.0 / l_s, 0.0)
        o_ref[:, h * HEAD_DIM:(h + 1) * HEAD_DIM] = (gate(h, 0) * o_c[h] + gate(h, 1) * o_s
                                                     + gate(h, 2) * o_w[h])


def _nsa(qn, kcmp, vcmp, ksl, vsl, kwn, vwn, sm, overlap, bsz, seq):
    tq, tk = 256, 512
    per_b = seq // tq
    ncp = kcmp.shape[1]
    nsp = overlap.shape[0]
    n_pick = min(N_SELECT, seq // SEL_BLOCK)
    assert n_pick == N_SELECT and seq % tk == 0 and seq >= WINDOW + tq and nsp == LANES
    ext = HEAD_DIM + LANES
    seq_kv = lambda w: pl.BlockSpec((seq, w), lambda b, i: (b, 0))
    cmp_kv = lambda: pl.BlockSpec((None, ncp, HEAD_DIM), lambda b, i: (b, 0, 0))
    return pl.pallas_call(
        functools.partial(_nsa_kernel, tq=tq, tk=tk, n_pick=n_pick),
        grid=(bsz, per_b),
        in_specs=[pl.BlockSpec((tq, NSA_WIDTH), lambda b, i: (b * per_b + i, 0)),
                  cmp_kv(), cmp_kv(), seq_kv(ext), seq_kv(ext), seq_kv(HEAD_DIM), seq_kv(ext),
                  pl.BlockSpec((tq, LANES), lambda b, i: (b * per_b + i, 0)),
                  pl.BlockSpec((nsp, ncp), lambda b, i: (0, 0))],
        out_specs=pl.BlockSpec((tq, NSA_WIDTH), lambda b, i: (b * per_b + i, 0)),
        out_shape=jax.ShapeDtypeStruct((bsz * seq, NSA_WIDTH), F32),
        scratch_shapes=[pltpu.VMEM((NSA_HEADS * tq, LANES), F32),
                        pltpu.VMEM((NSA_HEADS * tq, ext), F32)],
        compiler_params=_params(("arbitrary", "arbitrary")),
        name="nsa",
    )(qn, kcmp, vcmp, ksl, vsl, kwn, vwn, sm, overlap)


def _overlap_matrix(seq):
    n_cmp = (seq - CMP_BLOCK) // CMP_STRIDE + 1
    n_sel = seq // SEL_BLOCK
    ncp = seq // CMP_STRIDE
    nsp = -(-n_sel // LANES) * LANES
    cs = np.arange(n_cmp) * CMP_STRIDE
    ss = np.arange(n_sel) * SEL_BLOCK
    ov = np.minimum(cs[:, None] + CMP_BLOCK, ss[None, :] + SEL_BLOCK) - np.maximum(cs[:, None], ss[None, :])
    out = np.zeros((nsp, ncp), np.float32)
    out[:n_sel, :n_cmp] = (np.clip(ov, 0, None).astype(np.float32) / CMP_BLOCK).T
    return jnp.asarray(out, dtype=BF16)


def _ffn_kernel(x_ref, og_ref, on_ref, mod_ref, wo_ref, nw_ref, wu_ref, cw_ref, cb_ref, wd_ref, o_ref,
                up_ref, act_ref, *, tm, d_ff, cw, down_parts):
    head = SUBLANES

    @pl.when(pl.program_id(1) == 0)
    def _():
        up_ref[:, :head, :] = jnp.zeros((2, head, d_ff), F32)

    m = mod_ref[0]
    mixed = _dot(og_ref[...], wo_ref[:GDN_WIDTH, :]) + _dot(on_ref[...], wo_ref[GDN_WIDTH:, :])
    x1 = x_ref[...] + m[2:3] * mixed
    h2 = _rms_mod(x1, nw_ref[...], m[4:5], m[3:4]).astype(BF16)
    n_chunks = d_ff // cw
    y = None
    done = 0
    for c in range(n_chunks):
        cols = slice(c * cw, (c + 1) * cw)
        halves = []
        for g in range(2):
            wcols = slice(g * d_ff + c * cw, g * d_ff + (c + 1) * cw)
            up_ref[g, head:, cols] = jnp.dot(h2, wu_ref[:, wcols], preferred_element_type=F32)
            z = cb_ref[:, wcols]
            for j in range(FFN_CONV):
                tap = up_ref[g, pl.ds(head - (FFN_CONV - 1) + j, tm), cols]
                z = z + cw_ref[j:j + 1, wcols] * tap
            up_ref[g, :head, cols] = up_ref[g, tm:tm + head, cols]
            halves.append(z)
        act_ref[:, cols] = (_silu(halves[0]) * halves[1]).astype(BF16)
        if (c + 1) * down_parts // n_chunks > c * down_parts // n_chunks:
            part = slice(done * cw, (c + 1) * cw)
            d = jnp.dot(act_ref[:, part], wd_ref[part, :], preferred_element_type=F32)
            y = d if y is None else y + d
            done = c + 1
    o_ref[...] = x1 + m[5:6] * y


def _ffn(x2, o_g, o_n, mod3, w_out_b, norm2_w, w_up_b, conv_w, conv_b, w_down_b, bsz, seq):
    t, d = x2.shape
    d_ff = w_down_b.shape[0]
    tm, cw = 512, 256
    assert d_ff % cw == 0
    per_b = seq // tm
    full = lambda a: pl.BlockSpec(a.shape, lambda b, s: (0,) * a.ndim)
    conv_b2 = conv_b.reshape(1, -1)
    norm2_w2 = norm2_w.reshape(1, d)
    return pl.pallas_call(
        functools.partial(_ffn_kernel, tm=tm, d_ff=d_ff, cw=cw, down_parts=1),
        grid=(bsz, per_b),
        in_specs=[pl.BlockSpec((tm, d), lambda b, s: (b * per_b + s, 0)),
                  pl.BlockSpec((tm, GDN_WIDTH), lambda b, s: (b * per_b + s, 0)),
                  pl.BlockSpec((tm, NSA_WIDTH), lambda b, s: (b * per_b + s, 0)),
                  pl.BlockSpec((1, 6, d), lambda b, s: (b, 0, 0)),
                  full(w_out_b), full(norm2_w2), full(w_up_b), full(conv_w), full(conv_b2),
                  full(w_down_b)],
        out_specs=pl.BlockSpec((tm, d), lambda b, s: (b * per_b + s, 0)),
        out_shape=jax.ShapeDtypeStruct((t, d), F32),
        scratch_shapes=[pltpu.VMEM((2, SUBLANES + tm, d_ff), F32),
                        pltpu.VMEM((tm, d_ff), BF16)],
        compiler_params=_params(("arbitrary", "arbitrary")),
        name="ffn",
    )(x2, o_g, o_n, mod3, w_out_b, norm2_w2, w_up_b, conv_w, conv_b2, w_down_b)


def _pad_lanes(a, n):
    return jnp.pad(a, ((0, 0), (0, n - a.shape[1])))


def _layer(x, c, ada_w, ada_b, norm1_w, w_in, gdn_conv_w, gdn_A_log, gdn_dt_bias, gdn_out_norm_w,
           nsa_q_norm_w, nsa_k_norm_cmp, nsa_k_norm_slc, nsa_k_norm_win, cmp_k_pos, cmp_k_w1, cmp_k_w2,
           cmp_v_pos, cmp_v_w1, cmp_v_w2, w_out, norm2_w, ffn_w_up, ffn_conv_w, ffn_conv_b, ffn_w_down):
    bsz, seq, d = x.shape
    x2 = x.reshape(bsz * seq, d)
    mod3 = _ada(c, ada_w, ada_b).reshape(bsz, 6, d)

    gp, sm, cp, qn, ksl, vsl, kwn, vwn = _inproj(
        x2, mod3, norm1_w, w_in, gdn_conv_w, _pad_lanes(gdn_A_log.reshape(1, -1), LANES),
        _pad_lanes(gdn_dt_bias.reshape(1, -1), LANES), nsa_q_norm_w, nsa_k_norm_slc, nsa_k_norm_win, seq)

    o_g = _gdn(gp, sm, gdn_out_norm_w, bsz, seq)

    kcmp, vcmp = _compress(cp, cmp_k_pos, cmp_k_w1.astype(BF16), cmp_k_w2.astype(BF16), nsa_k_norm_cmp,
                           cmp_v_pos, cmp_v_w1.astype(BF16), cmp_v_w2.astype(BF16), bsz, seq)
    o_n = _nsa(qn, kcmp, vcmp, ksl, vsl, kwn, vwn, sm, _overlap_matrix(seq), bsz, seq)

    out = _ffn(x2, o_g, o_n, mod3, w_out.astype(BF16), norm2_w, ffn_w_up.astype(BF16), ffn_conv_w,
               ffn_conv_b, ffn_w_down.astype(BF16), bsz, seq)
    return out.reshape(bsz, seq, d)


def kernel(x, c, ada_w, ada_b, norm1_w, w_in, gdn_conv_w, gdn_A_log, gdn_dt_bias, gdn_out_norm_w, nsa_q_norm_w, nsa_k_norm_cmp, nsa_k_norm_slc, nsa_k_norm_win, cmp_k_pos, cmp_k_w1, cmp_k_w2, cmp_v_pos, cmp_v_w1, cmp_v_w2, w_out, norm2_w, ffn_w_up, ffn_conv_w, ffn_conv_b, ffn_w_down):
    for l in range(ada_w.shape[0]):
        x = _layer(x, c, ada_w[l], ada_b[l], norm1_w[l], w_in[l], gdn_conv_w[l], gdn_A_log[l],
                   gdn_dt_bias[l], gdn_out_norm_w[l], nsa_q_norm_w[l], nsa_k_norm_cmp[l],
                   nsa_k_norm_slc[l], nsa_k_norm_win[l], cmp_k_pos[l], cmp_k_w1[l], cmp_k_w2[l],
                   cmp_v_pos[l], cmp_v_w1[l], cmp_v_w2[l], w_out[l], norm2_w[l], ffn_w_up[l],
                   ffn_conv_w[l], ffn_conv_b[l], ffn_w_down[l])
    return x
```

```python
import functools

import numpy as np
import jax
import jax.numpy as jnp
from jax import lax
from jax.experimental import pallas as pl
from jax.experimental.pallas import tpu as pltpu

F32 = jnp.float32
BF16 = jnp.bfloat16

HEAD_DIM = 128
GDN_HEADS = 4
NSA_HEADS = 4
GDN_WIDTH = GDN_HEADS * HEAD_DIM
NSA_WIDTH = NSA_HEADS * HEAD_DIM
GDN_CONV = 4
GDN_CHUNK = 64
CMP_BLOCK = 32
CMP_STRIDE = 16
SEL_BLOCK = 64
N_SELECT = 16
WINDOW = 512
FFN_CONV = 3
N_GATES = 3
EPS = 1e-6
NEG_INF = -1e30
BIG = 3e38
LOG2E = 1.4426950408889634

LANES = 128
SUBLANES = 8
VMEM_LIMIT = 56 * 1024 * 1024

COL_GQ, COL_GK, COL_GV, COL_GZ = 0, 512, 1024, 1536
COL_NQ = 2048
COL_KC, COL_VC, COL_KSL, COL_VSL, COL_KWN, COL_VWN = 2560, 2688, 2816, 2944, 3072, 3200
COL_SMALL = 3328
N_PROJ = 3456
LANE_A, LANE_BETA, LANE_GATE = 0, 4, 8


def _params(sem):
    return pltpu.CompilerParams(dimension_semantics=sem, vmem_limit_bytes=VMEM_LIMIT)


def _dot(a, b):
    return jnp.dot(a.astype(BF16), b.astype(BF16), preferred_element_type=F32)


def _dot_nt(a, b):
    return lax.dot_general(a.astype(BF16), b.astype(BF16), (((1,), (1,)), ((), ())),
                           preferred_element_type=F32)


def _dot_tn(a, b):
    return lax.dot_general(a.astype(BF16), b.astype(BF16), (((0,), (0,)), ((), ())),
                           preferred_element_type=F32)


def _split2(a):
    hi = a.astype(BF16)
    lo = (a - hi.astype(F32)).astype(BF16)
    return hi, lo


def _split3(a):
    h1 = a.astype(BF16)
    r = a - h1.astype(F32)
    h2 = r.astype(BF16)
    h3 = (r - h2.astype(F32)).astype(BF16)
    return h1, h2, h3


def _dot_hp(a, b):
    ah, al = _split2(a)
    bh, bl = _split2(b)
    d = functools.partial(jnp.dot, preferred_element_type=F32)
    return d(ah, bh) + (d(ah, bl) + d(al, bh))


def _dot_exact_lhs(a_exact, b):
    a = a_exact.astype(BF16)
    b1, b2, b3 = _split3(b)
    d = functools.partial(jnp.dot, preferred_element_type=F32)
    return d(a, b1) + (d(a, b2) + d(a, b3))


def _sigmoid(x):
    return 1.0 / (1.0 + jnp.exp(-x))


def _silu(x):
    h = 0.5 * x
    return h + h * jnp.tanh(h)


def _softplus(x):
    return jnp.maximum(x, 0.0) + jnp.log1p(jnp.exp(-jnp.abs(x)))


def _ada_kernel(c_ref, w_ref, b_ref, o_ref):
    c = c_ref[...]
    o_ref[...] = _dot_hp(_silu(c), w_ref[...]) + b_ref[...]


def _ada(c, ada_w, ada_b):
    bsz, d = c.shape
    n = ada_w.shape[1]
    tn = 1024
    return pl.pallas_call(
        _ada_kernel,
        grid=(n // tn,),
        in_specs=[pl.BlockSpec((bsz, d), lambda j: (0, 0)),
                  pl.BlockSpec((d, tn), lambda j: (0, j)),
                  pl.BlockSpec((1, tn), lambda j: (0, j))],
        out_specs=pl.BlockSpec((bsz, tn), lambda j: (0, j)),
        out_shape=jax.ShapeDtypeStruct((bsz, n), F32),
        compiler_params=_params(("arbitrary",)),
        name="ada",
    )(c, ada_w, ada_b.reshape(1, n))


def _rms_mod(x, nw, scale, shift):
    y = x * lax.rsqrt(jnp.mean(x * x, axis=-1, keepdims=True) + EPS) * nw
    return y * (1.0 + scale) + shift


def _rms(x, w):
    return x * lax.rsqrt(jnp.mean(x * x, axis=-1, keepdims=True) + EPS) * w


def _inproj_kernel(x_ref, mod_ref, nw_ref, w_in_ref, cw_ref, alog_ref, dtb_ref, qw_ref, kslw_ref,
                   kwnw_ref, gp_ref, sm_ref, cp_ref, qn_ref, kext_ref, vext_ref, kwn_ref, vwext_ref,
                   w_ref, *pre_refs, tm, seq):
    head = SUBLANES
    i = pl.program_id(0)

    @pl.when(i == 0)
    def _():
        n_small = 2 * GDN_HEADS
        n_mid = COL_SMALL - COL_NQ
        n_gate = N_GATES * NSA_HEADS
        rb = LANES
        for r0 in range(0, w_in_ref.shape[0], rb):
            rows = slice(r0, r0 + rb)
            w_ref[rows, :COL_NQ] = w_in_ref[rows, :COL_NQ].astype(BF16)
            w_ref[rows, COL_NQ:COL_SMALL] = w_in_ref[rows, COL_NQ + n_small:COL_NQ + n_small + n_mid].astype(BF16)
            tail = jnp.concatenate(
                [w_in_ref[rows, COL_NQ:COL_NQ + n_small],
                 w_in_ref[rows, COL_NQ + n_small + n_mid:COL_NQ + n_small + n_mid + n_gate],
                 jnp.zeros((rb, LANES - n_small - n_gate), F32)], axis=1)
            w_ref[rows, COL_SMALL:] = tail.astype(BF16)

    @pl.when(i % (seq // tm) == 0)
    def _():
        for pre_ref in pre_refs:
            pre_ref[:head, :] = jnp.zeros((head, pre_ref.shape[1]), F32)

    m = mod_ref[0]
    h = _rms_mod(x_ref[...], nw_ref[...], m[1:2], m[0:1]).astype(BF16)

    def proj(c0, width):
        return jnp.dot(h, w_ref[:, c0:c0 + width], preferred_element_type=F32)

    for idx in range(3):
        cols = slice(idx * GDN_WIDTH, (idx + 1) * GDN_WIDTH)
        pre_ref = pre_refs[idx]
        pre_ref[head:, :] = proj(COL_GQ + idx * GDN_WIDTH, GDN_WIDTH)
        y = cw_ref[GDN_CONV - 1:GDN_CONV, cols] * pre_ref[head:, :]
        for j in range(GDN_CONV - 1):
            y = y + cw_ref[j:j + 1, cols] * pre_ref[pl.ds(head - (GDN_CONV - 1) + j, tm), :]
        pre_ref[:head, :] = pre_ref[tm:tm + head, :]
        y = _silu(y)
        for hh in range(GDN_HEADS):
            hl = slice(hh * HEAD_DIM, (hh + 1) * HEAD_DIM)
            yh = y[:, hl]
            if idx == 0:
                yh = yh * (lax.rsqrt(jnp.sum(yh * yh, axis=-1, keepdims=True) + EPS) * HEAD_DIM ** -0.5)
            elif idx == 1:
                yh = yh * lax.rsqrt(jnp.sum(yh * yh, axis=-1, keepdims=True) + EPS)
            gp_ref[:, idx * GDN_WIDTH + hh * HEAD_DIM:idx * GDN_WIDTH + (hh + 1) * HEAD_DIM] = yh
    gp_ref[:, COL_GZ:COL_GZ + GDN_WIDTH] = _silu(proj(COL_GZ, GDN_WIDTH))

    nq = proj(COL_NQ, NSA_WIDTH)
    for hh in range(NSA_HEADS):
        hl = slice(hh * HEAD_DIM, (hh + 1) * HEAD_DIM)
        qn_ref[:, hl] = (_rms(nq[:, hl], qw_ref[...]) * (HEAD_DIM ** -0.5 * LOG2E)).astype(BF16)
    r = proj(COL_KC, 4 * HEAD_DIM)
    cp_ref[...] = r[:, :2 * HEAD_DIM]
    ones = jnp.ones((tm, LANES), BF16)
    pos = (i * tm + lax.broadcasted_iota(jnp.int32, (tm, LANES), 0)) % seq
    lane = lax.broadcasted_iota(jnp.int32, (tm, LANES), 1)
    kext_ref[:, :HEAD_DIM] = _rms(r[:, 2 * HEAD_DIM:3 * HEAD_DIM], kslw_ref[...]).astype(BF16)
    kext_ref[:, HEAD_DIM:] = jnp.where(lane == pos // SEL_BLOCK, NEG_INF, 0.0).astype(BF16)
    vext_ref[:, :HEAD_DIM] = r[:, 3 * HEAD_DIM:].astype(BF16)
    vext_ref[:, HEAD_DIM:] = ones
    r = proj(COL_KWN, 2 * HEAD_DIM + LANES)
    kwn_ref[...] = _rms(r[:, :HEAD_DIM], kwnw_ref[...]).astype(BF16)
    vwext_ref[:, :HEAD_DIM] = r[:, HEAD_DIM:2 * HEAD_DIM].astype(BF16)
    vwext_ref[:, HEAD_DIM:] = ones
    small = r[:, 2 * HEAD_DIM:]
    log_decay = -jnp.exp(alog_ref[...]) * _softplus(small + dtb_ref[...])
    sm_ref[...] = jnp.where(lane < LANE_BETA, log_decay, _sigmoid(small))


def _inproj(x2, mod3, norm1_w, w_in, gdn_conv_w, alog_p, dtb_p, q_w, ksl_w, kwn_w, seq):
    t, d = x2.shape
    assert w_in.shape[1] == COL_SMALL + 2 * GDN_HEADS + N_GATES * NSA_HEADS
    tm = 512
    per_b = seq // tm
    assert seq // SEL_BLOCK <= LANES
    ext = HEAD_DIM + LANES
    full = lambda a: pl.BlockSpec(a.shape, lambda i: (0,) * a.ndim)
    row = lambda w: pl.BlockSpec((tm, w), lambda i: (i, 0))
    vecs = [q_w.reshape(1, -1), ksl_w.reshape(1, -1), kwn_w.reshape(1, -1)]
    out_widths = [(4 * GDN_WIDTH, F32), (LANES, F32), (2 * HEAD_DIM, F32), (NSA_WIDTH, BF16),
                  (ext, BF16), (ext, BF16), (HEAD_DIM, BF16), (ext, BF16)]
    return pl.pallas_call(
        functools.partial(_inproj_kernel, tm=tm, seq=seq),
        grid=(t // tm,),
        in_specs=[row(d),
                  pl.BlockSpec((1, 6, d), lambda i: (i // per_b, 0, 0)),
                  pl.BlockSpec((1, d), lambda i: (0, 0)),
                  full(w_in), full(gdn_conv_w), full(alog_p), full(dtb_p)] + [full(v) for v in vecs],
        out_specs=[row(w) for w, _ in out_widths],
        out_shape=[jax.ShapeDtypeStruct((t, w), dt) for w, dt in out_widths],
        scratch_shapes=[pltpu.VMEM((d, N_PROJ), BF16)] + [pltpu.VMEM((SUBLANES + tm, GDN_WIDTH), F32)] * 3,
        compiler_params=_params(("arbitrary",)),
        name="inproj",
    )(x2, mod3, norm1_w.reshape(1, d), w_in, gdn_conv_w, alog_p, dtb_p, *vecs)


def _gdn_kernel(q_ref, k_ref, v_ref, z_ref, sm_ref, onw_ref, o_ref,
                state_ref, pq_ref, p2_ref, oi_ref, os_ref, *, tile):
    c_len = GDN_CHUNK
    n_chunks = tile // c_len

    @pl.when(pl.program_id(1) == 0)
    def _():
        state_ref[...] = jnp.zeros_like(state_ref)

    qc = q_ref[...]
    kc = k_ref[...]
    vc = v_ref[...]
    g_all = sm_ref[...]
    beta_all = g_all

    ri = lax.broadcasted_iota(jnp.int32, (tile, tile), 0)
    ci = lax.broadcasted_iota(jnp.int32, (tile, tile), 1)
    lblk = jnp.where((ri // c_len == ci // c_len) & (ci <= ri), 1.0, 0.0)
    gc_all = _dot_exact_lhs(lblk, g_all)
    gc_t = gc_all.T
    gl_all = jnp.concatenate(
        [jnp.broadcast_to(gc_all[(c + 1) * c_len - 1:(c + 1) * c_len], (c_len, LANES))
         for c in range(n_chunks)], axis=0)
    eg_all = jnp.exp(gc_all)
    ekd_all = jnp.exp(gl_all - gc_all)
    egl_all = jnp.exp(gl_all)

    r64 = lax.broadcasted_iota(jnp.int32, (c_len, c_len), 0)
    c64 = lax.broadcasted_iota(jnp.int32, (c_len, c_len), 1)
    incl = r64 >= c64
    strict = r64 > c64
    eye = jnp.where(r64 == c64, 1.0, 0.0)

    def lane_bcast(x, lane):
        return jnp.broadcast_to(x[:, lane:lane + 1], (tile, HEAD_DIM))

    units = [(c, h) for c in range(n_chunks) for h in range(GDN_HEADS)]
    heads = []
    for h in range(GDN_HEADS):
        hl = slice(h * HEAD_DIM, (h + 1) * HEAD_DIM)
        qh = qc[:, hl]
        kh = kc[:, hl]
        bb = lane_bcast(beta_all, LANE_BETA + h)
        eg = lane_bcast(eg_all, LANE_A + h)
        kb = kh * bb
        heads.append(dict(qd=qh * eg, k=kh, kb=kb, q=qh, vb=vc[:, hl] * bb, kbe=kb * eg,
                          kd=kh * lane_bcast(ekd_all, LANE_A + h),
                          egl=lane_bcast(egl_all, LANE_A + h)))

    a_mats, attns = [], []
    for c, h in units:
        rows = slice(c * c_len, (c + 1) * c_len)
        hd = heads[h]
        gcol = gc_all[rows, LANE_A + h:LANE_A + h + 1]
        grow = gc_t[LANE_A + h:LANE_A + h + 1, rows]
        decay = jnp.exp(jnp.where(incl, gcol - grow, NEG_INF))
        gram = _dot_nt(jnp.concatenate([hd["kb"][rows], hd["q"][rows]], axis=0), hd["k"][rows])
        a_mats.append(jnp.where(strict, gram[:c_len] * decay, 0.0))
        attns.append(jnp.where(incl, gram[c_len:] * decay, 0.0))
    pack = LANES // c_len
    bd_mask = (lax.broadcasted_iota(jnp.int32, (LANES, LANES), 0) // c_len
               == lax.broadcasted_iota(jnp.int32, (LANES, LANES), 1) // c_len)
    eye_cat = jnp.concatenate([eye] * pack, axis=1)

    def block_diag(x):
        zero = jnp.zeros((LANES, LANES), BF16)
        hi, lo = (jnp.where(bd_mask, jnp.concatenate([part] * pack, axis=0), zero) for part in _split2(x))
        return jnp.concatenate([hi, hi], axis=0), lo

    def hp(a, b):
        ah, al = _split2(a)
        d = functools.partial(jnp.dot, preferred_element_type=F32)
        return d(jnp.concatenate([ah, al], axis=1), b[0]) + d(ah, b[1])

    n_levels = c_len.bit_length() - 2
    n_groups = len(units) // pack
    xs = [-jnp.concatenate(a_mats[g * pack:(g + 1) * pack], axis=1) for g in range(n_groups)]
    ts = [eye_cat + x for x in xs]
    xs = [hp(x, block_diag(x)) for x in xs]
    for level in range(1, n_levels):
        rs = [hp(jnp.concatenate([x, t], axis=0), block_diag(x)) for x, t in zip(xs, ts)]
        xs = [r[:c_len] for r in rs]
        ts = [t + r[c_len:] for t, r in zip(ts, rs)]
    ts = [t + hp(t, block_diag(x)) for x, t in zip(xs, ts)]
    t_mats = [t[:, j * c_len:(j + 1) * c_len] for t in ts for j in range(pack)]
    for i, (c, h) in enumerate(units):
        rows = slice(c * c_len, (c + 1) * c_len)
        hd = heads[h]
        uw = _dot(t_mats[i], jnp.concatenate([hd["vb"][rows], hd["kbe"][rows]], axis=1))
        aw = _dot(attns[i], uw)
        kw = _dot_tn(hd["kd"][rows], uw)
        pq_ref[i, :HEAD_DIM] = kw[:, HEAD_DIM:]
        pq_ref[i, HEAD_DIM:] = hd["qd"][rows] - aw[:, HEAD_DIM:]
        p2_ref[i] = kw[:, :HEAD_DIM]
        oi_ref[i] = aw[:, :HEAD_DIM]

    for i, (c, h) in enumerate(units):
        rows = slice(c * c_len, (c + 1) * c_len)
        state = state_ref[h]
        r = _dot(pq_ref[i], state)
        os_ref[rows, h * HEAD_DIM:(h + 1) * HEAD_DIM] = r[HEAD_DIM:] + oi_ref[i]
        egl = heads[h]["egl"][c * c_len:c * c_len + 1]
        state_ref[h] = state * egl - r[:HEAD_DIM] + p2_ref[i]

    for h in range(GDN_HEADS):
        hl = slice(h * HEAD_DIM, (h + 1) * HEAD_DIM)
        o = os_ref[:, hl]
        o = o * lax.rsqrt(jnp.mean(o * o, axis=-1, keepdims=True) + EPS) * onw_ref[...]
        o_ref[:, hl] = o * z_ref[:, hl]


def _gdn(gp, sm, out_norm_w, bsz, seq):
    tile = 256
    per_b = seq // tile
    n_units = tile // GDN_CHUNK * GDN_HEADS
    col = lambda c: pl.BlockSpec((tile, GDN_WIDTH), lambda b, s: (b * per_b + s, c))
    return pl.pallas_call(
        functools.partial(_gdn_kernel, tile=tile),
        grid=(bsz, per_b),
        in_specs=[col(0), col(1), col(2), col(3),
                  pl.BlockSpec((tile, LANES), lambda b, s: (b * per_b + s, 0)),
                  pl.BlockSpec((1, HEAD_DIM), lambda b, s: (0, 0))],
        out_specs=pl.BlockSpec((tile, GDN_WIDTH), lambda b, s: (b * per_b + s, 0)),
        out_shape=jax.ShapeDtypeStruct((bsz * seq, GDN_WIDTH), F32),
        scratch_shapes=[pltpu.VMEM((GDN_HEADS, HEAD_DIM, HEAD_DIM), F32),
                        pltpu.VMEM((n_units, HEAD_DIM + GDN_CHUNK, HEAD_DIM), F32),
                        pltpu.VMEM((n_units, HEAD_DIM, HEAD_DIM), F32),
                        pltpu.VMEM((n_units, GDN_CHUNK, HEAD_DIM), F32),
                        pltpu.VMEM((tile, GDN_WIDTH), F32)],
        compiler_params=_params(("arbitrary", "arbitrary")),
        name="gdn",
    )(gp, gp, gp, gp, sm, out_norm_w.reshape(1, HEAD_DIM))


def _compress_kernel(kc_ref, vc_ref, kpos_ref, kw1_ref, kw2_ref, knw_ref, vpos_ref, vw1_ref, vw2_ref,
                     ko_ref, vo_ref, *, n_cmp):
    ncp = ko_ref.shape[0]
    half = CMP_BLOCK // 2

    def one(x_ref, pos_ref, w1_ref, w2_ref):
        p = jnp.zeros((ncp, HEAD_DIM), F32)
        q = jnp.zeros((ncp, HEAD_DIM), F32)
        for j in range(half):
            a = x_ref[pl.ds(j, ncp, stride=CMP_STRIDE), :]
            p = p + _dot(a + pos_ref[j:j + 1, :], w1_ref[j * HEAD_DIM:(j + 1) * HEAD_DIM, :])
            q = q + _dot(a + pos_ref[half + j:half + j + 1, :],
                         w1_ref[(half + j) * HEAD_DIM:(half + j + 1) * HEAD_DIM, :])
        hid = _silu(p + pltpu.roll(q, ncp - 1, axis=0))
        out = _dot(hid, w2_ref[...])
        valid = lax.broadcasted_iota(jnp.int32, (ncp, HEAD_DIM), 0) < n_cmp
        return jnp.where(valid, out, 0.0)

    k = one(kc_ref, kpos_ref, kw1_ref, kw2_ref)
    ko_ref[...] = jnp.where(lax.broadcasted_iota(jnp.int32, k.shape, 0) < n_cmp,
                            _rms(k, knw_ref[...]), 0.0).astype(BF16)
    vo_ref[...] = one(vc_ref, vpos_ref, vw1_ref, vw2_ref).astype(BF16)


def _compress(cp, kpos, kw1, kw2, knw, vpos, vw1, vw2, bsz, seq):
    ncp = seq // CMP_STRIDE
    n_cmp = (seq - CMP_BLOCK) // CMP_STRIDE + 1
    full = lambda a: pl.BlockSpec(a.shape, lambda b: (0,) * a.ndim)
    knw2 = knw.reshape(1, HEAD_DIM)
    out = pl.BlockSpec((None, ncp, HEAD_DIM), lambda b: (b, 0, 0))
    shp = jax.ShapeDtypeStruct((bsz, ncp, HEAD_DIM), BF16)
    return pl.pallas_call(
        functools.partial(_compress_kernel, n_cmp=n_cmp),
        grid=(bsz,),
        in_specs=[pl.BlockSpec((seq, HEAD_DIM), lambda b: (b, 0)),
                  pl.BlockSpec((seq, HEAD_DIM), lambda b: (b, 1)),
                  full(kpos), full(kw1), full(kw2), full(knw2), full(vpos), full(vw1), full(vw2)],
        out_specs=[out, out],
        out_shape=[shp, shp],
        compiler_params=_params(("arbitrary",)),
        name="compress",
    )(cp, cp, kpos, kw1, kw2, knw2, vpos, vw1, vw2)


def _nsa_kernel(q_ref, kc_ref, vc_ref, ks_ref, vs_ref, kw_ref, vw_ref, sm_ref, ovt_ref, o_ref,
                m_ref, acc_ref, *, tq, tk, n_pick):
    qi = pl.program_id(1)
    t0 = qi * tq
    nh = NSA_HEADS
    ncp = kc_ref.shape[0]
    nsp = ovt_ref.shape[0]
    q = q_ref[...]
    q2 = jnp.concatenate([q[:, h * HEAD_DIM:(h + 1) * HEAD_DIM] for h in range(nh)], axis=0)
    gates = sm_ref[...]

    def gate(h, g):
        c = LANE_GATE + N_GATES * h + g
        return gates[:, c:c + 1]

    def head(x, h):
        return x[h * tq:(h + 1) * tq]

    t_c = t0 + lax.broadcasted_iota(jnp.int32, (tq, ncp), 0)
    cend = lax.broadcasted_iota(jnp.int32, (tq, ncp), 1) * CMP_STRIDE + (CMP_BLOCK - 1)
    bias_c = jnp.where(cend <= t_c, 0.0, NEG_INF)
    sees_any = (t0 + lax.broadcasted_iota(jnp.int32, (tq, 1), 0)) >= CMP_BLOCK - 1
    s_c = _dot_nt(q2, kc_ref[...])
    vcmp = vc_ref[...]
    o_c = []
    psum = jnp.zeros((tq, ncp), F32)
    for h in range(nh):
        s = head(s_c, h) + bias_c
        p = jnp.exp2(s - jnp.max(s, axis=-1, keepdims=True))
        l = jnp.sum(p, axis=-1, keepdims=True)
        p = p * jnp.where(sees_any, 1.0 / l, 0.0)
        psum = psum + p
        o_c.append(_dot(p, vcmp))
    ph, pl_ = _split2(psum)
    ovt = ovt_ref[...]
    dnt = functools.partial(lax.dot_general, dimension_numbers=(((1,), (1,)), ((), ())),
                            preferred_element_type=F32)
    imp_t = dnt(ovt, ph) + dnt(ovt, pl_)

    wlen = WINDOW + tq
    start = pl.multiple_of(jnp.maximum(t0 - WINDOW, 0), tq)
    kw = kw_ref[pl.ds(start, wlen), :]
    vw = vw_ref[pl.ds(start, wlen), :]
    s_w = _dot_nt(q2, kw)
    kpos_w = start + lax.broadcasted_iota(jnp.int32, (tq, wlen), 1)
    t_w = t0 + lax.broadcasted_iota(jnp.int32, (tq, wlen), 0)
    bias_w = jnp.where(kpos_w <= t_w, jnp.where(kpos_w > t_w - WINDOW, 0.0, NEG_INF), NEG_INF)
    o_w = []
    for h in range(nh):
        s = head(s_w, h) + bias_w
        p = jnp.exp2(s - jnp.max(s, axis=-1, keepdims=True))
        pv = _dot(p, vw)
        o_w.append(pv[:, :HEAD_DIM] / pv[:, HEAD_DIM:])

    jj = lax.broadcasted_iota(jnp.int32, (nsp, tq), 0)
    cur = (t0 + lax.broadcasted_iota(jnp.int32, (nsp, tq), 1)) // SEL_BLOCK
    forced = (jj == 0) | (jj == cur) | (jj == cur - 1)
    valid = jj <= cur
    notsel = jnp.where(valid & forced, 0.0, 1.0)
    work = jnp.where(valid, jnp.where(forced, -BIG, imp_t), -BIG)
    jf = jj.astype(F32)
    for _ in range(n_pick - 3):
        mx = jnp.max(work, axis=0, keepdims=True)
        cand = jnp.where(work == mx, jnp.where(mx > -1e38, jf, float(nsp)), float(nsp))
        first = jnp.min(cand, axis=0, keepdims=True)
        pick = jf == first
        notsel = jnp.where(pick, 0.0, notsel)
        work = jnp.where(pick, -BIG, work)
    notsel_q = notsel.T.astype(BF16)
    q2x = jnp.concatenate([q2, jnp.concatenate([notsel_q] * nh, axis=0)], axis=1)

    m_ref[...] = jnp.full(m_ref.shape, NEG_INF, F32)
    acc_ref[...] = jnp.zeros(acc_ref.shape, F32)
    n_kt = (t0 + tq - 1) // tk + 1

    def scores(kt):
        k0 = pl.multiple_of(kt * tk, tk)
        return _dot_nt(q2x, ks_ref[pl.ds(k0, tk), :])

    def probs(kt, s, causal):
        if causal:
            kpos = kt * tk + lax.broadcasted_iota(jnp.int32, (tq, tk), 1)
            t_s = t0 + lax.broadcasted_iota(jnp.int32, (tq, tk), 0)
            bias = jnp.where(kpos <= t_s, 0.0, NEG_INF)
        ps, alphas = [], []
        for h in range(nh):
            rows = slice(h * tq, (h + 1) * tq)
            sh = head(s, h) + bias if causal else head(s, h)
            m_prev = m_ref[rows]
            m_next = jnp.maximum(m_prev, jnp.max(sh, axis=-1, keepdims=True))
            alphas.append(jnp.exp2(m_prev - m_next))
            ps.append(jnp.exp2(sh - jnp.concatenate([m_next] * (tk // LANES), axis=1)).astype(BF16))
            m_ref[rows] = m_next
        return jnp.concatenate(ps, axis=0), jnp.concatenate(alphas, axis=0)

    def accumulate(kt, p, alpha):
        k0 = pl.multiple_of(kt * tk, tk)
        pv = jnp.dot(p, vs_ref[pl.ds(k0, tk), :], preferred_element_type=F32)
        acc_ref[...] = jnp.concatenate([alpha, alpha], axis=1) * acc_ref[...] + pv

    def sweep_body(kt, s_cur):
        s_next = scores(kt + 1)
        accumulate(kt, *probs(kt, s_cur, False))
        return s_next

    s_last = lax.fori_loop(0, n_kt - 1, sweep_body, scores(0))
    accumulate(n_kt - 1, *probs(n_kt - 1, s_last, True))

    for h in range(nh):
        rows = slice(h * tq, (h + 1) * tq)
        l_s = acc_ref[rows, HEAD_DIM:]
        o_s = acc_ref[rows, :HEAD_DIM] * jnp.where(l_s > 0.0, 1.0 / l_s, 0.0)
        o_ref[:, h * HEAD_DIM:(h + 1) * HEAD_DIM] = (gate(h, 0) * o_c[h] + gate(h, 1) * o_s
                                                     + gate(h, 2) * o_w[h])


def _nsa(qn, kcmp, vcmp, ksl, vsl, kwn, vwn, sm, overlap, bsz, seq):
    tq, tk = 256, 512
    per_b = seq // tq
    ncp = kcmp.shape[1]
    nsp = overlap.shape[0]
    n_pick = min(N_SELECT, seq // SEL_BLOCK)
    assert n_pick == N_SELECT and seq % tk == 0 and seq >= WINDOW + tq and nsp == LANES
    ext = HEAD_DIM + LANES
    seq_kv = lambda w: pl.BlockSpec((seq, w), lambda b, i: (b, 0))
    cmp_kv = lambda: pl.BlockSpec((None, ncp, HEAD_DIM), lambda b, i: (b, 0, 0))
    return pl.pallas_call(
        functools.partial(_nsa_kernel, tq=tq, tk=tk, n_pick=n_pick),
        grid=(bsz, per_b),
        in_specs=[pl.BlockSpec((tq, NSA_WIDTH), lambda b, i: (b * per_b + i, 0)),
                  cmp_kv(), cmp_kv(), seq_kv(ext), seq_kv(ext), seq_kv(HEAD_DIM), seq_kv(ext),
                  pl.BlockSpec((tq, LANES), lambda b, i: (b * per_b + i, 0)),
                  pl.BlockSpec((nsp, ncp), lambda b, i: (0, 0))],
        out_specs=pl.BlockSpec((tq, NSA_WIDTH), lambda b, i: (b * per_b + i, 0)),
        out_shape=jax.ShapeDtypeStruct((bsz * seq, NSA_WIDTH), F32),
        scratch_shapes=[pltpu.VMEM((NSA_HEADS * tq, LANES), F32),
                        pltpu.VMEM((NSA_HEADS * tq, ext), F32)],
        compiler_params=_params(("arbitrary", "arbitrary")),
        name="nsa",
    )(qn, kcmp, vcmp, ksl, vsl, kwn, vwn, sm, overlap)


def _overlap_matrix(seq):
    n_cmp = (seq - CMP_BLOCK) // CMP_STRIDE + 1
    n_sel = seq // SEL_BLOCK
    ncp = seq // CMP_STRIDE
    nsp = -(-n_sel // LANES) * LANES
    cs = np.arange(n_cmp) * CMP_STRIDE
    ss = np.arange(n_sel) * SEL_BLOCK
    ov = np.minimum(cs[:, None] + CMP_BLOCK, ss[None, :] + SEL_BLOCK) - np.maximum(cs[:, None], ss[None, :])
    out = np.zeros((nsp, ncp), np.float32)
    out[:n_sel, :n_cmp] = (np.clip(ov, 0, None).astype(np.float32) / CMP_BLOCK).T
    return jnp.asarray(out, dtype=BF16)


def _ffn_kernel(x_ref, og_ref, on_ref, mod_ref, wo_ref, nw_ref, wu_ref, cw_ref, cb_ref, wd_ref, o_ref,
                up_ref, act_ref, *, tm, d_ff, cw, down_parts):
    head = SUBLANES

    @pl.when(pl.program_id(1) == 0)
    def _():
        up_ref[:, :head, :] = jnp.zeros((2, head, d_ff), F32)

    m = mod_ref[0]
    mixed = _dot(og_ref[...], wo_ref[:GDN_WIDTH, :]) + _dot(on_ref[...], wo_ref[GDN_WIDTH:, :])
    x1 = x_ref[...] + m[2:3] * mixed
    h2 = _rms_mod(x1, nw_ref[...], m[4:5], m[3:4]).astype(BF16)
    n_chunks = d_ff // cw
    y = None
    done = 0
    for c in range(n_chunks):
        cols = slice(c * cw, (c + 1) * cw)
        halves = []
        for g in range(2):
            wcols = slice(g * d_ff + c * cw, g * d_ff + (c + 1) * cw)
            up_ref[g, head:, cols] = jnp.dot(h2, wu_ref[:, wcols], preferred_element_type=F32)
            z = cb_ref[:, wcols]
            for j in range(FFN_CONV):
                tap = up_ref[g, pl.ds(head - (FFN_CONV - 1) + j, tm), cols]
                z = z + cw_ref[j:j + 1, wcols] * tap
            up_ref[g, :head, cols] = up_ref[g, tm:tm + head, cols]
            halves.append(z)
        act_ref[:, cols] = (_silu(halves[0]) * halves[1]).astype(BF16)
        if (c + 1) * down_parts // n_chunks > c * down_parts // n_chunks:
            part = slice(done * cw, (c + 1) * cw)
            d = jnp.dot(act_ref[:, part], wd_ref[part, :], preferred_element_type=F32)
            y = d if y is None else y + d
            done = c + 1
    o_ref[...] = x1 + m[5:6] * y


def _ffn(x2, o_g, o_n, mod3, w_out_b, norm2_w, w_up_b, conv_w, conv_b, w_down_b, bsz, seq):
    t, d = x2.shape
    d_ff = w_down_b.shape[0]
    tm, cw = 512, 256
    assert d_ff % cw == 0
    per_b = seq // tm
    full = lambda a: pl.BlockSpec(a.shape, lambda b, s: (0,) * a.ndim)
    conv_b2 = conv_b.reshape(1, -1)
    norm2_w2 = norm2_w.reshape(1, d)
    return pl.pallas_call(
        functools.partial(_ffn_kernel, tm=tm, d_ff=d_ff, cw=cw, down_parts=1),
        grid=(bsz, per_b),
        in_specs=[pl.BlockSpec((tm, d), lambda b, s: (b * per_b + s, 0)),
                  pl.BlockSpec((tm, GDN_WIDTH), lambda b, s: (b * per_b + s, 0)),
                  pl.BlockSpec((tm, NSA_WIDTH), lambda b, s: (b * per_b + s, 0)),
                  pl.BlockSpec((1, 6, d), lambda b, s: (b, 0, 0)),
                  full(w_out_b), full(norm2_w2), full(w_up_b), full(conv_w), full(conv_b2),
                  full(w_down_b)],
        out_specs=pl.BlockSpec((tm, d), lambda b, s: (b * per_b + s, 0)),
        out_shape=jax.ShapeDtypeStruct((t, d), F32),
        scratch_shapes=[pltpu.VMEM((2, SUBLANES + tm, d_ff), F32),
                        pltpu.VMEM((tm, d_ff), BF16)],
        compiler_params=_params(("arbitrary", "arbitrary")),
        name="ffn",
    )(x2, o_g, o_n, mod3, w_out_b, norm2_w2, w_up_b, conv_w, conv_b2, w_down_b)


def _pad_lanes(a, n):
    return jnp.pad(a, ((0, 0), (0, n - a.shape[1])))


def _layer(x, c, ada_w, ada_b, norm1_w, w_in, gdn_conv_w, gdn_A_log, gdn_dt_bias, gdn_out_norm_w,
           nsa_q_norm_w, nsa_k_norm_cmp, nsa_k_norm_slc, nsa_k_norm_win, cmp_k_pos, cmp_k_w1, cmp_k_w2,
           cmp_v_pos, cmp_v_w1, cmp_v_w2, w_out, norm2_w, ffn_w_up, ffn_conv_w, ffn_conv_b, ffn_w_down):
    bsz, seq, d = x.shape
    x2 = x.reshape(bsz * seq, d)
    mod3 = _ada(c, ada_w, ada_b).reshape(bsz, 6, d)

    gp, sm, cp, qn, ksl, vsl, kwn, vwn = _inproj(
        x2, mod3, norm1_w, w_in, gdn_conv_w, _pad_lanes(gdn_A_log.reshape(1, -1), LANES),
        _pad_lanes(gdn_dt_bias.reshape(1, -1), LANES), nsa_q_norm_w, nsa_k_norm_slc, nsa_k_norm_win, seq)

    o_g = _gdn(gp, sm, gdn_out_norm_w, bsz, seq)

    kcmp, vcmp = _compress(cp, cmp_k_pos, cmp_k_w1.astype(BF16), cmp_k_w2.astype(BF16), nsa_k_norm_cmp,
                           cmp_v_pos, cmp_v_w1.astype(BF16), cmp_v_w2.astype(BF16), bsz, seq)
    o_n = _nsa(qn, kcmp, vcmp, ksl, vsl, kwn, vwn, sm, _overlap_matrix(seq), bsz, seq)

    out = _ffn(x2, o_g, o_n, mod3, w_out.astype(BF16), norm2_w, ffn_w_up.astype(BF16), ffn_conv_w,
               ffn_conv_b, ffn_w_down.astype(BF16), bsz, seq)
    return out.reshape(bsz, seq, d)


def kernel(x, c, ada_w, ada_b, norm1_w, w_in, gdn_conv_w, gdn_A_log, gdn_dt_bias, gdn_out_norm_w, nsa_q_norm_w, nsa_k_norm_cmp, nsa_k_norm_slc, nsa_k_norm_win, cmp_k_pos, cmp_k_w1, cmp_k_w2, cmp_v_pos, cmp_v_w1, cmp_v_w2, w_out, norm2_w, ffn_w_up, ffn_conv_w, ffn_conv_b, ffn_w_down):
    for l in range(ada_w.shape[0]):
        x = _layer(x, c, ada_w[l], ada_b[l], norm1_w[l], w_in[l], gdn_conv_w[l], gdn_A_log[l],
                   gdn_dt_bias[l], gdn_out_norm_w[l], nsa_q_norm_w[l], nsa_k_norm_cmp[l],
                   nsa_k_norm_slc[l], nsa_k_norm_win[l], cmp_k_pos[l], cmp_k_w1[l], cmp_k_w2[l],
                   cmp_v_pos[l], cmp_v_w1[l], cmp_v_w2[l], w_out[l], norm2_w[l], ffn_w_up[l],
                   ffn_conv_w[l], ffn_conv_b[l], ffn_w_down[l])
    return x
```

```python
import functools

import numpy as np
import jax
import jax.numpy as jnp
from jax import lax
from jax.experimental import pallas as pl
from jax.experimental.pallas import tpu as pltpu

F32 = jnp.float32
BF16 = jnp.bfloat16

HEAD_DIM = 128
GDN_HEADS = 4
NSA_HEADS = 4
GDN_WIDTH = GDN_HEADS * HEAD_DIM
NSA_WIDTH = NSA_HEADS * HEAD_DIM
GDN_CONV = 4
GDN_CHUNK = 64
CMP_BLOCK = 32
CMP_STRIDE = 16
SEL_BLOCK = 64
N_SELECT = 16
WINDOW = 512
FFN_CONV = 3
N_GATES = 3
EPS = 1e-6
NEG_INF = -1e30
BIG = 3e38
LOG2E = 1.4426950408889634

LANES = 128
SUBLANES = 8
VMEM_LIMIT = 56 * 1024 * 1024

COL_GQ, COL_GK, COL_GV, COL_GZ = 0, 512, 1024, 1536
COL_NQ = 2048
COL_KC, COL_VC, COL_KSL, COL_VSL, COL_KWN, COL_VWN = 2560, 2688, 2816, 2944, 3072, 3200
COL_SMALL = 3328
N_PROJ = 3456
LANE_A, LANE_BETA, LANE_GATE = 0, 4, 8


def _params(sem):
    return pltpu.CompilerParams(dimension_semantics=sem, vmem_limit_bytes=VMEM_LIMIT)


def _dot(a, b):
    return jnp.dot(a.astype(BF16), b.astype(BF16), preferred_element_type=F32)


def _dot_nt(a, b):
    return lax.dot_general(a.astype(BF16), b.astype(BF16), (((1,), (1,)), ((), ())),
                           preferred_element_type=F32)


def _dot_tn(a, b):
    return lax.dot_general(a.astype(BF16), b.astype(BF16), (((0,), (0,)), ((), ())),
                           preferred_element_type=F32)


def _split2(a):
    hi = a.astype(BF16)
    lo = (a - hi.astype(F32)).astype(BF16)
    return hi, lo


def _split3(a):
    h1 = a.astype(BF16)
    r = a - h1.astype(F32)
    h2 = r.astype(BF16)
    h3 = (r - h2.astype(F32)).astype(BF16)
    return h1, h2, h3


def _dot_hp(a, b):
    ah, al = _split2(a)
    bh, bl = _split2(b)
    d = functools.partial(jnp.dot, preferred_element_type=F32)
    return d(ah, bh) + (d(ah, bl) + d(al, bh))


def _dot_exact_lhs(a_exact, b):
    a = a_exact.astype(BF16)
    b1, b2, b3 = _split3(b)
    d = functools.partial(jnp.dot, preferred_element_type=F32)
    return d(a, b1) + (d(a, b2) + d(a, b3))


def _sigmoid(x):
    return 1.0 / (1.0 + jnp.exp(-x))


def _silu(x):
    h = 0.5 * x
    return h + h * jnp.tanh(h)


def _softplus(x):
    return jnp.maximum(x, 0.0) + jnp.log1p(jnp.exp(-jnp.abs(x)))


def _ada_kernel(c_ref, w_ref, b_ref, o_ref):
    c = c_ref[...]
    o_ref[...] = _dot_hp(_silu(c), w_ref[...]) + b_ref[...]


def _ada(c, ada_w, ada_b):
    bsz, d = c.shape
    n = ada_w.shape[1]
    tn = 1024
    return pl.pallas_call(
        _ada_kernel,
        grid=(n // tn,),
        in_specs=[pl.BlockSpec((bsz, d), lambda j: (0, 0)),
                  pl.BlockSpec((d, tn), lambda j: (0, j)),
                  pl.BlockSpec((1, tn), lambda j: (0, j))],
        out_specs=pl.BlockSpec((bsz, tn), lambda j: (0, j)),
        out_shape=jax.ShapeDtypeStruct((bsz, n), F32),
        compiler_params=_params(("arbitrary",)),
        name="ada",
    )(c, ada_w, ada_b.reshape(1, n))


def _rms_mod(x, nw, scale, shift):
    y = x * lax.rsqrt(jnp.mean(x * x, axis=-1, keepdims=True) + EPS) * nw
    return y * (1.0 + scale) + shift


def _rms(x, w):
    return x * lax.rsqrt(jnp.mean(x * x, axis=-1, keepdims=True) + EPS) * w


def _inproj_kernel(x_ref, mod_ref, nw_ref, w_in_ref, cw_ref, alog_ref, dtb_ref, qw_ref, kslw_ref,
                   kwnw_ref, gp_ref, sm_ref, cp_ref, qn_ref, kext_ref, vext_ref, kwn_ref, vwext_ref,
                   w_ref, *pre_refs, tm, seq):
    head = SUBLANES
    i = pl.program_id(0)

    @pl.when(i == 0)
    def _():
        n_small = 2 * GDN_HEADS
        n_mid = COL_SMALL - COL_NQ
        n_gate = N_GATES * NSA_HEADS
        rb = 4 * LANES
        for r0 in range(0, COL_NQ, rb):
            w_ref[r0:r0 + rb, :] = w_in_ref[r0:r0 + rb, :].astype(BF16)
        for r0 in range(0, n_mid, rb):
            n = min(rb, n_mid - r0)
            w_ref[COL_NQ + r0:COL_NQ + r0 + n, :] = w_in_ref[COL_NQ + n_small + r0:COL_NQ + n_small + r0 + n, :].astype(BF16)
        tail = jnp.concatenate(
            [w_in_ref[COL_NQ:COL_NQ + n_small, :],
             w_in_ref[COL_NQ + n_small + n_mid:COL_NQ + n_small + n_mid + n_gate, :],
             jnp.zeros((LANES - n_small - n_gate, w_in_ref.shape[1]), F32)], axis=0)
        w_ref[COL_SMALL:, :] = tail.astype(BF16)

    @pl.when(i % (seq // tm) == 0)
    def _():
        for pre_ref in pre_refs:
            pre_ref[:head, :] = jnp.zeros((head, pre_ref.shape[1]), F32)

    m = mod_ref[0]
    h = _rms_mod(x_ref[...], nw_ref[...], m[1:2], m[0:1]).astype(BF16)

    def proj(c0, width):
        return lax.dot_general(h, w_ref[c0:c0 + width, :], (((1,), (1,)), ((), ())),
                               preferred_element_type=F32)

    for idx in range(3):
        cols = slice(idx * GDN_WIDTH, (idx + 1) * GDN_WIDTH)
        pre_ref = pre_refs[idx]
        pre_ref[head:, :] = proj(COL_GQ + idx * GDN_WIDTH, GDN_WIDTH)
        y = cw_ref[GDN_CONV - 1:GDN_CONV, cols] * pre_ref[head:, :]
        for j in range(GDN_CONV - 1):
            y = y + cw_ref[j:j + 1, cols] * pre_ref[pl.ds(head - (GDN_CONV - 1) + j, tm), :]
        pre_ref[:head, :] = pre_ref[tm:tm + head, :]
        y = _silu(y)
        for hh in range(GDN_HEADS):
            hl = slice(hh * HEAD_DIM, (hh + 1) * HEAD_DIM)
            yh = y[:, hl]
            if idx == 0:
                yh = yh * (lax.rsqrt(jnp.sum(yh * yh, axis=-1, keepdims=True) + EPS) * HEAD_DIM ** -0.5)
            elif idx == 1:
                yh = yh * lax.rsqrt(jnp.sum(yh * yh, axis=-1, keepdims=True) + EPS)
            gp_ref[:, idx * GDN_WIDTH + hh * HEAD_DIM:idx * GDN_WIDTH + (hh + 1) * HEAD_DIM] = yh
    gp_ref[:, COL_GZ:COL_GZ + GDN_WIDTH] = _silu(proj(COL_GZ, GDN_WIDTH))

    nq = proj(COL_NQ, NSA_WIDTH)
    for hh in range(NSA_HEADS):
        hl = slice(hh * HEAD_DIM, (hh + 1) * HEAD_DIM)
        qn_ref[:, hl] = (_rms(nq[:, hl], qw_ref[...]) * (HEAD_DIM ** -0.5 * LOG2E)).astype(BF16)
    r = proj(COL_KC, 4 * HEAD_DIM)
    cp_ref[...] = r[:, :2 * HEAD_DIM]
    ones = jnp.ones((tm, LANES), BF16)
    pos = (i * tm + lax.broadcasted_iota(jnp.int32, (tm, LANES), 0)) % seq
    lane = lax.broadcasted_iota(jnp.int32, (tm, LANES), 1)
    kext_ref[:, :HEAD_DIM] = _rms(r[:, 2 * HEAD_DIM:3 * HEAD_DIM], kslw_ref[...]).astype(BF16)
    kext_ref[:, HEAD_DIM:] = jnp.where(lane == pos // SEL_BLOCK, NEG_INF, 0.0).astype(BF16)
    vext_ref[:, :HEAD_DIM] = r[:, 3 * HEAD_DIM:].astype(BF16)
    vext_ref[:, HEAD_DIM:] = ones
    r = proj(COL_KWN, 2 * HEAD_DIM + LANES)
    kwn_ref[...] = _rms(r[:, :HEAD_DIM], kwnw_ref[...]).astype(BF16)
    vwext_ref[:, :HEAD_DIM] = r[:, HEAD_DIM:2 * HEAD_DIM].astype(BF16)
    vwext_ref[:, HEAD_DIM:] = ones
    small = r[:, 2 * HEAD_DIM:]
    log_decay = -jnp.exp(alog_ref[...]) * _softplus(small + dtb_ref[...])
    sm_ref[...] = jnp.where(lane < LANE_BETA, log_decay, _sigmoid(small))


def _inproj(x2, mod3, norm1_w, w_in, gdn_conv_w, alog_p, dtb_p, q_w, ksl_w, kwn_w, seq):
    t, d = x2.shape
    assert w_in.shape[1] == COL_SMALL + 2 * GDN_HEADS + N_GATES * NSA_HEADS
    w_in_t = w_in.T
    tm = 512
    per_b = seq // tm
    assert seq // SEL_BLOCK <= LANES
    ext = HEAD_DIM + LANES
    full = lambda a: pl.BlockSpec(a.shape, lambda i: (0,) * a.ndim)
    row = lambda w: pl.BlockSpec((tm, w), lambda i: (i, 0))
    vecs = [q_w.reshape(1, -1), ksl_w.reshape(1, -1), kwn_w.reshape(1, -1)]
    out_widths = [(4 * GDN_WIDTH, F32), (LANES, F32), (2 * HEAD_DIM, F32), (NSA_WIDTH, BF16),
                  (ext, BF16), (ext, BF16), (HEAD_DIM, BF16), (ext, BF16)]
    return pl.pallas_call(
        functools.partial(_inproj_kernel, tm=tm, seq=seq),
        grid=(t // tm,),
        in_specs=[row(d),
                  pl.BlockSpec((1, 6, d), lambda i: (i // per_b, 0, 0)),
                  pl.BlockSpec((1, d), lambda i: (0, 0)),
                  full(w_in_t), full(gdn_conv_w), full(alog_p), full(dtb_p)] + [full(v) for v in vecs],
        out_specs=[row(w) for w, _ in out_widths],
        out_shape=[jax.ShapeDtypeStruct((t, w), dt) for w, dt in out_widths],
        scratch_shapes=[pltpu.VMEM((N_PROJ, d), BF16)] + [pltpu.VMEM((SUBLANES + tm, GDN_WIDTH), F32)] * 3,
        compiler_params=_params(("arbitrary",)),
        name="inproj",
    )(x2, mod3, norm1_w.reshape(1, d), w_in_t, gdn_conv_w, alog_p, dtb_p, *vecs)


def _gdn_kernel(q_ref, k_ref, v_ref, z_ref, sm_ref, onw_ref, o_ref,
                state_ref, pq_ref, p2_ref, oi_ref, os_ref, *, tile):
    c_len = GDN_CHUNK
    n_chunks = tile // c_len

    @pl.when(pl.program_id(1) == 0)
    def _():
        state_ref[...] = jnp.zeros_like(state_ref)

    qc = q_ref[...]
    kc = k_ref[...]
    vc = v_ref[...]
    g_all = sm_ref[...]
    beta_all = g_all

    ri = lax.broadcasted_iota(jnp.int32, (tile, tile), 0)
    ci = lax.broadcasted_iota(jnp.int32, (tile, tile), 1)
    lblk = jnp.where((ri // c_len == ci // c_len) & (ci <= ri), 1.0, 0.0)
    gc_all = _dot_exact_lhs(lblk, g_all)
    gc_t = gc_all.T
    gl_all = jnp.concatenate(
        [jnp.broadcast_to(gc_all[(c + 1) * c_len - 1:(c + 1) * c_len], (c_len, LANES))
         for c in range(n_chunks)], axis=0)
    eg_all = jnp.exp(gc_all)
    ekd_all = jnp.exp(gl_all - gc_all)
    egl_all = jnp.exp(gl_all)

    r64 = lax.broadcasted_iota(jnp.int32, (c_len, c_len), 0)
    c64 = lax.broadcasted_iota(jnp.int32, (c_len, c_len), 1)
    incl = r64 >= c64
    strict = r64 > c64
    eye = jnp.where(r64 == c64, 1.0, 0.0)

    def lane_bcast(x, lane):
        return jnp.broadcast_to(x[:, lane:lane + 1], (tile, HEAD_DIM))

    units = [(c, h) for c in range(n_chunks) for h in range(GDN_HEADS)]
    heads = []
    for h in range(GDN_HEADS):
        hl = slice(h * HEAD_DIM, (h + 1) * HEAD_DIM)
        qh = qc[:, hl]
        kh = kc[:, hl]
        bb = lane_bcast(beta_all, LANE_BETA + h)
        eg = lane_bcast(eg_all, LANE_A + h)
        kb = kh * bb
        heads.append(dict(qd=qh * eg, k=kh, kb=kb, q=qh, vb=vc[:, hl] * bb, kbe=kb * eg,
                          kd=kh * lane_bcast(ekd_all, LANE_A + h),
                          egl=lane_bcast(egl_all, LANE_A + h)))

    a_mats, attns = [], []
    for c, h in units:
        rows = slice(c * c_len, (c + 1) * c_len)
        hd = heads[h]
        gcol = gc_all[rows, LANE_A + h:LANE_A + h + 1]
        grow = gc_t[LANE_A + h:LANE_A + h + 1, rows]
        decay = jnp.exp(jnp.where(incl, gcol - grow, NEG_INF))
        gram = _dot_nt(jnp.concatenate([hd["kb"][rows], hd["q"][rows]], axis=0), hd["k"][rows])
        a_mats.append(jnp.where(strict, gram[:c_len] * decay, 0.0))
        attns.append(jnp.where(incl, gram[c_len:] * decay, 0.0))
    pack = LANES // c_len
    bd_mask = (lax.broadcasted_iota(jnp.int32, (LANES, LANES), 0) // c_len
               == lax.broadcasted_iota(jnp.int32, (LANES, LANES), 1) // c_len)
    eye_cat = jnp.concatenate([eye] * pack, axis=1)

    def block_diag(x):
        zero = jnp.zeros((LANES, LANES), BF16)
        hi, lo = (jnp.where(bd_mask, jnp.concatenate([part] * pack, axis=0), zero) for part in _split2(x))
        return jnp.concatenate([hi, hi], axis=0), lo

    def hp(a, b):
        ah, al = _split2(a)
        d = functools.partial(jnp.dot, preferred_element_type=F32)
        return d(jnp.concatenate([ah, al], axis=1), b[0]) + d(ah, b[1])

    n_levels = c_len.bit_length() - 2
    n_groups = len(units) // pack
    xs = [-jnp.concatenate(a_mats[g * pack:(g + 1) * pack], axis=1) for g in range(n_groups)]
    ts = [eye_cat + x for x in xs]
    xs = [hp(x, block_diag(x)) for x in xs]
    for level in range(1, n_levels):
        rs = [hp(jnp.concatenate([x, t], axis=0), block_diag(x)) for x, t in zip(xs, ts)]
        xs = [r[:c_len] for r in rs]
        ts = [t + r[c_len:] for t, r in zip(ts, rs)]
    ts = [t + hp(t, block_diag(x)) for x, t in zip(xs, ts)]
    t_mats = [t[:, j * c_len:(j + 1) * c_len] for t in ts for j in range(pack)]
    for i, (c, h) in enumerate(units):
        rows = slice(c * c_len, (c + 1) * c_len)
        hd = heads[h]
        uw = _dot(t_mats[i], jnp.concatenate([hd["vb"][rows], hd["kbe"][rows]], axis=1))
        aw = _dot(attns[i], uw)
        kw = _dot_tn(hd["kd"][rows], uw)
        pq_ref[i, :HEAD_DIM] = kw[:, HEAD_DIM:]
        pq_ref[i, HEAD_DIM:] = hd["qd"][rows] - aw[:, HEAD_DIM:]
        p2_ref[i] = kw[:, :HEAD_DIM]
        oi_ref[i] = aw[:, :HEAD_DIM]

    for i, (c, h) in enumerate(units):
        rows = slice(c * c_len, (c + 1) * c_len)
        state = state_ref[h]
        r = _dot(pq_ref[i], state)
        os_ref[rows, h * HEAD_DIM:(h + 1) * HEAD_DIM] = r[HEAD_DIM:] + oi_ref[i]
        egl = heads[h]["egl"][c * c_len:c * c_len + 1]
        state_ref[h] = state * egl - r[:HEAD_DIM] + p2_ref[i]

    for h in range(GDN_HEADS):
        hl = slice(h * HEAD_DIM, (h + 1) * HEAD_DIM)
        o = os_ref[:, hl]
        o = o * lax.rsqrt(jnp.mean(o * o, axis=-1, keepdims=True) + EPS) * onw_ref[...]
        o_ref[:, hl] = o * z_ref[:, hl]


def _gdn(gp, sm, out_norm_w, bsz, seq):
    tile = 256
    per_b = seq // tile
    n_units = tile // GDN_CHUNK * GDN_HEADS
    col = lambda c: pl.BlockSpec((tile, GDN_WIDTH), lambda b, s: (b * per_b + s, c))
    return pl.pallas_call(
        functools.partial(_gdn_kernel, tile=tile),
        grid=(bsz, per_b),
        in_specs=[col(0), col(1), col(2), col(3),
                  pl.BlockSpec((tile, LANES), lambda b, s: (b * per_b + s, 0)),
                  pl.BlockSpec((1, HEAD_DIM), lambda b, s: (0, 0))],
        out_specs=pl.BlockSpec((tile, GDN_WIDTH), lambda b, s: (b * per_b + s, 0)),
        out_shape=jax.ShapeDtypeStruct((bsz * seq, GDN_WIDTH), F32),
        scratch_shapes=[pltpu.VMEM((GDN_HEADS, HEAD_DIM, HEAD_DIM), F32),
                        pltpu.VMEM((n_units, HEAD_DIM + GDN_CHUNK, HEAD_DIM), F32),
                        pltpu.VMEM((n_units, HEAD_DIM, HEAD_DIM), F32),
                        pltpu.VMEM((n_units, GDN_CHUNK, HEAD_DIM), F32),
                        pltpu.VMEM((tile, GDN_WIDTH), F32)],
        compiler_params=_params(("arbitrary", "arbitrary")),
        name="gdn",
    )(gp, gp, gp, gp, sm, out_norm_w.reshape(1, HEAD_DIM))


def _compress_kernel(kc_ref, vc_ref, kpos_ref, kw1_ref, kw2_ref, knw_ref, vpos_ref, vw1_ref, vw2_ref,
                     ko_ref, vo_ref, *, n_cmp):
    ncp = ko_ref.shape[0]
    half = CMP_BLOCK // 2

    def one(x_ref, pos_ref, w1_ref, w2_ref):
        p = jnp.zeros((ncp, HEAD_DIM), F32)
        q = jnp.zeros((ncp, HEAD_DIM), F32)
        for j in range(half):
            a = x_ref[pl.ds(j, ncp, stride=CMP_STRIDE), :]
            p = p + _dot(a + pos_ref[j:j + 1, :], w1_ref[j * HEAD_DIM:(j + 1) * HEAD_DIM, :])
            q = q + _dot(a + pos_ref[half + j:half + j + 1, :],
                         w1_ref[(half + j) * HEAD_DIM:(half + j + 1) * HEAD_DIM, :])
        hid = _silu(p + pltpu.roll(q, ncp - 1, axis=0))
        out = _dot(hid, w2_ref[...])
        valid = lax.broadcasted_iota(jnp.int32, (ncp, HEAD_DIM), 0) < n_cmp
        return jnp.where(valid, out, 0.0)

    k = one(kc_ref, kpos_ref, kw1_ref, kw2_ref)
    ko_ref[...] = jnp.where(lax.broadcasted_iota(jnp.int32, k.shape, 0) < n_cmp,
                            _rms(k, knw_ref[...]), 0.0).astype(BF16)
    vo_ref[...] = one(vc_ref, vpos_ref, vw1_ref, vw2_ref).astype(BF16)


def _compress(cp, kpos, kw1, kw2, knw, vpos, vw1, vw2, bsz, seq):
    ncp = seq // CMP_STRIDE
    n_cmp = (seq - CMP_BLOCK) // CMP_STRIDE + 1
    full = lambda a: pl.BlockSpec(a.shape, lambda b: (0,) * a.ndim)
    knw2 = knw.reshape(1, HEAD_DIM)
    out = pl.BlockSpec((None, ncp, HEAD_DIM), lambda b: (b, 0, 0))
    shp = jax.ShapeDtypeStruct((bsz, ncp, HEAD_DIM), BF16)
    return pl.pallas_call(
        functools.partial(_compress_kernel, n_cmp=n_cmp),
        grid=(bsz,),
        in_specs=[pl.BlockSpec((seq, HEAD_DIM), lambda b: (b, 0)),
                  pl.BlockSpec((seq, HEAD_DIM), lambda b: (b, 1)),
                  full(kpos), full(kw1), full(kw2), full(knw2), full(vpos), full(vw1), full(vw2)],
        out_specs=[out, out],
        out_shape=[shp, shp],
        compiler_params=_params(("arbitrary",)),
        name="compress",
    )(cp, cp, kpos, kw1, kw2, knw2, vpos, vw1, vw2)


def _nsa_kernel(q_ref, kc_ref, vc_ref, ks_ref, vs_ref, kw_ref, vw_ref, sm_ref, ovt_ref, o_ref,
                m_ref, acc_ref, *, tq, tk, n_pick):
    qi = pl.program_id(1)
    t0 = qi * tq
    nh = NSA_HEADS
    ncp = kc_ref.shape[0]
    nsp = ovt_ref.shape[0]
    q = q_ref[...]
    q2 = jnp.concatenate([q[:, h * HEAD_DIM:(h + 1) * HEAD_DIM] for h in range(nh)], axis=0)
    gates = sm_ref[...]

    def gate(h, g):
        c = LANE_GATE + N_GATES * h + g
        return gates[:, c:c + 1]

    def head(x, h):
        return x[h * tq:(h + 1) * tq]

    t_c = t0 + lax.broadcasted_iota(jnp.int32, (tq, ncp), 0)
    cend = lax.broadcasted_iota(jnp.int32, (tq, ncp), 1) * CMP_STRIDE + (CMP_BLOCK - 1)
    bias_c = jnp.where(cend <= t_c, 0.0, NEG_INF)
    sees_any = (t0 + lax.broadcasted_iota(jnp.int32, (tq, 1), 0)) >= CMP_BLOCK - 1
    s_c = _dot_nt(q2, kc_ref[...])
    vcmp = vc_ref[...]
    o_c = []
    psum = jnp.zeros((tq, ncp), F32)
    for h in range(nh):
        s = head(s_c, h) + bias_c
        p = jnp.exp2(s - jnp.max(s, axis=-1, keepdims=True))
        l = jnp.sum(p, axis=-1, keepdims=True)
        p = p * jnp.where(sees_any, 1.0 / l, 0.0)
        psum = psum + p
        o_c.append(_dot(p, vcmp))
    ph, pl_ = _split2(psum)
    ovt = ovt_ref[...]
    dnt = functools.partial(lax.dot_general, dimension_numbers=(((1,), (1,)), ((), ())),
                            preferred_element_type=F32)
    imp_t = dnt(ovt, ph) + dnt(ovt, pl_)

    wlen = WINDOW + tq
    start = pl.multiple_of(jnp.maximum(t0 - WINDOW, 0), tq)
    kw = kw_ref[pl.ds(start, wlen), :]
    vw = vw_ref[pl.ds(start, wlen), :]
    s_w = _dot_nt(q2, kw)
    kpos_w = start + lax.broadcasted_iota(jnp.int32, (tq, wlen), 1)
    t_w = t0 + lax.broadcasted_iota(jnp.int32, (tq, wlen), 0)
    bias_w = jnp.where(kpos_w <= t_w, jnp.where(kpos_w > t_w - WINDOW, 0.0, NEG_INF), NEG_INF)
    o_w = []
    for h in range(nh):
        s = head(s_w, h) + bias_w
        p = jnp.exp2(s - jnp.max(s, axis=-1, keepdims=True))
        pv = _dot(p, vw)
        o_w.append(pv[:, :HEAD_DIM] / pv[:, HEAD_DIM:])

    jj = lax.broadcasted_iota(jnp.int32, (nsp, tq), 0)
    cur = (t0 + lax.broadcasted_iota(jnp.int32, (nsp, tq), 1)) // SEL_BLOCK
    forced = (jj == 0) | (jj == cur) | (jj == cur - 1)
    valid = jj <= cur
    notsel = jnp.where(valid & forced, 0.0, 1.0)
    work = jnp.where(valid, jnp.where(forced, -BIG, imp_t), -BIG)
    jf = jj.astype(F32)
    for _ in range(n_pick - 3):
        mx = jnp.max(work, axis=0, keepdims=True)
        cand = jnp.where(work == mx, jnp.where(mx > -1e38, jf, float(nsp)), float(nsp))
        first = jnp.min(cand, axis=0, keepdims=True)
        pick = jf == first
        notsel = jnp.where(pick, 0.0, notsel)
        work = jnp.where(pick, -BIG, work)
    notsel_q = notsel.T.astype(BF16)
    q2x = jnp.concatenate([q2, jnp.concatenate([notsel_q] * nh, axis=0)], axis=1)

    m_ref[...] = jnp.full(m_ref.shape, NEG_INF, F32)
    acc_ref[...] = jnp.zeros(acc_ref.shape, F32)
    n_kt = (t0 + tq - 1) // tk + 1

    def scores(kt):
        k0 = pl.multiple_of(kt * tk, tk)
        return _dot_nt(q2x, ks_ref[pl.ds(k0, tk), :])

    def probs(kt, s, causal):
        if causal:
            kpos = kt * tk + lax.broadcasted_iota(jnp.int32, (tq, tk), 1)
            t_s = t0 + lax.broadcasted_iota(jnp.int32, (tq, tk), 0)
            bias = jnp.where(kpos <= t_s, 0.0, NEG_INF)
        ps, alphas = [], []
        for h in range(nh):
            rows = slice(h * tq, (h + 1) * tq)
            sh = head(s, h) + bias if causal else head(s, h)
            m_prev = m_ref[rows]
            m_next = jnp.maximum(m_prev, jnp.max(sh, axis=-1, keepdims=True))
            alphas.append(jnp.exp2(m_prev - m_next))
            ps.append(jnp.exp2(sh - jnp.concatenate([m_next] * (tk // LANES), axis=1)).astype(BF16))
            m_ref[rows] = m_next
        return jnp.concatenate(ps, axis=0), jnp.concatenate(alphas, axis=0)

    def accumulate(kt, p, alpha):
        k0 = pl.multiple_of(kt * tk, tk)
        pv = jnp.dot(p, vs_ref[pl.ds(k0, tk), :], preferred_element_type=F32)
        acc_ref[...] = jnp.concatenate([alpha, alpha], axis=1) * acc_ref[...] + pv

    def sweep_body(kt, s_cur):
        s_next = scores(kt + 1)
        accumulate(kt, *probs(kt, s_cur, False))
        return s_next

    s_last = lax.fori_loop(0, n_kt - 1, sweep_body, scores(0))
    accumulate(n_kt - 1, *probs(n_kt - 1, s_last, True))

    for h in range(nh):
        rows = slice(h * tq, (h + 1) * tq)
        l_s = acc_ref[rows, HEAD_DIM:]
        o_s = acc_ref[rows, :HEAD_DIM] * jnp.where(l_s > 0.0, 1.0 / l_s, 0.0)
        o_ref[:, h * HEAD_DIM:(h + 1) * HEAD_DIM] = (gate(h, 0) * o_c[h] + gate(h, 1) * o_s
                                                     + gate(h, 2) * o_w[h])


def _nsa(qn, kcmp, vcmp, ksl, vsl, kwn, vwn, sm, overlap, bsz, seq):
    tq, tk = 256, 512
    per_b = seq // tq
    ncp = kcmp.shape[1]
    nsp = overlap.shape[0]
    n_pick = min(N_SELECT, seq // SEL_BLOCK)
    assert n_pick == N_SELECT and seq % tk == 0 and seq >= WINDOW + tq and nsp == LANES
    ext = HEAD_DIM + LANES
    seq_kv = lambda w: pl.BlockSpec((seq, w), lambda b, i: (b, 0))
    cmp_kv = lambda: pl.BlockSpec((None, ncp, HEAD_DIM), lambda b, i: (b, 0, 0))
    return pl.pallas_call(
        functools.partial(_nsa_kernel, tq=tq, tk=tk, n_pick=n_pick),
        grid=(bsz, per_b),
        in_specs=[pl.BlockSpec((tq, NSA_WIDTH), lambda b, i: (b * per_b + i, 0)),
                  cmp_kv(), cmp_kv(), seq_kv(ext), seq_kv(ext), seq_kv(HEAD_DIM), seq_kv(ext),
                  pl.BlockSpec((tq, LANES), lambda b, i: (b * per_b + i, 0)),
                  pl.BlockSpec((nsp, ncp), lambda b, i: (0, 0))],
        out_specs=pl.BlockSpec((tq, NSA_WIDTH), lambda b, i: (b * per_b + i, 0)),
        out_shape=jax.ShapeDtypeStruct((bsz * seq, NSA_WIDTH), F32),
        scratch_shapes=[pltpu.VMEM((NSA_HEADS * tq, LANES), F32),
                        pltpu.VMEM((NSA_HEADS * tq, ext), F32)],
        compiler_params=_params(("arbitrary", "arbitrary")),
        name="nsa",
    )(qn, kcmp, vcmp, ksl, vsl, kwn, vwn, sm, overlap)


def _overlap_matrix(seq):
    n_cmp = (seq - CMP_BLOCK) // CMP_STRIDE + 1
    n_sel = seq // SEL_BLOCK
    ncp = seq // CMP_STRIDE
    nsp = -(-n_sel // LANES) * LANES
    cs = np.arange(n_cmp) * CMP_STRIDE
    ss = np.arange(n_sel) * SEL_BLOCK
    ov = np.minimum(cs[:, None] + CMP_BLOCK, ss[None, :] + SEL_BLOCK) - np.maximum(cs[:, None], ss[None, :])
    out = np.zeros((nsp, ncp), np.float32)
    out[:n_sel, :n_cmp] = (np.clip(ov, 0, None).astype(np.float32) / CMP_BLOCK).T
    return jnp.asarray(out, dtype=BF16)


def _ffn_kernel(x_ref, og_ref, on_ref, mod_ref, wo_ref, nw_ref, wu_ref, cw_ref, cb_ref, wd_ref, o_ref,
                up_ref, act_ref, *, tm, d_ff, cw, down_parts):
    head = SUBLANES

    @pl.when(pl.program_id(1) == 0)
    def _():
        up_ref[:, :head, :] = jnp.zeros((2, head, d_ff), F32)

    m = mod_ref[0]
    mixed = _dot(og_ref[...], wo_ref[:GDN_WIDTH, :]) + _dot(on_ref[...], wo_ref[GDN_WIDTH:, :])
    x1 = x_ref[...] + m[2:3] * mixed
    h2 = _rms_mod(x1, nw_ref[...], m[4:5], m[3:4]).astype(BF16)
    n_chunks = d_ff // cw
    y = None
    done = 0
    for c in range(n_chunks):
        cols = slice(c * cw, (c + 1) * cw)
        halves = []
        for g in range(2):
            wcols = slice(g * d_ff + c * cw, g * d_ff + (c + 1) * cw)
            up_ref[g, head:, cols] = jnp.dot(h2, wu_ref[:, wcols], preferred_element_type=F32)
            z = cb_ref[:, wcols]
            for j in range(FFN_CONV):
                tap = up_ref[g, pl.ds(head - (FFN_CONV - 1) + j, tm), cols]
                z = z + cw_ref[j:j + 1, wcols] * tap
            up_ref[g, :head, cols] = up_ref[g, tm:tm + head, cols]
            halves.append(z)
        act_ref[:, cols] = (_silu(halves[0]) * halves[1]).astype(BF16)
        if (c + 1) * down_parts // n_chunks > c * down_parts // n_chunks:
            part = slice(done * cw, (c + 1) * cw)
            d = jnp.dot(act_ref[:, part], wd_ref[part, :], preferred_element_type=F32)
            y = d if y is None else y + d
            done = c + 1
    o_ref[...] = x1 + m[5:6] * y


def _ffn(x2, o_g, o_n, mod3, w_out_b, norm2_w, w_up_b, conv_w, conv_b, w_down_b, bsz, seq):
    t, d = x2.shape
    d_ff = w_down_b.shape[0]
    tm, cw = 512, 256
    assert d_ff % cw == 0
    per_b = seq // tm
    full = lambda a: pl.BlockSpec(a.shape, lambda b, s: (0,) * a.ndim)
    conv_b2 = conv_b.reshape(1, -1)
    norm2_w2 = norm2_w.reshape(1, d)
    return pl.pallas_call(
        functools.partial(_ffn_kernel, tm=tm, d_ff=d_ff, cw=cw, down_parts=1),
        grid=(bsz, per_b),
        in_specs=[pl.BlockSpec((tm, d), lambda b, s: (b * per_b + s, 0)),
                  pl.BlockSpec((tm, GDN_WIDTH), lambda b, s: (b * per_b + s, 0)),
                  pl.BlockSpec((tm, NSA_WIDTH), lambda b, s: (b * per_b + s, 0)),
                  pl.BlockSpec((1, 6, d), lambda b, s: (b, 0, 0)),
                  full(w_out_b), full(norm2_w2), full(w_up_b), full(conv_w), full(conv_b2),
                  full(w_down_b)],
        out_specs=pl.BlockSpec((tm, d), lambda b, s: (b * per_b + s, 0)),
        out_shape=jax.ShapeDtypeStruct((t, d), F32),
        scratch_shapes=[pltpu.VMEM((2, SUBLANES + tm, d_ff), F32),
                        pltpu.VMEM((tm, d_ff), BF16)],
        compiler_params=_params(("arbitrary", "arbitrary")),
        name="ffn",
    )(x2, o_g, o_n, mod3, w_out_b, norm2_w2, w_up_b, conv_w, conv_b2, w_down_b)


def _pad_lanes(a, n):
    return jnp.pad(a, ((0, 0), (0, n - a.shape[1])))


def _layer(x, c, ada_w, ada_b, norm1_w, w_in, gdn_conv_w, gdn_A_log, gdn_dt_bias, gdn_out_norm_w,
           nsa_q_norm_w, nsa_k_norm_cmp, nsa_k_norm_slc, nsa_k_norm_win, cmp_k_pos, cmp_k_w1, cmp_k_w2,
           cmp_v_pos, cmp_v_w1, cmp_v_w2, w_out, norm2_w, ffn_w_up, ffn_conv_w, ffn_conv_b, ffn_w_down):
    bsz, seq, d = x.shape
    x2 = x.reshape(bsz * seq, d)
    mod3 = _ada(c, ada_w, ada_b).reshape(bsz, 6, d)

    gp, sm, cp, qn, ksl, vsl, kwn, vwn = _inproj(
        x2, mod3, norm1_w, w_in, gdn_conv_w, _pad_lanes(gdn_A_log.reshape(1, -1), LANES),
        _pad_lanes(gdn_dt_bias.reshape(1, -1), LANES), nsa_q_norm_w, nsa_k_norm_slc, nsa_k_norm_win, seq)

    o_g = _gdn(gp, sm, gdn_out_norm_w, bsz, seq)

    kcmp, vcmp = _compress(cp, cmp_k_pos, cmp_k_w1.astype(BF16), cmp_k_w2.astype(BF16), nsa_k_norm_cmp,
                           cmp_v_pos, cmp_v_w1.astype(BF16), cmp_v_w2.astype(BF16), bsz, seq)
    o_n = _nsa(qn, kcmp, vcmp, ksl, vsl, kwn, vwn, sm, _overlap_matrix(seq), bsz, seq)

    out = _ffn(x2, o_g, o_n, mod3, w_out.astype(BF16), norm2_w, ffn_w_up.astype(BF16), ffn_conv_w,
               ffn_conv_b, ffn_w_down.astype(BF16), bsz, seq)
    return out.reshape(bsz, seq, d)


def kernel(x, c, ada_w, ada_b, norm1_w, w_in, gdn_conv_w, gdn_A_log, gdn_dt_bias, gdn_out_norm_w, nsa_q_norm_w, nsa_k_norm_cmp, nsa_k_norm_slc, nsa_k_norm_win, cmp_k_pos, cmp_k_w1, cmp_k_w2, cmp_v_pos, cmp_v_w1, cmp_v_w2, w_out, norm2_w, ffn_w_up, ffn_conv_w, ffn_conv_b, ffn_w_down):
    for l in range(ada_w.shape[0]):
        x = _layer(x, c, ada_w[l], ada_b[l], norm1_w[l], w_in[l], gdn_conv_w[l], gdn_A_log[l],
                   gdn_dt_bias[l], gdn_out_norm_w[l], nsa_q_norm_w[l], nsa_k_norm_cmp[l],
                   nsa_k_norm_slc[l], nsa_k_norm_win[l], cmp_k_pos[l], cmp_k_w1[l], cmp_k_w2[l],
                   cmp_v_pos[l], cmp_v_w1[l], cmp_v_w2[l], w_out[l], norm2_w[l], ffn_w_up[l],
                   ffn_conv_w[l], ffn_conv_b[l], ffn_w_down[l])
    return x
```

```python
import functools

import numpy as np
import jax
import jax.numpy as jnp
from jax import lax
from jax.experimental import pallas as pl
from jax.experimental.pallas import tpu as pltpu

F32 = jnp.float32
BF16 = jnp.bfloat16

HEAD_DIM = 128
GDN_HEADS = 4
NSA_HEADS = 4
GDN_WIDTH = GDN_HEADS * HEAD_DIM
NSA_WIDTH = NSA_HEADS * HEAD_DIM
GDN_CONV = 4
GDN_CHUNK = 64
CMP_BLOCK = 32
CMP_STRIDE = 16
SEL_BLOCK = 64
N_SELECT = 16
WINDOW = 512
FFN_CONV = 3
N_GATES = 3
EPS = 1e-6
NEG_INF = -1e30
BIG = 3e38
LOG2E = 1.4426950408889634

LANES = 128
SUBLANES = 8
VMEM_LIMIT = 56 * 1024 * 1024

COL_GQ, COL_GK, COL_GV, COL_GZ = 0, 512, 1024, 1536
COL_NQ = 2048
COL_KC, COL_VC, COL_KSL, COL_VSL, COL_KWN, COL_VWN = 2560, 2688, 2816, 2944, 3072, 3200
COL_SMALL = 3328
N_PROJ = 3456
LANE_A, LANE_BETA, LANE_GATE = 0, 4, 8


def _params(sem):
    return pltpu.CompilerParams(dimension_semantics=sem, vmem_limit_bytes=VMEM_LIMIT)


def _dot(a, b):
    return jnp.dot(a.astype(BF16), b.astype(BF16), preferred_element_type=F32)


def _dot_nt(a, b):
    return lax.dot_general(a.astype(BF16), b.astype(BF16), (((1,), (1,)), ((), ())),
                           preferred_element_type=F32)


def _dot_tn(a, b):
    return lax.dot_general(a.astype(BF16), b.astype(BF16), (((0,), (0,)), ((), ())),
                           preferred_element_type=F32)


def _split2(a):
    hi = a.astype(BF16)
    lo = (a - hi.astype(F32)).astype(BF16)
    return hi, lo


def _split3(a):
    h1 = a.astype(BF16)
    r = a - h1.astype(F32)
    h2 = r.astype(BF16)
    h3 = (r - h2.astype(F32)).astype(BF16)
    return h1, h2, h3


def _dot_hp(a, b):
    ah, al = _split2(a)
    bh, bl = _split2(b)
    d = functools.partial(jnp.dot, preferred_element_type=F32)
    return d(ah, bh) + (d(ah, bl) + d(al, bh))


def _dot_exact_lhs(a_exact, b):
    a = a_exact.astype(BF16)
    b1, b2, b3 = _split3(b)
    d = functools.partial(jnp.dot, preferred_element_type=F32)
    return d(a, b1) + (d(a, b2) + d(a, b3))


def _sigmoid(x):
    return 1.0 / (1.0 + jnp.exp(-x))


def _silu(x):
    h = 0.5 * x
    return h + h * jnp.tanh(h)


def _softplus(x):
    return jnp.maximum(x, 0.0) + jnp.log1p(jnp.exp(-jnp.abs(x)))


def _ada_kernel(c_ref, w_ref, b_ref, o_ref):
    c = c_ref[...]
    o_ref[...] = _dot_hp(_silu(c), w_ref[...]) + b_ref[...]


def _ada(c, ada_w, ada_b):
    bsz, d = c.shape
    n = ada_w.shape[1]
    tn = 1024
    return pl.pallas_call(
        _ada_kernel,
        grid=(n // tn,),
        in_specs=[pl.BlockSpec((bsz, d), lambda j: (0, 0)),
                  pl.BlockSpec((d, tn), lambda j: (0, j)),
                  pl.BlockSpec((1, tn), lambda j: (0, j))],
        out_specs=pl.BlockSpec((bsz, tn), lambda j: (0, j)),
        out_shape=jax.ShapeDtypeStruct((bsz, n), F32),
        compiler_params=_params(("arbitrary",)),
        name="ada",
    )(c, ada_w, ada_b.reshape(1, n))


def _rms_mod(x, nw, scale, shift):
    y = x * lax.rsqrt(jnp.mean(x * x, axis=-1, keepdims=True) + EPS) * nw
    return y * (1.0 + scale) + shift


def _rms(x, w):
    return x * lax.rsqrt(jnp.mean(x * x, axis=-1, keepdims=True) + EPS) * w


def _inproj_kernel(x_ref, mod_ref, nw_ref, w_in_ref, cw_ref, alog_ref, dtb_ref, qw_ref, kslw_ref,
                   kwnw_ref, gp_ref, sm_ref, cp_ref, qn_ref, kext_ref, vext_ref, kwn_ref, vwext_ref,
                   w_ref, *pre_refs, tm, seq):
    head = SUBLANES
    i = pl.program_id(0)

    @pl.when(i == 0)
    def _():
        n_small = 2 * GDN_HEADS
        n_mid = COL_SMALL - COL_NQ
        n_gate = N_GATES * NSA_HEADS
        rb = 4 * LANES
        for r0 in range(0, COL_NQ, rb):
            w_ref[r0:r0 + rb, :] = w_in_ref[r0:r0 + rb, :].astype(BF16)
        for r0 in range(0, n_mid, rb):
            n = min(rb, n_mid - r0)
            w_ref[COL_NQ + r0:COL_NQ + r0 + n, :] = w_in_ref[COL_NQ + n_small + r0:COL_NQ + n_small + r0 + n, :].astype(BF16)
        tail = jnp.concatenate(
            [w_in_ref[COL_NQ:COL_NQ + n_small, :],
             w_in_ref[COL_NQ + n_small + n_mid:COL_NQ + n_small + n_mid + n_gate, :],
             jnp.zeros((LANES - n_small - n_gate, w_in_ref.shape[1]), F32)], axis=0)
        w_ref[COL_SMALL:, :] = tail.astype(BF16)

    @pl.when(i % (seq // tm) == 0)
    def _():
        for pre_ref in pre_refs:
            pre_ref[:head, :] = jnp.zeros((head, pre_ref.shape[1]), F32)

    m = mod_ref[0]
    h = _rms_mod(x_ref[...], nw_ref[...], m[1:2], m[0:1]).astype(BF16)

    def proj(c0, width):
        return lax.dot_general(h, w_ref[c0:c0 + width, :], (((1,), (1,)), ((), ())),
                               preferred_element_type=F32)

    for idx in range(3):
        cols = slice(idx * GDN_WIDTH, (idx + 1) * GDN_WIDTH)
        pre_ref = pre_refs[idx]
        pre_ref[head:, :] = proj(COL_GQ + idx * GDN_WIDTH, GDN_WIDTH)
        y = cw_ref[GDN_CONV - 1:GDN_CONV, cols] * pre_ref[head:, :]
        for j in range(GDN_CONV - 1):
            y = y + cw_ref[j:j + 1, cols] * pre_ref[pl.ds(head - (GDN_CONV - 1) + j, tm), :]
        pre_ref[:head, :] = pre_ref[tm:tm + head, :]
        y = _silu(y)
        for hh in range(GDN_HEADS):
            hl = slice(hh * HEAD_DIM, (hh + 1) * HEAD_DIM)
            yh = y[:, hl]
            if idx == 0:
                yh = yh * (lax.rsqrt(jnp.sum(yh * yh, axis=-1, keepdims=True) + EPS) * HEAD_DIM ** -0.5)
            elif idx == 1:
                yh = yh * lax.rsqrt(jnp.sum(yh * yh, axis=-1, keepdims=True) + EPS)
            gp_ref[:, idx * GDN_WIDTH + hh * HEAD_DIM:idx * GDN_WIDTH + (hh + 1) * HEAD_DIM] = yh
    gp_ref[:, COL_GZ:COL_GZ + GDN_WIDTH] = _silu(proj(COL_GZ, GDN_WIDTH))

    nq = proj(COL_NQ, NSA_WIDTH)
    for hh in range(NSA_HEADS):
        hl = slice(hh * HEAD_DIM, (hh + 1) * HEAD_DIM)
        qn_ref[:, hl] = (_rms(nq[:, hl], qw_ref[...]) * (HEAD_DIM ** -0.5 * LOG2E)).astype(BF16)
    r = proj(COL_KC, 4 * HEAD_DIM)
    cp_ref[...] = r[:, :2 * HEAD_DIM]
    ones = jnp.ones((tm, LANES), BF16)
    pos = (i * tm + lax.broadcasted_iota(jnp.int32, (tm, LANES), 0)) % seq
    lane = lax.broadcasted_iota(jnp.int32, (tm, LANES), 1)
    kext_ref[:, :HEAD_DIM] = _rms(r[:, 2 * HEAD_DIM:3 * HEAD_DIM], kslw_ref[...]).astype(BF16)
    kext_ref[:, HEAD_DIM:] = jnp.where(lane == pos // SEL_BLOCK, NEG_INF, 0.0).astype(BF16)
    vext_ref[:, :HEAD_DIM] = r[:, 3 * HEAD_DIM:].astype(BF16)
    vext_ref[:, HEAD_DIM:] = ones
    r = proj(COL_KWN, 2 * HEAD_DIM + LANES)
    kwn_ref[...] = _rms(r[:, :HEAD_DIM], kwnw_ref[...]).astype(BF16)
    vwext_ref[:, :HEAD_DIM] = r[:, HEAD_DIM:2 * HEAD_DIM].astype(BF16)
    vwext_ref[:, HEAD_DIM:] = ones
    small = r[:, 2 * HEAD_DIM:]
    log_decay = -jnp.exp(alog_ref[...]) * _softplus(small + dtb_ref[...])
    sm_ref[...] = jnp.where(lane < LANE_BETA, log_decay, _sigmoid(small))


def _inproj(x2, mod3, norm1_w, w_in, gdn_conv_w, alog_p, dtb_p, q_w, ksl_w, kwn_w, seq):
    t, d = x2.shape
    assert w_in.shape[1] == COL_SMALL + 2 * GDN_HEADS + N_GATES * NSA_HEADS
    w_in_t = w_in.T
    tm = 512
    per_b = seq // tm
    assert seq // SEL_BLOCK <= LANES
    ext = HEAD_DIM + LANES
    full = lambda a: pl.BlockSpec(a.shape, lambda i: (0,) * a.ndim)
    row = lambda w: pl.BlockSpec((tm, w), lambda i: (i, 0))
    vecs = [q_w.reshape(1, -1), ksl_w.reshape(1, -1), kwn_w.reshape(1, -1)]
    out_widths = [(4 * GDN_WIDTH, F32), (LANES, F32), (2 * HEAD_DIM, F32), (NSA_WIDTH, BF16),
                  (ext, BF16), (ext, BF16), (HEAD_DIM, BF16), (ext, BF16)]
    return pl.pallas_call(
        functools.partial(_inproj_kernel, tm=tm, seq=seq),
        grid=(t // tm,),
        in_specs=[row(d),
                  pl.BlockSpec((1, 6, d), lambda i: (i // per_b, 0, 0)),
                  pl.BlockSpec((1, d), lambda i: (0, 0)),
                  full(w_in_t), full(gdn_conv_w), full(alog_p), full(dtb_p)] + [full(v) for v in vecs],
        out_specs=[row(w) for w, _ in out_widths],
        out_shape=[jax.ShapeDtypeStruct((t, w), dt) for w, dt in out_widths],
        scratch_shapes=[pltpu.VMEM((N_PROJ, d), BF16)] + [pltpu.VMEM((SUBLANES + tm, GDN_WIDTH), F32)] * 3,
        compiler_params=_params(("arbitrary",)),
        name="inproj",
    )(x2, mod3, norm1_w.reshape(1, d), w_in_t, gdn_conv_w, alog_p, dtb_p, *vecs)


def _gdn_kernel(q_ref, k_ref, v_ref, z_ref, sm_ref, onw_ref, o_ref,
                state_ref, pq_ref, p2_ref, oi_ref, os_ref, *, tile):
    c_len = GDN_CHUNK
    n_chunks = tile // c_len

    @pl.when(pl.program_id(1) == 0)
    def _():
        state_ref[...] = jnp.zeros_like(state_ref)

    qc = q_ref[...]
    kc = k_ref[...]
    vc = v_ref[...]
    g_all = sm_ref[...]
    beta_all = g_all

    ri = lax.broadcasted_iota(jnp.int32, (tile, tile), 0)
    ci = lax.broadcasted_iota(jnp.int32, (tile, tile), 1)
    lblk = jnp.where((ri // c_len == ci // c_len) & (ci <= ri), 1.0, 0.0)
    gc_all = _dot_exact_lhs(lblk, g_all)
    gc_t = gc_all.T
    gl_all = jnp.concatenate(
        [jnp.broadcast_to(gc_all[(c + 1) * c_len - 1:(c + 1) * c_len], (c_len, LANES))
         for c in range(n_chunks)], axis=0)
    eg_all = jnp.exp(gc_all)
    ekd_all = jnp.exp(gl_all - gc_all)
    egl_all = jnp.exp(gl_all)

    r64 = lax.broadcasted_iota(jnp.int32, (c_len, c_len), 0)
    c64 = lax.broadcasted_iota(jnp.int32, (c_len, c_len), 1)
    incl = r64 >= c64
    strict = r64 > c64
    eye = jnp.where(r64 == c64, 1.0, 0.0)

    def lane_bcast(x, lane):
        return jnp.broadcast_to(x[:, lane:lane + 1], (tile, HEAD_DIM))

    units = [(c, h) for c in range(n_chunks) for h in range(GDN_HEADS)]
    heads = []
    for h in range(GDN_HEADS):
        hl = slice(h * HEAD_DIM, (h + 1) * HEAD_DIM)
        qh = qc[:, hl]
        kh = kc[:, hl]
        bb = lane_bcast(beta_all, LANE_BETA + h)
        eg = lane_bcast(eg_all, LANE_A + h)
        kb = kh * bb
        heads.append(dict(qd=qh * eg, k=kh, kb=kb, q=qh, vb=vc[:, hl] * bb, kbe=kb * eg,
                          kd=kh * lane_bcast(ekd_all, LANE_A + h),
                          egl=lane_bcast(egl_all, LANE_A + h)))

    a_mats, attns = [], []
    for c, h in units:
        rows = slice(c * c_len, (c + 1) * c_len)
        hd = heads[h]
        gcol = gc_all[rows, LANE_A + h:LANE_A + h + 1]
        grow = gc_t[LANE_A + h:LANE_A + h + 1, rows]
        decay = jnp.exp(jnp.where(incl, gcol - grow, NEG_INF))
        gram = _dot_nt(jnp.concatenate([hd["kb"][rows], hd["q"][rows]], axis=0), hd["k"][rows])
        a_mats.append(jnp.where(strict, gram[:c_len] * decay, 0.0))
        attns.append(jnp.where(incl, gram[c_len:] * decay, 0.0))
    pack = LANES // c_len
    bd_mask = (lax.broadcasted_iota(jnp.int32, (LANES, LANES), 0) // c_len
               == lax.broadcasted_iota(jnp.int32, (LANES, LANES), 1) // c_len)
    eye_cat = jnp.concatenate([eye] * pack, axis=1)

    def block_diag(x):
        zero = jnp.zeros((LANES, LANES), BF16)
        hi, lo = (jnp.where(bd_mask, jnp.concatenate([part] * pack, axis=0), zero) for part in _split2(x))
        return jnp.concatenate([hi, hi], axis=0), lo

    def hp(a, b):
        ah, al = _split2(a)
        d = functools.partial(jnp.dot, preferred_element_type=F32)
        return d(jnp.concatenate([ah, al], axis=1), b[0]) + d(ah, b[1])

    n_levels = c_len.bit_length() - 2
    n_groups = len(units) // pack
    xs = [-jnp.concatenate(a_mats[g * pack:(g + 1) * pack], axis=1) for g in range(n_groups)]
    ts = [eye_cat + x for x in xs]
    xs = [hp(x, block_diag(x)) for x in xs]
    for level in range(1, n_levels):
        rs = [hp(jnp.concatenate([x, t], axis=0), block_diag(x)) for x, t in zip(xs, ts)]
        xs = [r[:c_len] for r in rs]
        ts = [t + r[c_len:] for t, r in zip(ts, rs)]
    ts = [t + hp(t, block_diag(x)) for x, t in zip(xs, ts)]
    t_mats = [t[:, j * c_len:(j + 1) * c_len] for t in ts for j in range(pack)]
    for i, (c, h) in enumerate(units):
        rows = slice(c * c_len, (c + 1) * c_len)
        hd = heads[h]
        uw = _dot(t_mats[i], jnp.concatenate([hd["vb"][rows], hd["kbe"][rows]], axis=1))
        aw = _dot(attns[i], uw)
        kw = _dot_tn(hd["kd"][rows], uw)
        pq_ref[i, :HEAD_DIM] = kw[:, HEAD_DIM:]
        pq_ref[i, HEAD_DIM:] = hd["qd"][rows] - aw[:, HEAD_DIM:]
        p2_ref[i] = kw[:, :HEAD_DIM]
        oi_ref[i] = aw[:, :HEAD_DIM]

    for i, (c, h) in enumerate(units):
        rows = slice(c * c_len, (c + 1) * c_len)
        state = state_ref[h]
        r = _dot(pq_ref[i], state)
        os_ref[rows, h * HEAD_DIM:(h + 1) * HEAD_DIM] = r[HEAD_DIM:] + oi_ref[i]
        egl = heads[h]["egl"][c * c_len:c * c_len + 1]
        state_ref[h] = state * egl - r[:HEAD_DIM] + p2_ref[i]

    for h in range(GDN_HEADS):
        hl = slice(h * HEAD_DIM, (h + 1) * HEAD_DIM)
        o = os_ref[:, hl]
        o = o * lax.rsqrt(jnp.mean(o * o, axis=-1, keepdims=True) + EPS) * onw_ref[...]
        o_ref[:, hl] = o * z_ref[:, hl]


def _gdn(gp, sm, out_norm_w, bsz, seq):
    tile = 256
    per_b = seq // tile
    n_units = tile // GDN_CHUNK * GDN_HEADS
    col = lambda c: pl.BlockSpec((tile, GDN_WIDTH), lambda b, s: (b * per_b + s, c))
    return pl.pallas_call(
        functools.partial(_gdn_kernel, tile=tile),
        grid=(bsz, per_b),
        in_specs=[col(0), col(1), col(2), col(3),
                  pl.BlockSpec((tile, LANES), lambda b, s: (b * per_b + s, 0)),
                  pl.BlockSpec((1, HEAD_DIM), lambda b, s: (0, 0))],
        out_specs=pl.BlockSpec((tile, GDN_WIDTH), lambda b, s: (b * per_b + s, 0)),
        out_shape=jax.ShapeDtypeStruct((bsz * seq, GDN_WIDTH), F32),
        scratch_shapes=[pltpu.VMEM((GDN_HEADS, HEAD_DIM, HEAD_DIM), F32),
                        pltpu.VMEM((n_units, HEAD_DIM + GDN_CHUNK, HEAD_DIM), F32),
                        pltpu.VMEM((n_units, HEAD_DIM, HEAD_DIM), F32),
                        pltpu.VMEM((n_units, GDN_CHUNK, HEAD_DIM), F32),
                        pltpu.VMEM((tile, GDN_WIDTH), F32)],
        compiler_params=_params(("arbitrary", "arbitrary")),
        name="gdn",
    )(gp, gp, gp, gp, sm, out_norm_w.reshape(1, HEAD_DIM))


def _compress_kernel(kc_ref, vc_ref, kpos_ref, kw1_ref, kw2_ref, knw_ref, vpos_ref, vw1_ref, vw2_ref,
                     ko_ref, vo_ref, *, n_cmp):
    ncp = ko_ref.shape[0]
    half = CMP_BLOCK // 2

    def one(x_ref, pos_ref, w1_ref, w2_ref):
        p = jnp.zeros((ncp, HEAD_DIM), F32)
        q = jnp.zeros((ncp, HEAD_DIM), F32)
        for j in range(half):
            a = x_ref[pl.ds(j, ncp, stride=CMP_STRIDE), :]
            p = p + _dot(a + pos_ref[j:j + 1, :], w1_ref[j * HEAD_DIM:(j + 1) * HEAD_DIM, :])
            q = q + _dot(a + pos_ref[half + j:half + j + 1, :],
                         w1_ref[(half + j) * HEAD_DIM:(half + j + 1) * HEAD_DIM, :])
        hid = _silu(p + pltpu.roll(q, ncp - 1, axis=0))
        out = _dot(hid, w2_ref[...])
        valid = lax.broadcasted_iota(jnp.int32, (ncp, HEAD_DIM), 0) < n_cmp
        return jnp.where(valid, out, 0.0)

    k = one(kc_ref, kpos_ref, kw1_ref, kw2_ref)
    ko_ref[...] = jnp.where(lax.broadcasted_iota(jnp.int32, k.shape, 0) < n_cmp,
                            _rms(k, knw_ref[...]), 0.0).astype(BF16)
    vo_ref[...] = one(vc_ref, vpos_ref, vw1_ref, vw2_ref).astype(BF16)


def _compress(cp, kpos, kw1, kw2, knw, vpos, vw1, vw2, bsz, seq):
    ncp = seq // CMP_STRIDE
    n_cmp = (seq - CMP_BLOCK) // CMP_STRIDE + 1
    full = lambda a: pl.BlockSpec(a.shape, lambda b: (0,) * a.ndim)
    knw2 = knw.reshape(1, HEAD_DIM)
    out = pl.BlockSpec((None, ncp, HEAD_DIM), lambda b: (b, 0, 0))
    shp = jax.ShapeDtypeStruct((bsz, ncp, HEAD_DIM), BF16)
    return pl.pallas_call(
        functools.partial(_compress_kernel, n_cmp=n_cmp),
        grid=(bsz,),
        in_specs=[pl.BlockSpec((seq, HEAD_DIM), lambda b: (b, 0)),
                  pl.BlockSpec((seq, HEAD_DIM), lambda b: (b, 1)),
                  full(kpos), full(kw1), full(kw2), full(knw2), full(vpos), full(vw1), full(vw2)],
        out_specs=[out, out],
        out_shape=[shp, shp],
        compiler_params=_params(("arbitrary",)),
        name="compress",
    )(cp, cp, kpos, kw1, kw2, knw2, vpos, vw1, vw2)


def _nsa_kernel(q_ref, kc_ref, vc_ref, ks_ref, vs_ref, kw_ref, vw_ref, sm_ref, ovt_ref, o_ref,
                m_ref, acc_ref, qx_ref, sa_ref, sb_ref, *, tq, tk, n_pick):
    qi = pl.program_id(1)
    t0 = qi * tq
    nh = NSA_HEADS
    ncp = kc_ref.shape[0]
    nsp = ovt_ref.shape[0]
    q = q_ref[...]
    q2 = jnp.concatenate([q[:, h * HEAD_DIM:(h + 1) * HEAD_DIM] for h in range(nh)], axis=0)
    gates = sm_ref[...]

    def gate(h, g):
        c = LANE_GATE + N_GATES * h + g
        return gates[:, c:c + 1]

    def head(x, h):
        return x[h * tq:(h + 1) * tq]

    t_c = t0 + lax.broadcasted_iota(jnp.int32, (tq, ncp), 0)
    cend = lax.broadcasted_iota(jnp.int32, (tq, ncp), 1) * CMP_STRIDE + (CMP_BLOCK - 1)
    bias_c = jnp.where(cend <= t_c, 0.0, NEG_INF)
    sees_any = (t0 + lax.broadcasted_iota(jnp.int32, (tq, 1), 0)) >= CMP_BLOCK - 1
    s_c = _dot_nt(q2, kc_ref[...])
    vcmp = vc_ref[...]
    o_c = []
    psum = jnp.zeros((tq, ncp), F32)
    for h in range(nh):
        s = head(s_c, h) + bias_c
        p = jnp.exp2(s - jnp.max(s, axis=-1, keepdims=True))
        l = jnp.sum(p, axis=-1, keepdims=True)
        p = p * jnp.where(sees_any, 1.0 / l, 0.0)
        psum = psum + p
        o_c.append(_dot(p, vcmp))
    ph, pl_ = _split2(psum)
    ovt = ovt_ref[...]
    dnt = functools.partial(lax.dot_general, dimension_numbers=(((1,), (1,)), ((), ())),
                            preferred_element_type=F32)
    imp_t = dnt(ovt, ph) + dnt(ovt, pl_)

    wlen = WINDOW + tq
    start = pl.multiple_of(jnp.maximum(t0 - WINDOW, 0), tq)
    kw = kw_ref[pl.ds(start, wlen), :]
    vw = vw_ref[pl.ds(start, wlen), :]
    s_w = _dot_nt(q2, kw)
    kpos_w = start + lax.broadcasted_iota(jnp.int32, (tq, wlen), 1)
    t_w = t0 + lax.broadcasted_iota(jnp.int32, (tq, wlen), 0)
    bias_w = jnp.where(kpos_w <= t_w, jnp.where(kpos_w > t_w - WINDOW, 0.0, NEG_INF), NEG_INF)
    o_w = []
    for h in range(nh):
        s = head(s_w, h) + bias_w
        p = jnp.exp2(s - jnp.max(s, axis=-1, keepdims=True))
        pv = _dot(p, vw)
        o_w.append(pv[:, :HEAD_DIM] / pv[:, HEAD_DIM:])

    jj = lax.broadcasted_iota(jnp.int32, (nsp, tq), 0)
    cur = (t0 + lax.broadcasted_iota(jnp.int32, (nsp, tq), 1)) // SEL_BLOCK
    forced = (jj == 0) | (jj == cur) | (jj == cur - 1)
    valid = jj <= cur
    notsel = jnp.where(valid & forced, 0.0, 1.0)
    work = jnp.where(valid, jnp.where(forced, -BIG, imp_t), -BIG)
    jf = jj.astype(F32)
    for _ in range(n_pick - 3):
        mx = jnp.max(work, axis=0, keepdims=True)
        cand = jnp.where(work == mx, jnp.where(mx > -1e38, jf, float(nsp)), float(nsp))
        first = jnp.min(cand, axis=0, keepdims=True)
        pick = jf == first
        notsel = jnp.where(pick, 0.0, notsel)
        work = jnp.where(pick, -BIG, work)
    notsel_q = notsel.T.astype(BF16)
    q2x = jnp.concatenate([q2, jnp.concatenate([notsel_q] * nh, axis=0)], axis=1)

    m_ref[...] = jnp.full(m_ref.shape, NEG_INF, F32)
    acc_ref[...] = jnp.zeros(acc_ref.shape, F32)
    n_last = (t0 + tq - 1) // tk
    qx_ref[...] = q2x

    def scores_into(dst_ref, kt):
        k0 = pl.multiple_of(kt * tk, tk)
        dst_ref[...] = _dot_nt(qx_ref[...], ks_ref[pl.ds(k0, tk), :])

    def step(kt, s_ref, causal):
        if causal:
            kpos = kt * tk + lax.broadcasted_iota(jnp.int32, (tq, tk), 1)
            t_s = t0 + lax.broadcasted_iota(jnp.int32, (tq, tk), 0)
            bias = jnp.where(kpos <= t_s, 0.0, NEG_INF)
        ps, alphas = [], []
        for h in range(nh):
            rows = slice(h * tq, (h + 1) * tq)
            sh = s_ref[rows] + bias if causal else s_ref[rows]
            m_prev = m_ref[rows]
            m_next = jnp.maximum(m_prev, jnp.max(sh, axis=-1, keepdims=True))
            alphas.append(jnp.exp2(m_prev - m_next))
            ps.append(jnp.exp2(sh - jnp.concatenate([m_next] * (tk // LANES), axis=1)).astype(BF16))
            m_ref[rows] = m_next
        k0 = pl.multiple_of(kt * tk, tk)
        pv = jnp.dot(jnp.concatenate(ps, axis=0), vs_ref[pl.ds(k0, tk), :],
                     preferred_element_type=F32)
        alpha = jnp.concatenate(alphas, axis=0)
        acc_ref[...] = jnp.concatenate([alpha, alpha], axis=1) * acc_ref[...] + pv

    scores_into(sa_ref, 0)

    def pair(j, carry):
        scores_into(sb_ref, 2 * j + 1)
        step(2 * j, sa_ref, False)
        scores_into(sa_ref, 2 * j + 2)
        step(2 * j + 1, sb_ref, False)
        return carry

    lax.fori_loop(0, n_last // 2, pair, 0)

    @pl.when(n_last % 2 == 1)
    def _():
        scores_into(sb_ref, n_last)
        step(n_last - 1, sa_ref, False)
        step(n_last, sb_ref, True)

    @pl.when(n_last % 2 == 0)
    def _():
        step(n_last, sa_ref, True)

    for h in range(nh):
        rows = slice(h * tq, (h + 1) * tq)
        l_s = acc_ref[rows, HEAD_DIM:]
        o_s = acc_ref[rows, :HEAD_DIM] * jnp.where(l_s > 0.0, 1.0 / l_s, 0.0)
        o_ref[:, h * HEAD_DIM:(h + 1) * HEAD_DIM] = (gate(h, 0) * o_c[h] + gate(h, 1) * o_s
                                                     + gate(h, 2) * o_w[h])


def _nsa(qn, kcmp, vcmp, ksl, vsl, kwn, vwn, sm, overlap, bsz, seq):
    tq, tk = 256, 512
    per_b = seq // tq
    ncp = kcmp.shape[1]
    nsp = overlap.shape[0]
    n_pick = min(N_SELECT, seq // SEL_BLOCK)
    assert n_pick == N_SELECT and seq % tk == 0 and seq >= WINDOW + tq and nsp == LANES
    ext = HEAD_DIM + LANES
    seq_kv = lambda w: pl.BlockSpec((seq, w), lambda b, i: (b, 0))
    cmp_kv = lambda: pl.BlockSpec((None, ncp, HEAD_DIM), lambda b, i: (b, 0, 0))
    return pl.pallas_call(
        functools.partial(_nsa_kernel, tq=tq, tk=tk, n_pick=n_pick),
        grid=(bsz, per_b),
        in_specs=[pl.BlockSpec((tq, NSA_WIDTH), lambda b, i: (b * per_b + i, 0)),
                  cmp_kv(), cmp_kv(), seq_kv(ext), seq_kv(ext), seq_kv(HEAD_DIM), seq_kv(ext),
                  pl.BlockSpec((tq, LANES), lambda b, i: (b * per_b + i, 0)),
                  pl.BlockSpec((nsp, ncp), lambda b, i: (0, 0))],
        out_specs=pl.BlockSpec((tq, NSA_WIDTH), lambda b, i: (b * per_b + i, 0)),
        out_shape=jax.ShapeDtypeStruct((bsz * seq, NSA_WIDTH), F32),
        scratch_shapes=[pltpu.VMEM((NSA_HEADS * tq, LANES), F32),
                        pltpu.VMEM((NSA_HEADS * tq, ext), F32),
                        pltpu.VMEM((NSA_HEADS * tq, ext), BF16),
                        pltpu.VMEM((NSA_HEADS * tq, tk), F32),
                        pltpu.VMEM((NSA_HEADS * tq, tk), F32)],
        compiler_params=_params(("arbitrary", "arbitrary")),
        name="nsa",
    )(qn, kcmp, vcmp, ksl, vsl, kwn, vwn, sm, overlap)


def _overlap_matrix(seq):
    n_cmp = (seq - CMP_BLOCK) // CMP_STRIDE + 1
    n_sel = seq // SEL_BLOCK
    ncp = seq // CMP_STRIDE
    nsp = -(-n_sel // LANES) * LANES
    cs = np.arange(n_cmp) * CMP_STRIDE
    ss = np.arange(n_sel) * SEL_BLOCK
    ov = np.minimum(cs[:, None] + CMP_BLOCK, ss[None, :] + SEL_BLOCK) - np.maximum(cs[:, None], ss[None, :])
    out = np.zeros((nsp, ncp), np.float32)
    out[:n_sel, :n_cmp] = (np.clip(ov, 0, None).astype(np.float32) / CMP_BLOCK).T
    return jnp.asarray(out, dtype=BF16)


def _ffn_kernel(x_ref, og_ref, on_ref, mod_ref, wo_ref, nw_ref, wu_ref, cw_ref, cb_ref, wd_ref, o_ref,
                up_ref, act_ref, *, tm, d_ff, cw, down_parts):
    head = SUBLANES

    @pl.when(pl.program_id(1) == 0)
    def _():
        up_ref[:, :head, :] = jnp.zeros((2, head, d_ff), F32)

    m = mod_ref[0]
    mixed = _dot(og_ref[...], wo_ref[:GDN_WIDTH, :]) + _dot(on_ref[...], wo_ref[GDN_WIDTH:, :])
    x1 = x_ref[...] + m[2:3] * mixed
    h2 = _rms_mod(x1, nw_ref[...], m[4:5], m[3:4]).astype(BF16)
    n_chunks = d_ff // cw
    y = None
    done = 0
    for c in range(n_chunks):
        cols = slice(c * cw, (c + 1) * cw)
        halves = []
        for g in range(2):
            wcols = slice(g * d_ff + c * cw, g * d_ff + (c + 1) * cw)
            up_ref[g, head:, cols] = jnp.dot(h2, wu_ref[:, wcols], preferred_element_type=F32)
            z = cb_ref[:, wcols]
            for j in range(FFN_CONV):
                tap = up_ref[g, pl.ds(head - (FFN_CONV - 1) + j, tm), cols]
                z = z + cw_ref[j:j + 1, wcols] * tap
            up_ref[g, :head, cols] = up_ref[g, tm:tm + head, cols]
            halves.append(z)
        act_ref[:, cols] = (_silu(halves[0]) * halves[1]).astype(BF16)
        if (c + 1) * down_parts // n_chunks > c * down_parts // n_chunks:
            part = slice(done * cw, (c + 1) * cw)
            d = jnp.dot(act_ref[:, part], wd_ref[part, :], preferred_element_type=F32)
            y = d if y is None else y + d
            done = c + 1
    o_ref[...] = x1 + m[5:6] * y


def _ffn(x2, o_g, o_n, mod3, w_out_b, norm2_w, w_up_b, conv_w, conv_b, w_down_b, bsz, seq):
    t, d = x2.shape
    d_ff = w_down_b.shape[0]
    tm, cw = 512, 256
    assert d_ff % cw == 0
    per_b = seq // tm
    full = lambda a: pl.BlockSpec(a.shape, lambda b, s: (0,) * a.ndim)
    conv_b2 = conv_b.reshape(1, -1)
    norm2_w2 = norm2_w.reshape(1, d)
    return pl.pallas_call(
        functools.partial(_ffn_kernel, tm=tm, d_ff=d_ff, cw=cw, down_parts=1),
        grid=(bsz, per_b),
        in_specs=[pl.BlockSpec((tm, d), lambda b, s: (b * per_b + s, 0)),
                  pl.BlockSpec((tm, GDN_WIDTH), lambda b, s: (b * per_b + s, 0)),
                  pl.BlockSpec((tm, NSA_WIDTH), lambda b, s: (b * per_b + s, 0)),
                  pl.BlockSpec((1, 6, d), lambda b, s: (b, 0, 0)),
                  full(w_out_b), full(norm2_w2), full(w_up_b), full(conv_w), full(conv_b2),
                  full(w_down_b)],
        out_specs=pl.BlockSpec((tm, d), lambda b, s: (b * per_b + s, 0)),
        out_shape=jax.ShapeDtypeStruct((t, d), F32),
        scratch_shapes=[pltpu.VMEM((2, SUBLANES + tm, d_ff), F32),
                        pltpu.VMEM((tm, d_ff), BF16)],
        compiler_params=_params(("arbitrary", "arbitrary")),
        name="ffn",
    )(x2, o_g, o_n, mod3, w_out_b, norm2_w2, w_up_b, conv_w, conv_b2, w_down_b)


def _pad_lanes(a, n):
    return jnp.pad(a, ((0, 0), (0, n - a.shape[1])))


def _layer(x, c, ada_w, ada_b, norm1_w, w_in, gdn_conv_w, gdn_A_log, gdn_dt_bias, gdn_out_norm_w,
           nsa_q_norm_w, nsa_k_norm_cmp, nsa_k_norm_slc, nsa_k_norm_win, cmp_k_pos, cmp_k_w1, cmp_k_w2,
           cmp_v_pos, cmp_v_w1, cmp_v_w2, w_out, norm2_w, ffn_w_up, ffn_conv_w, ffn_conv_b, ffn_w_down):
    bsz, seq, d = x.shape
    x2 = x.reshape(bsz * seq, d)
    mod3 = _ada(c, ada_w, ada_b).reshape(bsz, 6, d)

    gp, sm, cp, qn, ksl, vsl, kwn, vwn = _inproj(
        x2, mod3, norm1_w, w_in, gdn_conv_w, _pad_lanes(gdn_A_log.reshape(1, -1), LANES),
        _pad_lanes(gdn_dt_bias.reshape(1, -1), LANES), nsa_q_norm_w, nsa_k_norm_slc, nsa_k_norm_win, seq)

    o_g = _gdn(gp, sm, gdn_out_norm_w, bsz, seq)

    kcmp, vcmp = _compress(cp, cmp_k_pos, cmp_k_w1.astype(BF16), cmp_k_w2.astype(BF16), nsa_k_norm_cmp,
                           cmp_v_pos, cmp_v_w1.astype(BF16), cmp_v_w2.astype(BF16), bsz, seq)
    o_n = _nsa(qn, kcmp, vcmp, ksl, vsl, kwn, vwn, sm, _overlap_matrix(seq), bsz, seq)

    out = _ffn(x2, o_g, o_n, mod3, w_out.astype(BF16), norm2_w, ffn_w_up.astype(BF16), ffn_conv_w,
               ffn_conv_b, ffn_w_down.astype(BF16), bsz, seq)
    return out.reshape(bsz, seq, d)


def kernel(x, c, ada_w, ada_b, norm1_w, w_in, gdn_conv_w, gdn_A_log, gdn_dt_bias, gdn_out_norm_w, nsa_q_norm_w, nsa_k_norm_cmp, nsa_k_norm_slc, nsa_k_norm_win, cmp_k_pos, cmp_k_w1, cmp_k_w2, cmp_v_pos, cmp_v_w1, cmp_v_w2, w_out, norm2_w, ffn_w_up, ffn_conv_w, ffn_conv_b, ffn_w_down):
    for l in range(ada_w.shape[0]):
        x = _layer(x, c, ada_w[l], ada_b[l], norm1_w[l], w_in[l], gdn_conv_w[l], gdn_A_log[l],
                   gdn_dt_bias[l], gdn_out_norm_w[l], nsa_q_norm_w[l], nsa_k_norm_cmp[l],
                   nsa_k_norm_slc[l], nsa_k_norm_win[l], cmp_k_pos[l], cmp_k_w1[l], cmp_k_w2[l],
                   cmp_v_pos[l], cmp_v_w1[l], cmp_v_w2[l], w_out[l], norm2_w[l], ffn_w_up[l],
                   ffn_conv_w[l], ffn_conv_b[l], ffn_w_down[l])
    return x
```

```python
import functools

import numpy as np
import jax
import jax.numpy as jnp
from jax import lax
from jax.experimental import pallas as pl
from jax.experimental.pallas import tpu as pltpu

F32 = jnp.float32
BF16 = jnp.bfloat16

HEAD_DIM = 128
GDN_HEADS = 4
NSA_HEADS = 4
GDN_WIDTH = GDN_HEADS * HEAD_DIM
NSA_WIDTH = NSA_HEADS * HEAD_DIM
GDN_CONV = 4
GDN_CHUNK = 64
CMP_BLOCK = 32
CMP_STRIDE = 16
SEL_BLOCK = 64
N_SELECT = 16
WINDOW = 512
FFN_CONV = 3
N_GATES = 3
EPS = 1e-6
NEG_INF = -1e30
BIG = 3e38
LOG2E = 1.4426950408889634

LANES = 128
SUBLANES = 8
VMEM_LIMIT = 56 * 1024 * 1024

COL_GQ, COL_GK, COL_GV, COL_GZ = 0, 512, 1024, 1536
COL_NQ = 2048
COL_KC, COL_VC, COL_KSL, COL_VSL, COL_KWN, COL_VWN = 2560, 2688, 2816, 2944, 3072, 3200
COL_SMALL = 3328
N_PROJ = 3456
LANE_A, LANE_BETA, LANE_GATE = 0, 4, 8


def _params(sem):
    return pltpu.CompilerParams(dimension_semantics=sem, vmem_limit_bytes=VMEM_LIMIT)


def _dot(a, b):
    return jnp.dot(a.astype(BF16), b.astype(BF16), preferred_element_type=F32)


def _dot_nt(a, b):
    return lax.dot_general(a.astype(BF16), b.astype(BF16), (((1,), (1,)), ((), ())),
                           preferred_element_type=F32)


def _dot_tn(a, b):
    return lax.dot_general(a.astype(BF16), b.astype(BF16), (((0,), (0,)), ((), ())),
                           preferred_element_type=F32)


def _split2(a):
    hi = a.astype(BF16)
    lo = (a - hi.astype(F32)).astype(BF16)
    return hi, lo


def _split3(a):
    h1 = a.astype(BF16)
    r = a - h1.astype(F32)
    h2 = r.astype(BF16)
    h3 = (r - h2.astype(F32)).astype(BF16)
    return h1, h2, h3


def _dot_hp(a, b):
    ah, al = _split2(a)
    bh, bl = _split2(b)
    d = functools.partial(jnp.dot, preferred_element_type=F32)
    return d(ah, bh) + (d(ah, bl) + d(al, bh))


def _dot_exact_lhs(a_exact, b):
    a = a_exact.astype(BF16)
    b1, b2, b3 = _split3(b)
    d = functools.partial(jnp.dot, preferred_element_type=F32)
    return d(a, b1) + (d(a, b2) + d(a, b3))


def _sigmoid(x):
    return 1.0 / (1.0 + jnp.exp(-x))


def _silu(x):
    h = 0.5 * x
    return h + h * jnp.tanh(h)


def _softplus(x):
    return jnp.maximum(x, 0.0) + jnp.log1p(jnp.exp(-jnp.abs(x)))


def _ada_kernel(c_ref, w_ref, b_ref, o_ref):
    c = c_ref[...]
    o_ref[...] = _dot_hp(_silu(c), w_ref[...]) + b_ref[...]


def _ada(c, ada_w, ada_b):
    bsz, d = c.shape
    n = ada_w.shape[1]
    tn = 1024
    return pl.pallas_call(
        _ada_kernel,
        grid=(n // tn,),
        in_specs=[pl.BlockSpec((bsz, d), lambda j: (0, 0)),
                  pl.BlockSpec((d, tn), lambda j: (0, j)),
                  pl.BlockSpec((1, tn), lambda j: (0, j))],
        out_specs=pl.BlockSpec((bsz, tn), lambda j: (0, j)),
        out_shape=jax.ShapeDtypeStruct((bsz, n), F32),
        compiler_params=_params(("arbitrary",)),
        name="ada",
    )(c, ada_w, ada_b.reshape(1, n))


def _rms_mod(x, nw, scale, shift):
    y = x * lax.rsqrt(jnp.mean(x * x, axis=-1, keepdims=True) + EPS) * nw
    return y * (1.0 + scale) + shift


def _rms(x, w):
    return x * lax.rsqrt(jnp.mean(x * x, axis=-1, keepdims=True) + EPS) * w


def _inproj_kernel(x_ref, mod_ref, nw_ref, w_in_ref, cw_ref, alog_ref, dtb_ref, qw_ref, kslw_ref,
                   kwnw_ref, gp_ref, sm_ref, cp_ref, qn_ref, kext_ref, vext_ref, kwn_ref, vwext_ref,
                   w_ref, *pre_refs, tm, seq):
    head = SUBLANES
    i = pl.program_id(0)

    @pl.when(i == 0)
    def _():
        n_small = 2 * GDN_HEADS
        n_mid = COL_SMALL - COL_NQ
        n_gate = N_GATES * NSA_HEADS
        rb = 4 * LANES
        for r0 in range(0, COL_NQ, rb):
            w_ref[r0:r0 + rb, :] = w_in_ref[r0:r0 + rb, :].astype(BF16)
        for r0 in range(0, n_mid, rb):
            n = min(rb, n_mid - r0)
            w_ref[COL_NQ + r0:COL_NQ + r0 + n, :] = w_in_ref[COL_NQ + n_small + r0:COL_NQ + n_small + r0 + n, :].astype(BF16)
        tail = jnp.concatenate(
            [w_in_ref[COL_NQ:COL_NQ + n_small, :],
             w_in_ref[COL_NQ + n_small + n_mid:COL_NQ + n_small + n_mid + n_gate, :],
             jnp.zeros((LANES - n_small - n_gate, w_in_ref.shape[1]), F32)], axis=0)
        w_ref[COL_SMALL:, :] = tail.astype(BF16)

    @pl.when(i % (seq // tm) == 0)
    def _():
        for pre_ref in pre_refs:
            pre_ref[:head, :] = jnp.zeros((head, pre_ref.shape[1]), F32)

    m = mod_ref[0]
    h = _rms_mod(x_ref[...], nw_ref[...], m[1:2], m[0:1]).astype(BF16)

    def proj(c0, width):
        return lax.dot_general(h, w_ref[c0:c0 + width, :], (((1,), (1,)), ((), ())),
                               preferred_element_type=F32)

    for idx in range(3):
        cols = slice(idx * GDN_WIDTH, (idx + 1) * GDN_WIDTH)
        pre_ref = pre_refs[idx]
        pre_ref[head:, :] = proj(COL_GQ + idx * GDN_WIDTH, GDN_WIDTH)
        y = cw_ref[GDN_CONV - 1:GDN_CONV, cols] * pre_ref[head:, :]
        for j in range(GDN_CONV - 1):
            y = y + cw_ref[j:j + 1, cols] * pre_ref[pl.ds(head - (GDN_CONV - 1) + j, tm), :]
        pre_ref[:head, :] = pre_ref[tm:tm + head, :]
        y = _silu(y)
        for hh in range(GDN_HEADS):
            hl = slice(hh * HEAD_DIM, (hh + 1) * HEAD_DIM)
            yh = y[:, hl]
            if idx == 0:
                yh = yh * (lax.rsqrt(jnp.sum(yh * yh, axis=-1, keepdims=True) + EPS) * HEAD_DIM ** -0.5)
            elif idx == 1:
                yh = yh * lax.rsqrt(jnp.sum(yh * yh, axis=-1, keepdims=True) + EPS)
            gp_ref[:, idx * GDN_WIDTH + hh * HEAD_DIM:idx * GDN_WIDTH + (hh + 1) * HEAD_DIM] = yh
    gp_ref[:, COL_GZ:COL_GZ + GDN_WIDTH] = _silu(proj(COL_GZ, GDN_WIDTH))

    nq = proj(COL_NQ, NSA_WIDTH)
    for hh in range(NSA_HEADS):
        hl = slice(hh * HEAD_DIM, (hh + 1) * HEAD_DIM)
        qn_ref[:, hl] = (_rms(nq[:, hl], qw_ref[...]) * (HEAD_DIM ** -0.5 * LOG2E)).astype(BF16)
    r = proj(COL_KC, 4 * HEAD_DIM)
    cp_ref[...] = r[:, :2 * HEAD_DIM]
    ones = jnp.ones((tm, LANES), BF16)
    pos = (i * tm + lax.broadcasted_iota(jnp.int32, (tm, LANES), 0)) % seq
    lane = lax.broadcasted_iota(jnp.int32, (tm, LANES), 1)
    kext_ref[:, :HEAD_DIM] = _rms(r[:, 2 * HEAD_DIM:3 * HEAD_DIM], kslw_ref[...]).astype(BF16)
    kext_ref[:, HEAD_DIM:] = jnp.where(lane == pos // SEL_BLOCK, NEG_INF, 0.0).astype(BF16)
    vext_ref[:, :HEAD_DIM] = r[:, 3 * HEAD_DIM:].astype(BF16)
    vext_ref[:, HEAD_DIM:] = ones
    r = proj(COL_KWN, 2 * HEAD_DIM + LANES)
    kwn_ref[...] = _rms(r[:, :HEAD_DIM], kwnw_ref[...]).astype(BF16)
    vwext_ref[:, :HEAD_DIM] = r[:, HEAD_DIM:2 * HEAD_DIM].astype(BF16)
    vwext_ref[:, HEAD_DIM:] = ones
    small = r[:, 2 * HEAD_DIM:]
    log_decay = -jnp.exp(alog_ref[...]) * _softplus(small + dtb_ref[...])
    sm_ref[...] = jnp.where(lane < LANE_BETA, log_decay, _sigmoid(small))


def _inproj(x2, mod3, norm1_w, w_in, gdn_conv_w, alog_p, dtb_p, q_w, ksl_w, kwn_w, seq):
    t, d = x2.shape
    assert w_in.shape[1] == COL_SMALL + 2 * GDN_HEADS + N_GATES * NSA_HEADS
    w_in_t = w_in.T
    tm = 512
    per_b = seq // tm
    assert seq // SEL_BLOCK <= LANES
    ext = HEAD_DIM + LANES
    full = lambda a: pl.BlockSpec(a.shape, lambda i: (0,) * a.ndim)
    row = lambda w: pl.BlockSpec((tm, w), lambda i: (i, 0))
    vecs = [q_w.reshape(1, -1), ksl_w.reshape(1, -1), kwn_w.reshape(1, -1)]
    out_widths = [(4 * GDN_WIDTH, F32), (LANES, F32), (2 * HEAD_DIM, F32), (NSA_WIDTH, BF16),
                  (ext, BF16), (ext, BF16), (HEAD_DIM, BF16), (ext, BF16)]
    return pl.pallas_call(
        functools.partial(_inproj_kernel, tm=tm, seq=seq),
        grid=(t // tm,),
        in_specs=[row(d),
                  pl.BlockSpec((1, 6, d), lambda i: (i // per_b, 0, 0)),
                  pl.BlockSpec((1, d), lambda i: (0, 0)),
                  full(w_in_t), full(gdn_conv_w), full(alog_p), full(dtb_p)] + [full(v) for v in vecs],
        out_specs=[row(w) for w, _ in out_widths],
        out_shape=[jax.ShapeDtypeStruct((t, w), dt) for w, dt in out_widths],
        scratch_shapes=[pltpu.VMEM((N_PROJ, d), BF16)] + [pltpu.VMEM((SUBLANES + tm, GDN_WIDTH), F32)] * 3,
        compiler_params=_params(("arbitrary",)),
        name="inproj",
    )(x2, mod3, norm1_w.reshape(1, d), w_in_t, gdn_conv_w, alog_p, dtb_p, *vecs)


def _gdn_kernel(q_ref, k_ref, v_ref, z_ref, sm_ref, onw_ref, o_ref,
                state_ref, pq_ref, p2_ref, oi_ref, os_ref, *, tile):
    c_len = GDN_CHUNK
    n_chunks = tile // c_len

    @pl.when(pl.program_id(1) == 0)
    def _():
        state_ref[...] = jnp.zeros_like(state_ref)

    qc = q_ref[...]
    kc = k_ref[...]
    vc = v_ref[...]
    g_all = sm_ref[...]
    beta_all = g_all

    ri = lax.broadcasted_iota(jnp.int32, (tile, tile), 0)
    ci = lax.broadcasted_iota(jnp.int32, (tile, tile), 1)
    lblk = jnp.where((ri // c_len == ci // c_len) & (ci <= ri), 1.0, 0.0)
    gc_all = _dot_exact_lhs(lblk, g_all)
    gc_t = gc_all.T
    gl_all = jnp.concatenate(
        [jnp.broadcast_to(gc_all[(c + 1) * c_len - 1:(c + 1) * c_len], (c_len, LANES))
         for c in range(n_chunks)], axis=0)
    eg_all = jnp.exp(gc_all)
    ekd_all = jnp.exp(gl_all - gc_all)
    egl_all = jnp.exp(gl_all)

    r64 = lax.broadcasted_iota(jnp.int32, (c_len, c_len), 0)
    c64 = lax.broadcasted_iota(jnp.int32, (c_len, c_len), 1)
    incl = r64 >= c64
    strict = r64 > c64
    eye = jnp.where(r64 == c64, 1.0, 0.0)

    def lane_bcast(x, lane):
        return jnp.broadcast_to(x[:, lane:lane + 1], (tile, HEAD_DIM))

    units = [(c, h) for c in range(n_chunks) for h in range(GDN_HEADS)]
    heads = []
    for h in range(GDN_HEADS):
        hl = slice(h * HEAD_DIM, (h + 1) * HEAD_DIM)
        qh = qc[:, hl]
        kh = kc[:, hl]
        bb = lane_bcast(beta_all, LANE_BETA + h)
        eg = lane_bcast(eg_all, LANE_A + h)
        kb = kh * bb
        heads.append(dict(qd=qh * eg, k=kh, kb=kb, q=qh, vb=vc[:, hl] * bb, kbe=kb * eg,
                          kd=kh * lane_bcast(ekd_all, LANE_A + h),
                          egl=lane_bcast(egl_all, LANE_A + h)))

    a_mats, attns = [], []
    for c, h in units:
        rows = slice(c * c_len, (c + 1) * c_len)
        hd = heads[h]
        gcol = gc_all[rows, LANE_A + h:LANE_A + h + 1]
        grow = gc_t[LANE_A + h:LANE_A + h + 1, rows]
        decay = jnp.exp(jnp.where(incl, gcol - grow, NEG_INF))
        gram = _dot_nt(jnp.concatenate([hd["kb"][rows], hd["q"][rows]], axis=0), hd["k"][rows])
        a_mats.append(jnp.where(strict, gram[:c_len] * decay, 0.0))
        attns.append(jnp.where(incl, gram[c_len:] * decay, 0.0))
    pack = LANES // c_len
    bd_mask = (lax.broadcasted_iota(jnp.int32, (LANES, LANES), 0) // c_len
               == lax.broadcasted_iota(jnp.int32, (LANES, LANES), 1) // c_len)
    eye_cat = jnp.concatenate([eye] * pack, axis=1)

    def block_diag(x):
        zero = jnp.zeros((LANES, LANES), BF16)
        hi, lo = (jnp.where(bd_mask, jnp.concatenate([part] * pack, axis=0), zero) for part in _split2(x))
        return jnp.concatenate([hi, hi], axis=0), lo

    def hp(a, b):
        ah, al = _split2(a)
        d = functools.partial(jnp.dot, preferred_element_type=F32)
        return d(jnp.concatenate([ah, al], axis=1), b[0]) + d(ah, b[1])

    n_levels = c_len.bit_length() - 2
    n_groups = len(units) // pack
    xs = [-jnp.concatenate(a_mats[g * pack:(g + 1) * pack], axis=1) for g in range(n_groups)]
    ts = [eye_cat + x for x in xs]
    xs = [hp(x, block_diag(x)) for x in xs]
    for level in range(1, n_levels):
        rs = [hp(jnp.concatenate([x, t], axis=0), block_diag(x)) for x, t in zip(xs, ts)]
        xs = [r[:c_len] for r in rs]
        ts = [t + r[c_len:] for t, r in zip(ts, rs)]
    ts = [t + hp(t, block_diag(x)) for x, t in zip(xs, ts)]
    t_mats = [t[:, j * c_len:(j + 1) * c_len] for t in ts for j in range(pack)]
    for i, (c, h) in enumerate(units):
        rows = slice(c * c_len, (c + 1) * c_len)
        hd = heads[h]
        uw = _dot(t_mats[i], jnp.concatenate([hd["vb"][rows], hd["kbe"][rows]], axis=1))
        aw = _dot(attns[i], uw)
        kw = _dot_tn(hd["kd"][rows], uw)
        pq_ref[i, :HEAD_DIM] = kw[:, HEAD_DIM:]
        pq_ref[i, HEAD_DIM:] = hd["qd"][rows] - aw[:, HEAD_DIM:]
        p2_ref[i] = kw[:, :HEAD_DIM]
        oi_ref[i] = aw[:, :HEAD_DIM]

    for i, (c, h) in enumerate(units):
        rows = slice(c * c_len, (c + 1) * c_len)
        state = state_ref[h]
        r = _dot(pq_ref[i], state)
        os_ref[rows, h * HEAD_DIM:(h + 1) * HEAD_DIM] = r[HEAD_DIM:] + oi_ref[i]
        egl = heads[h]["egl"][c * c_len:c * c_len + 1]
        state_ref[h] = state * egl - r[:HEAD_DIM] + p2_ref[i]

    for h in range(GDN_HEADS):
        hl = slice(h * HEAD_DIM, (h + 1) * HEAD_DIM)
        o = os_ref[:, hl]
        o = o * lax.rsqrt(jnp.mean(o * o, axis=-1, keepdims=True) + EPS) * onw_ref[...]
        o_ref[:, hl] = o * z_ref[:, hl]


def _gdn(gp, sm, out_norm_w, bsz, seq):
    tile = 256
    per_b = seq // tile
    n_units = tile // GDN_CHUNK * GDN_HEADS
    col = lambda c: pl.BlockSpec((tile, GDN_WIDTH), lambda b, s: (b * per_b + s, c))
    return pl.pallas_call(
        functools.partial(_gdn_kernel, tile=tile),
        grid=(bsz, per_b),
        in_specs=[col(0), col(1), col(2), col(3),
                  pl.BlockSpec((tile, LANES), lambda b, s: (b * per_b + s, 0)),
                  pl.BlockSpec((1, HEAD_DIM), lambda b, s: (0, 0))],
        out_specs=pl.BlockSpec((tile, GDN_WIDTH), lambda b, s: (b * per_b + s, 0)),
        out_shape=jax.ShapeDtypeStruct((bsz * seq, GDN_WIDTH), F32),
        scratch_shapes=[pltpu.VMEM((GDN_HEADS, HEAD_DIM, HEAD_DIM), F32),
                        pltpu.VMEM((n_units, HEAD_DIM + GDN_CHUNK, HEAD_DIM), F32),
                        pltpu.VMEM((n_units, HEAD_DIM, HEAD_DIM), F32),
                        pltpu.VMEM((n_units, GDN_CHUNK, HEAD_DIM), F32),
                        pltpu.VMEM((tile, GDN_WIDTH), F32)],
        compiler_params=_params(("arbitrary", "arbitrary")),
        name="gdn",
    )(gp, gp, gp, gp, sm, out_norm_w.reshape(1, HEAD_DIM))


def _compress_kernel(kc_ref, vc_ref, kpos_ref, kw1_ref, kw2_ref, knw_ref, vpos_ref, vw1_ref, vw2_ref,
                     ko_ref, vo_ref, *, n_cmp):
    ncp = ko_ref.shape[0]
    half = CMP_BLOCK // 2

    def one(x_ref, pos_ref, w1_ref, w2_ref):
        p = jnp.zeros((ncp, HEAD_DIM), F32)
        q = jnp.zeros((ncp, HEAD_DIM), F32)
        for j in range(half):
            a = x_ref[pl.ds(j, ncp, stride=CMP_STRIDE), :]
            p = p + _dot(a + pos_ref[j:j + 1, :], w1_ref[j * HEAD_DIM:(j + 1) * HEAD_DIM, :])
            q = q + _dot(a + pos_ref[half + j:half + j + 1, :],
                         w1_ref[(half + j) * HEAD_DIM:(half + j + 1) * HEAD_DIM, :])
        hid = _silu(p + pltpu.roll(q, ncp - 1, axis=0))
        out = _dot(hid, w2_ref[...])
        valid = lax.broadcasted_iota(jnp.int32, (ncp, HEAD_DIM), 0) < n_cmp
        return jnp.where(valid, out, 0.0)

    k = one(kc_ref, kpos_ref, kw1_ref, kw2_ref)
    ko_ref[...] = jnp.where(lax.broadcasted_iota(jnp.int32, k.shape, 0) < n_cmp,
                            _rms(k, knw_ref[...]), 0.0).astype(BF16)
    vo_ref[...] = one(vc_ref, vpos_ref, vw1_ref, vw2_ref).astype(BF16)


def _compress(cp, kpos, kw1, kw2, knw, vpos, vw1, vw2, bsz, seq):
    ncp = seq // CMP_STRIDE
    n_cmp = (seq - CMP_BLOCK) // CMP_STRIDE + 1
    full = lambda a: pl.BlockSpec(a.shape, lambda b: (0,) * a.ndim)
    knw2 = knw.reshape(1, HEAD_DIM)
    out = pl.BlockSpec((None, ncp, HEAD_DIM), lambda b: (b, 0, 0))
    shp = jax.ShapeDtypeStruct((bsz, ncp, HEAD_DIM), BF16)
    return pl.pallas_call(
        functools.partial(_compress_kernel, n_cmp=n_cmp),
        grid=(bsz,),
        in_specs=[pl.BlockSpec((seq, HEAD_DIM), lambda b: (b, 0)),
                  pl.BlockSpec((seq, HEAD_DIM), lambda b: (b, 1)),
                  full(kpos), full(kw1), full(kw2), full(knw2), full(vpos), full(vw1), full(vw2)],
        out_specs=[out, out],
        out_shape=[shp, shp],
        compiler_params=_params(("arbitrary",)),
        name="compress",
    )(cp, cp, kpos, kw1, kw2, knw2, vpos, vw1, vw2)


def _nsa_kernel(q_ref, kc_ref, vc_ref, ks_ref, vs_ref, kw_ref, vw_ref, sm_ref, ovt_ref, o_ref,
                m_ref, acc_ref, qx_ref, sa_ref, sb_ref, sw_ref, bc_ref, bw_ref, ps_ref, *, tq, tk, n_pick):
    qi = pl.program_id(1)
    t0 = qi * tq
    nh = NSA_HEADS
    ncp = kc_ref.shape[0]
    nsp = ovt_ref.shape[0]
    q = q_ref[...]
    for h in range(nh):
        qx_ref[h * tq:(h + 1) * tq, :HEAD_DIM] = q[:, h * HEAD_DIM:(h + 1) * HEAD_DIM]
    gates = sm_ref[...]

    def gate(h, g):
        c = LANE_GATE + N_GATES * h + g
        return gates[:, c:c + 1]

    t_c = t0 + lax.broadcasted_iota(jnp.int32, (tq, ncp), 0)
    cend = lax.broadcasted_iota(jnp.int32, (tq, ncp), 1) * CMP_STRIDE + (CMP_BLOCK - 1)
    bc_ref[...] = jnp.where(cend <= t_c, 0.0, NEG_INF)
    sees_any = (t0 + lax.broadcasted_iota(jnp.int32, (tq, 1), 0)) >= CMP_BLOCK - 1
    sa_ref[:, :ncp] = _dot_nt(qx_ref[:, :HEAD_DIM], kc_ref[...])
    vcmp = vc_ref[...]
    for h in range(nh):
        rows = slice(h * tq, (h + 1) * tq)
        s = sa_ref[rows, :ncp] + bc_ref[...]
        p = jnp.exp2(s - jnp.max(s, axis=-1, keepdims=True))
        l = jnp.sum(p, axis=-1, keepdims=True)
        p = p * jnp.where(sees_any, 1.0 / l, 0.0)
        if h == 0:
            ps_ref[...] = p
        else:
            ps_ref[...] += p
        o_ref[:, h * HEAD_DIM:(h + 1) * HEAD_DIM] = gate(h, 0) * _dot(p, vcmp)
    ph, pl_ = _split2(ps_ref[...])
    ovt = ovt_ref[...]
    dnt = functools.partial(lax.dot_general, dimension_numbers=(((1,), (1,)), ((), ())),
                            preferred_element_type=F32)
    imp_t = dnt(ovt, ph) + dnt(ovt, pl_)

    wlen = WINDOW + tq
    start = pl.multiple_of(jnp.maximum(t0 - WINDOW, 0), tq)
    vw = vw_ref[pl.ds(start, wlen), :]
    sw_ref[...] = _dot_nt(qx_ref[:, :HEAD_DIM], kw_ref[pl.ds(start, wlen), :])
    kpos_w = start + lax.broadcasted_iota(jnp.int32, (tq, wlen), 1)
    t_w = t0 + lax.broadcasted_iota(jnp.int32, (tq, wlen), 0)
    bw_ref[...] = jnp.where(kpos_w <= t_w, jnp.where(kpos_w > t_w - WINDOW, 0.0, NEG_INF), NEG_INF)
    for h in range(nh):
        s = sw_ref[h * tq:(h + 1) * tq] + bw_ref[...]
        p = jnp.exp2(s - jnp.max(s, axis=-1, keepdims=True))
        pv = _dot(p, vw)
        o_ref[:, h * HEAD_DIM:(h + 1) * HEAD_DIM] += gate(h, 2) * (pv[:, :HEAD_DIM] / pv[:, HEAD_DIM:])

    jj = lax.broadcasted_iota(jnp.int32, (nsp, tq), 0)
    cur = (t0 + lax.broadcasted_iota(jnp.int32, (nsp, tq), 1)) // SEL_BLOCK
    forced = (jj == 0) | (jj == cur) | (jj == cur - 1)
    valid = jj <= cur
    notsel = jnp.where(valid & forced, 0.0, 1.0)
    work = jnp.where(valid, jnp.where(forced, -BIG, imp_t), -BIG)
    jf = jj.astype(F32)
    for _ in range(n_pick - 3):
        mx = jnp.max(work, axis=0, keepdims=True)
        cand = jnp.where(work == mx, jnp.where(mx > -1e38, jf, float(nsp)), float(nsp))
        first = jnp.min(cand, axis=0, keepdims=True)
        pick = jf == first
        notsel = jnp.where(pick, 0.0, notsel)
        work = jnp.where(pick, -BIG, work)
    notsel_q = notsel.T.astype(BF16)
    for h in range(nh):
        qx_ref[h * tq:(h + 1) * tq, HEAD_DIM:] = notsel_q

    m_ref[...] = jnp.full(m_ref.shape, NEG_INF, F32)
    acc_ref[...] = jnp.zeros(acc_ref.shape, F32)
    n_last = (t0 + tq - 1) // tk

    def scores_into(dst_ref, kt):
        k0 = pl.multiple_of(kt * tk, tk)
        dst_ref[...] = _dot_nt(qx_ref[...], ks_ref[pl.ds(k0, tk), :])

    def step(kt, s_ref, causal):
        if causal:
            kpos = kt * tk + lax.broadcasted_iota(jnp.int32, (tq, tk), 1)
            t_s = t0 + lax.broadcasted_iota(jnp.int32, (tq, tk), 0)
            bias = jnp.where(kpos <= t_s, 0.0, NEG_INF)
        ps, alphas = [], []
        for h in range(nh):
            rows = slice(h * tq, (h + 1) * tq)
            sh = s_ref[rows] + bias if causal else s_ref[rows]
            m_prev = m_ref[rows]
            m_next = jnp.maximum(m_prev, jnp.max(sh, axis=-1, keepdims=True))
            alphas.append(jnp.exp2(m_prev - m_next))
            ps.append(jnp.exp2(sh - jnp.concatenate([m_next] * (tk // LANES), axis=1)).astype(BF16))
            m_ref[rows] = m_next
        k0 = pl.multiple_of(kt * tk, tk)
        pv = jnp.dot(jnp.concatenate(ps, axis=0), vs_ref[pl.ds(k0, tk), :],
                     preferred_element_type=F32)
        alpha = jnp.concatenate(alphas, axis=0)
        acc_ref[...] = jnp.concatenate([alpha, alpha], axis=1) * acc_ref[...] + pv

    scores_into(sa_ref, 0)

    def pair(j, carry):
        scores_into(sb_ref, 2 * j + 1)
        step(2 * j, sa_ref, False)
        scores_into(sa_ref, 2 * j + 2)
        step(2 * j + 1, sb_ref, False)
        return carry

    lax.fori_loop(0, n_last // 2, pair, 0)

    @pl.when(n_last % 2 == 1)
    def _():
        scores_into(sb_ref, n_last)
        step(n_last - 1, sa_ref, False)
        step(n_last, sb_ref, True)

    @pl.when(n_last % 2 == 0)
    def _():
        step(n_last, sa_ref, True)

    for h in range(nh):
        rows = slice(h * tq, (h + 1) * tq)
        l_s = acc_ref[rows, HEAD_DIM:]
        o_s = acc_ref[rows, :HEAD_DIM] * jnp.where(l_s > 0.0, 1.0 / l_s, 0.0)
        o_ref[:, h * HEAD_DIM:(h + 1) * HEAD_DIM] += gate(h, 1) * o_s


def _nsa(qn, kcmp, vcmp, ksl, vsl, kwn, vwn, sm, overlap, bsz, seq):
    tq, tk = 256, 512
    per_b = seq // tq
    ncp = kcmp.shape[1]
    nsp = overlap.shape[0]
    n_pick = min(N_SELECT, seq // SEL_BLOCK)
    assert n_pick == N_SELECT and seq % tk == 0 and seq >= WINDOW + tq and nsp == LANES and ncp <= tk
    ext = HEAD_DIM + LANES
    seq_kv = lambda w: pl.BlockSpec((seq, w), lambda b, i: (b, 0))
    cmp_kv = lambda: pl.BlockSpec((None, ncp, HEAD_DIM), lambda b, i: (b, 0, 0))
    return pl.pallas_call(
        functools.partial(_nsa_kernel, tq=tq, tk=tk, n_pick=n_pick),
        grid=(bsz, per_b),
        in_specs=[pl.BlockSpec((tq, NSA_WIDTH), lambda b, i: (b * per_b + i, 0)),
                  cmp_kv(), cmp_kv(), seq_kv(ext), seq_kv(ext), seq_kv(HEAD_DIM), seq_kv(ext),
                  pl.BlockSpec((tq, LANES), lambda b, i: (b * per_b + i, 0)),
                  pl.BlockSpec((nsp, ncp), lambda b, i: (0, 0))],
        out_specs=pl.BlockSpec((tq, NSA_WIDTH), lambda b, i: (b * per_b + i, 0)),
        out_shape=jax.ShapeDtypeStruct((bsz * seq, NSA_WIDTH), F32),
        scratch_shapes=[pltpu.VMEM((NSA_HEADS * tq, LANES), F32),
                        pltpu.VMEM((NSA_HEADS * tq, ext), F32),
                        pltpu.VMEM((NSA_HEADS * tq, ext), BF16),
                        pltpu.VMEM((NSA_HEADS * tq, tk), F32),
                        pltpu.VMEM((NSA_HEADS * tq, tk), F32),
                        pltpu.VMEM((NSA_HEADS * tq, WINDOW + tq), F32),
                        pltpu.VMEM((tq, ncp), F32),
                        pltpu.VMEM((tq, WINDOW + tq), F32),
                        pltpu.VMEM((tq, ncp), F32)],
        compiler_params=_params(("arbitrary", "arbitrary")),
        name="nsa",
    )(qn, kcmp, vcmp, ksl, vsl, kwn, vwn, sm, overlap)


def _overlap_matrix(seq):
    n_cmp = (seq - CMP_BLOCK) // CMP_STRIDE + 1
    n_sel = seq // SEL_BLOCK
    ncp = seq // CMP_STRIDE
    nsp = -(-n_sel // LANES) * LANES
    cs = np.arange(n_cmp) * CMP_STRIDE
    ss = np.arange(n_sel) * SEL_BLOCK
    ov = np.minimum(cs[:, None] + CMP_BLOCK, ss[None, :] + SEL_BLOCK) - np.maximum(cs[:, None], ss[None, :])
    out = np.zeros((nsp, ncp), np.float32)
    out[:n_sel, :n_cmp] = (np.clip(ov, 0, None).astype(np.float32) / CMP_BLOCK).T
    return jnp.asarray(out, dtype=BF16)


def _ffn_kernel(x_ref, og_ref, on_ref, mod_ref, wo_ref, nw_ref, wu_ref, cw_ref, cb_ref, wd_ref, o_ref,
                up_ref, act_ref, *, tm, d_ff, cw, down_parts):
    head = SUBLANES

    @pl.when(pl.program_id(1) == 0)
    def _():
        up_ref[:, :head, :] = jnp.zeros((2, head, d_ff), F32)

    m = mod_ref[0]
    mixed = _dot(og_ref[...], wo_ref[:GDN_WIDTH, :]) + _dot(on_ref[...], wo_ref[GDN_WIDTH:, :])
    x1 = x_ref[...] + m[2:3] * mixed
    h2 = _rms_mod(x1, nw_ref[...], m[4:5], m[3:4]).astype(BF16)
    n_chunks = d_ff // cw
    y = None
    done = 0
    for c in range(n_chunks):
        cols = slice(c * cw, (c + 1) * cw)
        halves = []
        for g in range(2):
            wcols = slice(g * d_ff + c * cw, g * d_ff + (c + 1) * cw)
            up_ref[g, head:, cols] = jnp.dot(h2, wu_ref[:, wcols], preferred_element_type=F32)
            z = cb_ref[:, wcols]
            for j in range(FFN_CONV):
                tap = up_ref[g, pl.ds(head - (FFN_CONV - 1) + j, tm), cols]
                z = z + cw_ref[j:j + 1, wcols] * tap
            up_ref[g, :head, cols] = up_ref[g, tm:tm + head, cols]
            halves.append(z)
        act_ref[:, cols] = (_silu(halves[0]) * halves[1]).astype(BF16)
        if (c + 1) * down_parts // n_chunks > c * down_parts // n_chunks:
            part = slice(done * cw, (c + 1) * cw)
            d = jnp.dot(act_ref[:, part], wd_ref[part, :], preferred_element_type=F32)
            y = d if y is None else y + d
            done = c + 1
    o_ref[...] = x1 + m[5:6] * y


def _ffn(x2, o_g, o_n, mod3, w_out_b, norm2_w, w_up_b, conv_w, conv_b, w_down_b, bsz, seq):
    t, d = x2.shape
    d_ff = w_down_b.shape[0]
    tm, cw = 512, 256
    assert d_ff % cw == 0
    per_b = seq // tm
    full = lambda a: pl.BlockSpec(a.shape, lambda b, s: (0,) * a.ndim)
    conv_b2 = conv_b.reshape(1, -1)
    norm2_w2 = norm2_w.reshape(1, d)
    return pl.pallas_call(
        functools.partial(_ffn_kernel, tm=tm, d_ff=d_ff, cw=cw, down_parts=1),
        grid=(bsz, per_b),
        in_specs=[pl.BlockSpec((tm, d), lambda b, s: (b * per_b + s, 0)),
                  pl.BlockSpec((tm, GDN_WIDTH), lambda b, s: (b * per_b + s, 0)),
                  pl.BlockSpec((tm, NSA_WIDTH), lambda b, s: (b * per_b + s, 0)),
                  pl.BlockSpec((1, 6, d), lambda b, s: (b, 0, 0)),
                  full(w_out_b), full(norm2_w2), full(w_up_b), full(conv_w), full(conv_b2),
                  full(w_down_b)],
        out_specs=pl.BlockSpec((tm, d), lambda b, s: (b * per_b + s, 0)),
        out_shape=jax.ShapeDtypeStruct((t, d), F32),
        scratch_shapes=[pltpu.VMEM((2, SUBLANES + tm, d_ff), F32),
                        pltpu.VMEM((tm, d_ff), BF16)],
        compiler_params=_params(("arbitrary", "arbitrary")),
        name="ffn",
    )(x2, o_g, o_n, mod3, w_out_b, norm2_w2, w_up_b, conv_w, conv_b2, w_down_b)


def _pad_lanes(a, n):
    return jnp.pad(a, ((0, 0), (0, n - a.shape[1])))


def _layer(x, c, ada_w, ada_b, norm1_w, w_in, gdn_conv_w, gdn_A_log, gdn_dt_bias, gdn_out_norm_w,
           nsa_q_norm_w, nsa_k_norm_cmp, nsa_k_norm_slc, nsa_k_norm_win, cmp_k_pos, cmp_k_w1, cmp_k_w2,
           cmp_v_pos, cmp_v_w1, cmp_v_w2, w_out, norm2_w, ffn_w_up, ffn_conv_w, ffn_conv_b, ffn_w_down):
    bsz, seq, d = x.shape
    x2 = x.reshape(bsz * seq, d)
    mod3 = _ada(c, ada_w, ada_b).reshape(bsz, 6, d)

    gp, sm, cp, qn, ksl, vsl, kwn, vwn = _inproj(
        x2, mod3, norm1_w, w_in, gdn_conv_w, _pad_lanes(gdn_A_log.reshape(1, -1), LANES),
        _pad_lanes(gdn_dt_bias.reshape(1, -1), LANES), nsa_q_norm_w, nsa_k_norm_slc, nsa_k_norm_win, seq)

    o_g = _gdn(gp, sm, gdn_out_norm_w, bsz, seq)

    kcmp, vcmp = _compress(cp, cmp_k_pos, cmp_k_w1.astype(BF16), cmp_k_w2.astype(BF16), nsa_k_norm_cmp,
                           cmp_v_pos, cmp_v_w1.astype(BF16), cmp_v_w2.astype(BF16), bsz, seq)
    o_n = _nsa(qn, kcmp, vcmp, ksl, vsl, kwn, vwn, sm, _overlap_matrix(seq), bsz, seq)

    out = _ffn(x2, o_g, o_n, mod3, w_out.astype(BF16), norm2_w, ffn_w_up.astype(BF16), ffn_conv_w,
               ffn_conv_b, ffn_w_down.astype(BF16), bsz, seq)
    return out.reshape(bsz, seq, d)


def kernel(x, c, ada_w, ada_b, norm1_w, w_in, gdn_conv_w, gdn_A_log, gdn_dt_bias, gdn_out_norm_w, nsa_q_norm_w, nsa_k_norm_cmp, nsa_k_norm_slc, nsa_k_norm_win, cmp_k_pos, cmp_k_w1, cmp_k_w2, cmp_v_pos, cmp_v_w1, cmp_v_w2, w_out, norm2_w, ffn_w_up, ffn_conv_w, ffn_conv_b, ffn_w_down):
    for l in range(ada_w.shape[0]):
        x = _layer(x, c, ada_w[l], ada_b[l], norm1_w[l], w_in[l], gdn_conv_w[l], gdn_A_log[l],
                   gdn_dt_bias[l], gdn_out_norm_w[l], nsa_q_norm_w[l], nsa_k_norm_cmp[l],
                   nsa_k_norm_slc[l], nsa_k_norm_win[l], cmp_k_pos[l], cmp_k_w1[l], cmp_k_w2[l],
                   cmp_v_pos[l], cmp_v_w1[l], cmp_v_w2[l], w_out[l], norm2_w[l], ffn_w_up[l],
                   ffn_conv_w[l], ffn_conv_b[l], ffn_w_down[l])
    return x
```

```python
import functools

import numpy as np
import jax
import jax.numpy as jnp
from jax import lax
from jax.experimental import pallas as pl
from jax.experimental.pallas import tpu as pltpu

F32 = jnp.float32
BF16 = jnp.bfloat16

HEAD_DIM = 128
GDN_HEADS = 4
NSA_HEADS = 4
GDN_WIDTH = GDN_HEADS * HEAD_DIM
NSA_WIDTH = NSA_HEADS * HEAD_DIM
GDN_CONV = 4
GDN_CHUNK = 64
CMP_BLOCK = 32
CMP_STRIDE = 16
SEL_BLOCK = 64
N_SELECT = 16
WINDOW = 512
FFN_CONV = 3
N_GATES = 3
EPS = 1e-6
NEG_INF = -1e30
BIG = 3e38
LOG2E = 1.4426950408889634

LANES = 128
SUBLANES = 8
VMEM_LIMIT = 56 * 1024 * 1024

COL_GQ, COL_GZ = 0, 1536
COL_NQ = 2048
COL_KC, COL_KWN = 2560, 3072
COL_SMALL = 3328
N_PROJ = 3456
LANE_A, LANE_BETA, LANE_GATE = 0, 4, 8


def _params(sem):
    return pltpu.CompilerParams(dimension_semantics=sem, vmem_limit_bytes=VMEM_LIMIT)


def _dot(a, b):
    return jnp.dot(a.astype(BF16), b.astype(BF16), preferred_element_type=F32)


def _dot_nt(a, b):
    return lax.dot_general(a.astype(BF16), b.astype(BF16), (((1,), (1,)), ((), ())),
                           preferred_element_type=F32)


def _dot_tn(a, b):
    return lax.dot_general(a.astype(BF16), b.astype(BF16), (((0,), (0,)), ((), ())),
                           preferred_element_type=F32)


def _split2(a):
    hi = a.astype(BF16)
    lo = (a - hi.astype(F32)).astype(BF16)
    return hi, lo


def _split3(a):
    h1 = a.astype(BF16)
    r = a - h1.astype(F32)
    h2 = r.astype(BF16)
    h3 = (r - h2.astype(F32)).astype(BF16)
    return h1, h2, h3


def _dot_hp(a, b):
    ah, al = _split2(a)
    bh, bl = _split2(b)
    d = functools.partial(jnp.dot, preferred_element_type=F32)
    return d(ah, bh) + (d(ah, bl) + d(al, bh))


def _dot_exact_lhs(a_exact, b):
    a = a_exact.astype(BF16)
    b1, b2, b3 = _split3(b)
    d = functools.partial(jnp.dot, preferred_element_type=F32)
    return d(a, b1) + (d(a, b2) + d(a, b3))


def _sigmoid(x):
    return 1.0 / (1.0 + jnp.exp(-x))


def _silu(x):
    h = 0.5 * x
    return h + h * jnp.tanh(h)


def _softplus(x):
    return jnp.maximum(x, 0.0) + jnp.log1p(jnp.exp(-jnp.abs(x)))


def _ada_kernel(c_ref, w_ref, b_ref, o_ref):
    c = c_ref[...]
    o_ref[...] = _dot_hp(_silu(c), w_ref[...]) + b_ref[...]


def _ada(c, ada_w, ada_b):
    bsz, d = c.shape
    n = ada_w.shape[1]
    tn = 1024
    return pl.pallas_call(
        _ada_kernel,
        grid=(n // tn,),
        in_specs=[pl.BlockSpec((bsz, d), lambda j: (0, 0)),
                  pl.BlockSpec((d, tn), lambda j: (0, j)),
                  pl.BlockSpec((1, tn), lambda j: (0, j))],
        out_specs=pl.BlockSpec((bsz, tn), lambda j: (0, j)),
        out_shape=jax.ShapeDtypeStruct((bsz, n), F32),
        compiler_params=_params(("arbitrary",)),
        name="ada",
    )(c, ada_w, ada_b.reshape(1, n))


def _rms_mod(x, nw, scale, shift):
    y = x * lax.rsqrt(jnp.mean(x * x, axis=-1, keepdims=True) + EPS) * nw
    return y * (1.0 + scale) + shift


def _rms(x, w):
    return x * lax.rsqrt(jnp.mean(x * x, axis=-1, keepdims=True) + EPS) * w


def _inproj_kernel(x_ref, mod_ref, nw_ref, w_in_ref, cw_ref, alog_ref, dtb_ref, qw_ref, kslw_ref,
                   kwnw_ref, gp_ref, sm_ref, cp_ref, qn_ref, kext_ref, vext_ref, kwn_ref, vwext_ref,
                   w_ref, *pre_refs, tm, seq):
    head = SUBLANES
    i = pl.program_id(0)

    @pl.when(i == 0)
    def _():
        n_small = 2 * GDN_HEADS
        n_mid = COL_SMALL - COL_NQ
        n_gate = N_GATES * NSA_HEADS
        rb = 4 * LANES
        for r0 in range(0, COL_NQ, rb):
            w_ref[r0:r0 + rb, :] = w_in_ref[r0:r0 + rb, :].astype(BF16)
        for r0 in range(0, n_mid, rb):
            n = min(rb, n_mid - r0)
            w_ref[COL_NQ + r0:COL_NQ + r0 + n, :] = w_in_ref[COL_NQ + n_small + r0:COL_NQ + n_small + r0 + n, :].astype(BF16)
        tail = jnp.concatenate(
            [w_in_ref[COL_NQ:COL_NQ + n_small, :],
             w_in_ref[COL_NQ + n_small + n_mid:COL_NQ + n_small + n_mid + n_gate, :],
             jnp.zeros((LANES - n_small - n_gate, w_in_ref.shape[1]), F32)], axis=0)
        w_ref[COL_SMALL:, :] = tail.astype(BF16)

    @pl.when(i % (seq // tm) == 0)
    def _():
        for pre_ref in pre_refs:
            pre_ref[:head, :] = jnp.zeros((head, pre_ref.shape[1]), F32)

    m = mod_ref[0]
    h = _rms_mod(x_ref[...], nw_ref[...], m[1:2], m[0:1]).astype(BF16)

    def proj(c0, width):
        return lax.dot_general(h, w_ref[c0:c0 + width, :], (((1,), (1,)), ((), ())),
                               preferred_element_type=F32)

    for idx in range(3):
        cols = slice(idx * GDN_WIDTH, (idx + 1) * GDN_WIDTH)
        pre_ref = pre_refs[idx]
        pre_ref[head:, :] = proj(COL_GQ + idx * GDN_WIDTH, GDN_WIDTH)
        y = cw_ref[GDN_CONV - 1:GDN_CONV, cols] * pre_ref[head:, :]
        for j in range(GDN_CONV - 1):
            y = y + cw_ref[j:j + 1, cols] * pre_ref[pl.ds(head - (GDN_CONV - 1) + j, tm), :]
        pre_ref[:head, :] = pre_ref[tm:tm + head, :]
        y = _silu(y)
        for hh in range(GDN_HEADS):
            hl = slice(hh * HEAD_DIM, (hh + 1) * HEAD_DIM)
            yh = y[:, hl]
            if idx == 0:
                yh = yh * (lax.rsqrt(jnp.sum(yh * yh, axis=-1, keepdims=True) + EPS) * HEAD_DIM ** -0.5)
            elif idx == 1:
                yh = yh * lax.rsqrt(jnp.sum(yh * yh, axis=-1, keepdims=True) + EPS)
            gp_ref[:, idx * GDN_WIDTH + hh * HEAD_DIM:idx * GDN_WIDTH + (hh + 1) * HEAD_DIM] = yh
    gp_ref[:, COL_GZ:COL_GZ + GDN_WIDTH] = _silu(proj(COL_GZ, GDN_WIDTH))

    nq = proj(COL_NQ, NSA_WIDTH)
    for hh in range(NSA_HEADS):
        hl = slice(hh * HEAD_DIM, (hh + 1) * HEAD_DIM)
        qn_ref[:, hl] = (_rms(nq[:, hl], qw_ref[...]) * (HEAD_DIM ** -0.5 * LOG2E)).astype(BF16)
    r = proj(COL_KC, 4 * HEAD_DIM)
    cp_ref[...] = r[:, :2 * HEAD_DIM]
    ones = jnp.ones((tm, LANES), BF16)
    pos = (i * tm + lax.broadcasted_iota(jnp.int32, (tm, LANES), 0)) % seq
    lane = lax.broadcasted_iota(jnp.int32, (tm, LANES), 1)
    kext_ref[:, :HEAD_DIM] = _rms(r[:, 2 * HEAD_DIM:3 * HEAD_DIM], kslw_ref[...]).astype(BF16)
    kext_ref[:, HEAD_DIM:] = jnp.where(lane == pos // SEL_BLOCK, NEG_INF, 0.0).astype(BF16)
    vext_ref[:, :HEAD_DIM] = r[:, 3 * HEAD_DIM:].astype(BF16)
    vext_ref[:, HEAD_DIM:] = ones
    r = proj(COL_KWN, 2 * HEAD_DIM + LANES)
    kwn_ref[...] = _rms(r[:, :HEAD_DIM], kwnw_ref[...]).astype(BF16)
    vwext_ref[:, :HEAD_DIM] = r[:, HEAD_DIM:2 * HEAD_DIM].astype(BF16)
    vwext_ref[:, HEAD_DIM:] = ones
    small = r[:, 2 * HEAD_DIM:]
    log_decay = -jnp.exp(alog_ref[...]) * _softplus(small + dtb_ref[...])
    sm_ref[...] = jnp.where(lane < LANE_BETA, log_decay, _sigmoid(small))


def _inproj(x2, mod3, norm1_w, w_in, gdn_conv_w, alog_p, dtb_p, q_w, ksl_w, kwn_w, seq):
    t, d = x2.shape
    assert w_in.shape[1] == COL_SMALL + 2 * GDN_HEADS + N_GATES * NSA_HEADS
    w_in_t = w_in.T
    tm = 512
    per_b = seq // tm
    assert seq // SEL_BLOCK <= LANES
    ext = HEAD_DIM + LANES
    full = lambda a: pl.BlockSpec(a.shape, lambda i: (0,) * a.ndim)
    row = lambda w: pl.BlockSpec((tm, w), lambda i: (i, 0))
    vecs = [q_w.reshape(1, -1), ksl_w.reshape(1, -1), kwn_w.reshape(1, -1)]
    out_widths = [(4 * GDN_WIDTH, F32), (LANES, F32), (2 * HEAD_DIM, F32), (NSA_WIDTH, BF16),
                  (ext, BF16), (ext, BF16), (HEAD_DIM, BF16), (ext, BF16)]
    return pl.pallas_call(
        functools.partial(_inproj_kernel, tm=tm, seq=seq),
        grid=(t // tm,),
        in_specs=[row(d),
                  pl.BlockSpec((1, 6, d), lambda i: (i // per_b, 0, 0)),
                  pl.BlockSpec((1, d), lambda i: (0, 0)),
                  full(w_in_t), full(gdn_conv_w), full(alog_p), full(dtb_p)] + [full(v) for v in vecs],
        out_specs=[row(w) for w, _ in out_widths],
        out_shape=[jax.ShapeDtypeStruct((t, w), dt) for w, dt in out_widths],
        scratch_shapes=[pltpu.VMEM((N_PROJ, d), BF16)] + [pltpu.VMEM((SUBLANES + tm, GDN_WIDTH), F32)] * 3,
        compiler_params=_params(("arbitrary",)),
        name="inproj",
    )(x2, mod3, norm1_w.reshape(1, d), w_in_t, gdn_conv_w, alog_p, dtb_p, *vecs)


def _gdn_kernel(q_ref, k_ref, v_ref, z_ref, sm_ref, onw_ref, o_ref,
                state_ref, pq_ref, p2_ref, oi_ref, os_ref, *, tile):
    c_len = GDN_CHUNK
    n_chunks = tile // c_len

    @pl.when(pl.program_id(1) == 0)
    def _():
        state_ref[...] = jnp.zeros_like(state_ref)

    qc = q_ref[...]
    kc = k_ref[...]
    vc = v_ref[...]
    g_all = sm_ref[...]
    beta_all = g_all

    ri = lax.broadcasted_iota(jnp.int32, (tile, tile), 0)
    ci = lax.broadcasted_iota(jnp.int32, (tile, tile), 1)
    lblk = jnp.where((ri // c_len == ci // c_len) & (ci <= ri), 1.0, 0.0)
    gc_all = _dot_exact_lhs(lblk, g_all)
    gc_t = gc_all.T
    gl_all = jnp.concatenate(
        [jnp.broadcast_to(gc_all[(c + 1) * c_len - 1:(c + 1) * c_len], (c_len, LANES))
         for c in range(n_chunks)], axis=0)
    eg_all = jnp.exp(gc_all)
    ekd_all = jnp.exp(gl_all - gc_all)
    egl_all = jnp.exp(gl_all)

    r64 = lax.broadcasted_iota(jnp.int32, (c_len, c_len), 0)
    c64 = lax.broadcasted_iota(jnp.int32, (c_len, c_len), 1)
    incl = r64 >= c64
    strict = r64 > c64
    eye = jnp.where(r64 == c64, 1.0, 0.0)

    def lane_bcast(x, lane):
        return jnp.broadcast_to(x[:, lane:lane + 1], (tile, HEAD_DIM))

    units = [(c, h) for c in range(n_chunks) for h in range(GDN_HEADS)]
    heads = []
    for h in range(GDN_HEADS):
        hl = slice(h * HEAD_DIM, (h + 1) * HEAD_DIM)
        qh = qc[:, hl]
        kh = kc[:, hl]
        bb = lane_bcast(beta_all, LANE_BETA + h)
        eg = lane_bcast(eg_all, LANE_A + h)
        kb = kh * bb
        heads.append(dict(qd=qh * eg, k=kh, kb=kb, q=qh, vb=vc[:, hl] * bb, kbe=kb * eg,
                          kd=kh * lane_bcast(ekd_all, LANE_A + h),
                          egl=lane_bcast(egl_all, LANE_A + h)))

    a_mats, attns = [], []
    for c, h in units:
        rows = slice(c * c_len, (c + 1) * c_len)
        hd = heads[h]
        gcol = gc_all[rows, LANE_A + h:LANE_A + h + 1]
        grow = gc_t[LANE_A + h:LANE_A + h + 1, rows]
        decay = jnp.exp(jnp.where(incl, gcol - grow, NEG_INF))
        gram = _dot_nt(jnp.concatenate([hd["kb"][rows], hd["q"][rows]], axis=0), hd["k"][rows])
        a_mats.append(jnp.where(strict, gram[:c_len] * decay, 0.0))
        attns.append(jnp.where(incl, gram[c_len:] * decay, 0.0))
    pack = LANES // c_len
    bd_mask = (lax.broadcasted_iota(jnp.int32, (LANES, LANES), 0) // c_len
               == lax.broadcasted_iota(jnp.int32, (LANES, LANES), 1) // c_len)
    eye_cat = jnp.concatenate([eye] * pack, axis=1)

    def block_diag(x):
        zero = jnp.zeros((LANES, LANES), BF16)
        hi, lo = (jnp.where(bd_mask, jnp.concatenate([part] * pack, axis=0), zero) for part in _split2(x))
        return jnp.concatenate([hi, hi], axis=0), lo

    def hp(a, b):
        ah, al = _split2(a)
        d = functools.partial(jnp.dot, preferred_element_type=F32)
        return d(jnp.concatenate([ah, al], axis=1), b[0]) + d(ah, b[1])

    n_levels = c_len.bit_length() - 2
    n_groups = len(units) // pack
    xs = [-jnp.concatenate(a_mats[g * pack:(g + 1) * pack], axis=1) for g in range(n_groups)]
    ts = [eye_cat + x for x in xs]
    xs = [hp(x, block_diag(x)) for x in xs]
    for level in range(1, n_levels):
        rs = [hp(jnp.concatenate([x, t], axis=0), block_diag(x)) for x, t in zip(xs, ts)]
        xs = [r[:c_len] for r in rs]
        ts = [t + r[c_len:] for t, r in zip(ts, rs)]
    ts = [t + hp(t, block_diag(x)) for x, t in zip(xs, ts)]
    t_mats = [t[:, j * c_len:(j + 1) * c_len] for t in ts for j in range(pack)]
    for i, (c, h) in enumerate(units):
        rows = slice(c * c_len, (c + 1) * c_len)
        hd = heads[h]
        uw = _dot(t_mats[i], jnp.concatenate([hd["vb"][rows], hd["kbe"][rows]], axis=1))
        aw = _dot(attns[i], uw)
        kw = _dot_tn(hd["kd"][rows], uw)
        pq_ref[i, :HEAD_DIM] = kw[:, HEAD_DIM:]
        pq_ref[i, HEAD_DIM:] = hd["qd"][rows] - aw[:, HEAD_DIM:]
        p2_ref[i] = kw[:, :HEAD_DIM]
        oi_ref[i] = aw[:, :HEAD_DIM]

    for i, (c, h) in enumerate(units):
        rows = slice(c * c_len, (c + 1) * c_len)
        state = state_ref[h]
        r = _dot(pq_ref[i], state)
        os_ref[rows, h * HEAD_DIM:(h + 1) * HEAD_DIM] = r[HEAD_DIM:] + oi_ref[i]
        egl = heads[h]["egl"][c * c_len:c * c_len + 1]
        state_ref[h] = state * egl - r[:HEAD_DIM] + p2_ref[i]

    for h in range(GDN_HEADS):
        hl = slice(h * HEAD_DIM, (h + 1) * HEAD_DIM)
        o = os_ref[:, hl]
        o = o * lax.rsqrt(jnp.mean(o * o, axis=-1, keepdims=True) + EPS) * onw_ref[...]
        o_ref[:, hl] = o * z_ref[:, hl]


def _gdn(gp, sm, out_norm_w, bsz, seq):
    tile = 256
    per_b = seq // tile
    n_units = tile // GDN_CHUNK * GDN_HEADS
    col = lambda c: pl.BlockSpec((tile, GDN_WIDTH), lambda b, s: (b * per_b + s, c))
    return pl.pallas_call(
        functools.partial(_gdn_kernel, tile=tile),
        grid=(bsz, per_b),
        in_specs=[col(0), col(1), col(2), col(3),
                  pl.BlockSpec((tile, LANES), lambda b, s: (b * per_b + s, 0)),
                  pl.BlockSpec((1, HEAD_DIM), lambda b, s: (0, 0))],
        out_specs=pl.BlockSpec((tile, GDN_WIDTH), lambda b, s: (b * per_b + s, 0)),
        out_shape=jax.ShapeDtypeStruct((bsz * seq, GDN_WIDTH), F32),
        scratch_shapes=[pltpu.VMEM((GDN_HEADS, HEAD_DIM, HEAD_DIM), F32),
                        pltpu.VMEM((n_units, HEAD_DIM + GDN_CHUNK, HEAD_DIM), F32),
                        pltpu.VMEM((n_units, HEAD_DIM, HEAD_DIM), F32),
                        pltpu.VMEM((n_units, GDN_CHUNK, HEAD_DIM), F32),
                        pltpu.VMEM((tile, GDN_WIDTH), F32)],
        compiler_params=_params(("arbitrary", "arbitrary")),
        name="gdn",
    )(gp, gp, gp, gp, sm, out_norm_w.reshape(1, HEAD_DIM))


def _compress_kernel(kc_ref, vc_ref, kpos_ref, kw1_ref, kw2_ref, knw_ref, vpos_ref, vw1_ref, vw2_ref,
                     ko_ref, vo_ref, *, n_cmp):
    ncp = ko_ref.shape[0]
    half = CMP_BLOCK // 2

    def one(x_ref, pos_ref, w1_ref, w2_ref):
        p = jnp.zeros((ncp, HEAD_DIM), F32)
        q = jnp.zeros((ncp, HEAD_DIM), F32)
        for j in range(half):
            a = x_ref[pl.ds(j, ncp, stride=CMP_STRIDE), :]
            p = p + _dot(a + pos_ref[j:j + 1, :], w1_ref[j * HEAD_DIM:(j + 1) * HEAD_DIM, :])
            q = q + _dot(a + pos_ref[half + j:half + j + 1, :],
                         w1_ref[(half + j) * HEAD_DIM:(half + j + 1) * HEAD_DIM, :])
        hid = _silu(p + pltpu.roll(q, ncp - 1, axis=0))
        out = _dot(hid, w2_ref[...])
        valid = lax.broadcasted_iota(jnp.int32, (ncp, HEAD_DIM), 0) < n_cmp
        return jnp.where(valid, out, 0.0)

    k = one(kc_ref, kpos_ref, kw1_ref, kw2_ref)
    ko_ref[...] = jnp.where(lax.broadcasted_iota(jnp.int32, k.shape, 0) < n_cmp,
                            _rms(k, knw_ref[...]), 0.0).astype(BF16)
    vo_ref[...] = one(vc_ref, vpos_ref, vw1_ref, vw2_ref).astype(BF16)


def _compress(cp, kpos, kw1, kw2, knw, vpos, vw1, vw2, bsz, seq):
    ncp = seq // CMP_STRIDE
    n_cmp = (seq - CMP_BLOCK) // CMP_STRIDE + 1
    full = lambda a: pl.BlockSpec(a.shape, lambda b: (0,) * a.ndim)
    knw2 = knw.reshape(1, HEAD_DIM)
    out = pl.BlockSpec((None, ncp, HEAD_DIM), lambda b: (b, 0, 0))
    shp = jax.ShapeDtypeStruct((bsz, ncp, HEAD_DIM), BF16)
    return pl.pallas_call(
        functools.partial(_compress_kernel, n_cmp=n_cmp),
        grid=(bsz,),
        in_specs=[pl.BlockSpec((seq, HEAD_DIM), lambda b: (b, 0)),
                  pl.BlockSpec((seq, HEAD_DIM), lambda b: (b, 1)),
                  full(kpos), full(kw1), full(kw2), full(knw2), full(vpos), full(vw1), full(vw2)],
        out_specs=[out, out],
        out_shape=[shp, shp],
        compiler_params=_params(("arbitrary",)),
        name="compress",
    )(cp, cp, kpos, kw1, kw2, knw2, vpos, vw1, vw2)


def _nsa_kernel(q_ref, kc_ref, vc_ref, ks_ref, vs_ref, kw_ref, vw_ref, sm_ref, ovt_ref, o_ref,
                m_ref, acc_ref, qx_ref, sa_ref, sb_ref, sw_ref, bc_ref, bw_ref, ps_ref, *, tq, tk, n_pick):
    qi = pl.program_id(1)
    t0 = qi * tq
    nh = NSA_HEADS
    ncp = kc_ref.shape[0]
    nsp = ovt_ref.shape[0]
    q = q_ref[...]
    for h in range(nh):
        qx_ref[h * tq:(h + 1) * tq, :HEAD_DIM] = q[:, h * HEAD_DIM:(h + 1) * HEAD_DIM]
    gates = sm_ref[...]

    def gate(h, g):
        c = LANE_GATE + N_GATES * h + g
        return gates[:, c:c + 1]

    t_c = t0 + lax.broadcasted_iota(jnp.int32, (tq, ncp), 0)
    cend = lax.broadcasted_iota(jnp.int32, (tq, ncp), 1) * CMP_STRIDE + (CMP_BLOCK - 1)
    bc_ref[...] = jnp.where(cend <= t_c, 0.0, NEG_INF)
    sees_any = (t0 + lax.broadcasted_iota(jnp.int32, (tq, 1), 0)) >= CMP_BLOCK - 1
    sa_ref[:, :ncp] = _dot_nt(qx_ref[:, :HEAD_DIM], kc_ref[...])
    vcmp = vc_ref[...]
    for h in range(nh):
        rows = slice(h * tq, (h + 1) * tq)
        s = sa_ref[rows, :ncp] + bc_ref[...]
        p = jnp.exp2(s - jnp.max(s, axis=-1, keepdims=True))
        l = jnp.sum(p, axis=-1, keepdims=True)
        p = p * jnp.where(sees_any, 1.0 / l, 0.0)
        if h == 0:
            ps_ref[...] = p
        else:
            ps_ref[...] += p
        o_ref[:, h * HEAD_DIM:(h + 1) * HEAD_DIM] = gate(h, 0) * _dot(p, vcmp)
    ph, pl_ = _split2(ps_ref[...])
    ovt = ovt_ref[...]
    dnt = functools.partial(lax.dot_general, dimension_numbers=(((1,), (1,)), ((), ())),
                            preferred_element_type=F32)
    imp_t = dnt(ovt, ph) + dnt(ovt, pl_)

    wlen = WINDOW + tq
    start = pl.multiple_of(jnp.maximum(t0 - WINDOW, 0), tq)
    vw = vw_ref[pl.ds(start, wlen), :]
    sw_ref[...] = _dot_nt(qx_ref[:, :HEAD_DIM], kw_ref[pl.ds(start, wlen), :])
    kpos_w = start + lax.broadcasted_iota(jnp.int32, (tq, wlen), 1)
    t_w = t0 + lax.broadcasted_iota(jnp.int32, (tq, wlen), 0)
    bw_ref[...] = jnp.where(kpos_w <= t_w, jnp.where(kpos_w > t_w - WINDOW, 0.0, NEG_INF), NEG_INF)
    for h in range(nh):
        s = sw_ref[h * tq:(h + 1) * tq] + bw_ref[...]
        p = jnp.exp2(s - jnp.max(s, axis=-1, keepdims=True))
        pv = _dot(p, vw)
        o_ref[:, h * HEAD_DIM:(h + 1) * HEAD_DIM] += gate(h, 2) * (pv[:, :HEAD_DIM] / pv[:, HEAD_DIM:])

    jj = lax.broadcasted_iota(jnp.int32, (nsp, tq), 0)
    cur = (t0 + lax.broadcasted_iota(jnp.int32, (nsp, tq), 1)) // SEL_BLOCK
    forced = (jj == 0) | (jj == cur) | (jj == cur - 1)
    valid = jj <= cur
    notsel = jnp.where(valid & forced, 0.0, 1.0)
    work = jnp.where(valid, jnp.where(forced, -BIG, imp_t), -BIG)
    jf = jj.astype(F32)
    for _ in range(n_pick - 3):
        mx = jnp.max(work, axis=0, keepdims=True)
        cand = jnp.where(work == mx, jnp.where(mx > -1e38, jf, float(nsp)), float(nsp))
        first = jnp.min(cand, axis=0, keepdims=True)
        pick = jf == first
        notsel = jnp.where(pick, 0.0, notsel)
        work = jnp.where(pick, -BIG, work)
    notsel_q = notsel.T.astype(BF16)
    for h in range(nh):
        qx_ref[h * tq:(h + 1) * tq, HEAD_DIM:] = notsel_q

    m_ref[...] = jnp.full(m_ref.shape, NEG_INF, F32)
    acc_ref[...] = jnp.zeros(acc_ref.shape, F32)
    n_last = (t0 + tq - 1) // tk

    def scores_into(dst_ref, kt):
        k0 = pl.multiple_of(kt * tk, tk)
        dst_ref[...] = _dot_nt(qx_ref[...], ks_ref[pl.ds(k0, tk), :])

    def step(kt, s_ref, causal):
        if causal:
            kpos = kt * tk + lax.broadcasted_iota(jnp.int32, (tq, tk), 1)
            t_s = t0 + lax.broadcasted_iota(jnp.int32, (tq, tk), 0)
            bias = jnp.where(kpos <= t_s, 0.0, NEG_INF)
        ps, alphas = [], []
        for h in range(nh):
            rows = slice(h * tq, (h + 1) * tq)
            sh = s_ref[rows] + bias if causal else s_ref[rows]
            m_prev = m_ref[rows]
            m_next = jnp.maximum(m_prev, jnp.max(sh, axis=-1, keepdims=True))
            alphas.append(jnp.exp2(m_prev - m_next))
            ps.append(jnp.exp2(sh - jnp.concatenate([m_next] * (tk // LANES), axis=1)).astype(BF16))
            m_ref[rows] = m_next
        k0 = pl.multiple_of(kt * tk, tk)
        pv = jnp.dot(jnp.concatenate(ps, axis=0), vs_ref[pl.ds(k0, tk), :],
                     preferred_element_type=F32)
        alpha = jnp.concatenate(alphas, axis=0)
        acc_ref[...] = jnp.concatenate([alpha, alpha], axis=1) * acc_ref[...] + pv

    scores_into(sa_ref, 0)

    def pair(j, carry):
        scores_into(sb_ref, 2 * j + 1)
        step(2 * j, sa_ref, False)
        scores_into(sa_ref, 2 * j + 2)
        step(2 * j + 1, sb_ref, False)
        return carry

    lax.fori_loop(0, n_last // 2, pair, 0)

    @pl.when(n_last % 2 == 1)
    def _():
        scores_into(sb_ref, n_last)
        step(n_last - 1, sa_ref, False)
        step(n_last, sb_ref, True)

    @pl.when(n_last % 2 == 0)
    def _():
        step(n_last, sa_ref, True)

    for h in range(nh):
        rows = slice(h * tq, (h + 1) * tq)
        l_s = acc_ref[rows, HEAD_DIM:]
        o_s = acc_ref[rows, :HEAD_DIM] * jnp.where(l_s > 0.0, 1.0 / l_s, 0.0)
        o_ref[:, h * HEAD_DIM:(h + 1) * HEAD_DIM] += gate(h, 1) * o_s


def _nsa(qn, kcmp, vcmp, ksl, vsl, kwn, vwn, sm, overlap, bsz, seq):
    tq, tk = 256, 512
    per_b = seq // tq
    ncp = kcmp.shape[1]
    nsp = overlap.shape[0]
    n_pick = min(N_SELECT, seq // SEL_BLOCK)
    assert n_pick == N_SELECT and seq % tk == 0 and seq >= WINDOW + tq and nsp == LANES and ncp <= tk
    ext = HEAD_DIM + LANES
    seq_kv = lambda w: pl.BlockSpec((seq, w), lambda b, i: (b, 0))
    cmp_kv = lambda: pl.BlockSpec((None, ncp, HEAD_DIM), lambda b, i: (b, 0, 0))
    return pl.pallas_call(
        functools.partial(_nsa_kernel, tq=tq, tk=tk, n_pick=n_pick),
        grid=(bsz, per_b),
        in_specs=[pl.BlockSpec((tq, NSA_WIDTH), lambda b, i: (b * per_b + i, 0)),
                  cmp_kv(), cmp_kv(), seq_kv(ext), seq_kv(ext), seq_kv(HEAD_DIM), seq_kv(ext),
                  pl.BlockSpec((tq, LANES), lambda b, i: (b * per_b + i, 0)),
                  pl.BlockSpec((nsp, ncp), lambda b, i: (0, 0))],
        out_specs=pl.BlockSpec((tq, NSA_WIDTH), lambda b, i: (b * per_b + i, 0)),
        out_shape=jax.ShapeDtypeStruct((bsz * seq, NSA_WIDTH), F32),
        scratch_shapes=[pltpu.VMEM((NSA_HEADS * tq, LANES), F32),
                        pltpu.VMEM((NSA_HEADS * tq, ext), F32),
                        pltpu.VMEM((NSA_HEADS * tq, ext), BF16),
                        pltpu.VMEM((NSA_HEADS * tq, tk), F32),
                        pltpu.VMEM((NSA_HEADS * tq, tk), F32),
                        pltpu.VMEM((NSA_HEADS * tq, WINDOW + tq), F32),
                        pltpu.VMEM((tq, ncp), F32),
                        pltpu.VMEM((tq, WINDOW + tq), F32),
                        pltpu.VMEM((tq, ncp), F32)],
        compiler_params=_params(("arbitrary", "arbitrary")),
        name="nsa",
    )(qn, kcmp, vcmp, ksl, vsl, kwn, vwn, sm, overlap)


def _overlap_matrix(seq):
    n_cmp = (seq - CMP_BLOCK) // CMP_STRIDE + 1
    n_sel = seq // SEL_BLOCK
    ncp = seq // CMP_STRIDE
    nsp = -(-n_sel // LANES) * LANES
    cs = np.arange(n_cmp) * CMP_STRIDE
    ss = np.arange(n_sel) * SEL_BLOCK
    ov = np.minimum(cs[:, None] + CMP_BLOCK, ss[None, :] + SEL_BLOCK) - np.maximum(cs[:, None], ss[None, :])
    out = np.zeros((nsp, ncp), np.float32)
    out[:n_sel, :n_cmp] = (np.clip(ov, 0, None).astype(np.float32) / CMP_BLOCK).T
    return jnp.asarray(out, dtype=BF16)


def _ffn_kernel(x_ref, og_ref, on_ref, mod_ref, wo_ref, nw_ref, wu_ref, cw_ref, cb_ref, wd_ref, o_ref,
                up_ref, act_ref, *, tm, d_ff, cw):
    head = SUBLANES

    @pl.when(pl.program_id(1) == 0)
    def _():
        up_ref[:, :head, :] = jnp.zeros((2, head, d_ff), F32)

    m = mod_ref[0]
    mixed = _dot(og_ref[...], wo_ref[:GDN_WIDTH, :]) + _dot(on_ref[...], wo_ref[GDN_WIDTH:, :])
    x1 = x_ref[...] + m[2:3] * mixed
    h2 = _rms_mod(x1, nw_ref[...], m[4:5], m[3:4]).astype(BF16)
    for c in range(d_ff // cw):
        cols = slice(c * cw, (c + 1) * cw)
        halves = []
        for g in range(2):
            wcols = slice(g * d_ff + c * cw, g * d_ff + (c + 1) * cw)
            up_ref[g, head:, cols] = jnp.dot(h2, wu_ref[:, wcols], preferred_element_type=F32)
            z = cb_ref[:, wcols]
            for j in range(FFN_CONV):
                tap = up_ref[g, pl.ds(head - (FFN_CONV - 1) + j, tm), cols]
                z = z + cw_ref[j:j + 1, wcols] * tap
            up_ref[g, :head, cols] = up_ref[g, tm:tm + head, cols]
            halves.append(z)
        act_ref[:, cols] = (_silu(halves[0]) * halves[1]).astype(BF16)
    y = jnp.dot(act_ref[...], wd_ref[...], preferred_element_type=F32)
    o_ref[...] = x1 + m[5:6] * y


def _ffn(x2, o_g, o_n, mod3, w_out_b, norm2_w, w_up_b, conv_w, conv_b, w_down_b, bsz, seq):
    t, d = x2.shape
    d_ff = w_down_b.shape[0]
    tm, cw = 512, 256
    assert d_ff % cw == 0
    per_b = seq // tm
    full = lambda a: pl.BlockSpec(a.shape, lambda b, s: (0,) * a.ndim)
    conv_b2 = conv_b.reshape(1, -1)
    norm2_w2 = norm2_w.reshape(1, d)
    return pl.pallas_call(
        functools.partial(_ffn_kernel, tm=tm, d_ff=d_ff, cw=cw),
        grid=(bsz, per_b),
        in_specs=[pl.BlockSpec((tm, d), lambda b, s: (b * per_b + s, 0)),
                  pl.BlockSpec((tm, GDN_WIDTH), lambda b, s: (b * per_b + s, 0)),
                  pl.BlockSpec((tm, NSA_WIDTH), lambda b, s: (b * per_b + s, 0)),
                  pl.BlockSpec((1, 6, d), lambda b, s: (b, 0, 0)),
                  full(w_out_b), full(norm2_w2), full(w_up_b), full(conv_w), full(conv_b2),
                  full(w_down_b)],
        out_specs=pl.BlockSpec((tm, d), lambda b, s: (b * per_b + s, 0)),
        out_shape=jax.ShapeDtypeStruct((t, d), F32),
        scratch_shapes=[pltpu.VMEM((2, SUBLANES + tm, d_ff), F32),
                        pltpu.VMEM((tm, d_ff), BF16)],
        compiler_params=_params(("arbitrary", "arbitrary")),
        name="ffn",
    )(x2, o_g, o_n, mod3, w_out_b, norm2_w2, w_up_b, conv_w, conv_b2, w_down_b)


def _pad_lanes(a, n):
    return jnp.pad(a, ((0, 0), (0, n - a.shape[1])))


def _layer(x, c, ada_w, ada_b, norm1_w, w_in, gdn_conv_w, gdn_A_log, gdn_dt_bias, gdn_out_norm_w,
           nsa_q_norm_w, nsa_k_norm_cmp, nsa_k_norm_slc, nsa_k_norm_win, cmp_k_pos, cmp_k_w1, cmp_k_w2,
           cmp_v_pos, cmp_v_w1, cmp_v_w2, w_out, norm2_w, ffn_w_up, ffn_conv_w, ffn_conv_b, ffn_w_down):
    bsz, seq, d = x.shape
    x2 = x.reshape(bsz * seq, d)
    mod3 = _ada(c, ada_w, ada_b).reshape(bsz, 6, d)

    gp, sm, cp, qn, ksl, vsl, kwn, vwn = _inproj(
        x2, mod3, norm1_w, w_in, gdn_conv_w, _pad_lanes(gdn_A_log.reshape(1, -1), LANES),
        _pad_lanes(gdn_dt_bias.reshape(1, -1), LANES), nsa_q_norm_w, nsa_k_norm_slc, nsa_k_norm_win, seq)

    o_g = _gdn(gp, sm, gdn_out_norm_w, bsz, seq)

    kcmp, vcmp = _compress(cp, cmp_k_pos, cmp_k_w1.astype(BF16), cmp_k_w2.astype(BF16), nsa_k_norm_cmp,
                           cmp_v_pos, cmp_v_w1.astype(BF16), cmp_v_w2.astype(BF16), bsz, seq)
    o_n = _nsa(qn, kcmp, vcmp, ksl, vsl, kwn, vwn, sm, _overlap_matrix(seq), bsz, seq)

    out = _ffn(x2, o_g, o_n, mod3, w_out.astype(BF16), norm2_w, ffn_w_up.astype(BF16), ffn_conv_w,
               ffn_conv_b, ffn_w_down.astype(BF16), bsz, seq)
    return out.reshape(bsz, seq, d)


def kernel(x, c, ada_w, ada_b, norm1_w, w_in, gdn_conv_w, gdn_A_log, gdn_dt_bias, gdn_out_norm_w, nsa_q_norm_w, nsa_k_norm_cmp, nsa_k_norm_slc, nsa_k_norm_win, cmp_k_pos, cmp_k_w1, cmp_k_w2, cmp_v_pos, cmp_v_w1, cmp_v_w2, w_out, norm2_w, ffn_w_up, ffn_conv_w, ffn_conv_b, ffn_w_down):
    for l in range(ada_w.shape[0]):
        x = _layer(x, c, ada_w[l], ada_b[l], norm1_w[l], w_in[l], gdn_conv_w[l], gdn_A_log[l],
                   gdn_dt_bias[l], gdn_out_norm_w[l], nsa_q_norm_w[l], nsa_k_norm_cmp[l],
                   nsa_k_norm_slc[l], nsa_k_norm_win[l], cmp_k_pos[l], cmp_k_w1[l], cmp_k_w2[l],
                   cmp_v_pos[l], cmp_v_w1[l], cmp_v_w2[l], w_out[l], norm2_w[l], ffn_w_up[l],
                   ffn_conv_w[l], ffn_conv_b[l], ffn_w_down[l])
    return x
```

```python
import functools

import numpy as np
import jax
import jax.numpy as jnp
from jax import lax
from jax.experimental import pallas as pl
from jax.experimental.pallas import tpu as pltpu

F32 = jnp.float32
BF16 = jnp.bfloat16

HEAD_DIM = 128
GDN_HEADS = 4
NSA_HEADS = 4
GDN_WIDTH = GDN_HEADS * HEAD_DIM
NSA_WIDTH = NSA_HEADS * HEAD_DIM
GDN_CONV = 4
GDN_CHUNK = 64
CMP_BLOCK = 32
CMP_STRIDE = 16
SEL_BLOCK = 64
N_SELECT = 16
WINDOW = 512
FFN_CONV = 3
N_GATES = 3
EPS = 1e-6
NEG_INF = -1e30
BIG = 3e38
LOG2E = 1.4426950408889634

LANES = 128
SUBLANES = 8
VMEM_LIMIT = 56 * 1024 * 1024

COL_GQ, COL_GZ = 0, 1536
COL_NQ = 2048
COL_KC, COL_KWN = 2560, 3072
COL_SMALL = 3328
N_PROJ = 3456
LANE_A, LANE_BETA, LANE_GATE = 0, 4, 8


def _params(sem):
    return pltpu.CompilerParams(dimension_semantics=sem, vmem_limit_bytes=VMEM_LIMIT)


def _dot(a, b):
    return jnp.dot(a.astype(BF16), b.astype(BF16), preferred_element_type=F32)


def _dot_nt(a, b):
    return lax.dot_general(a.astype(BF16), b.astype(BF16), (((1,), (1,)), ((), ())),
                           preferred_element_type=F32)


def _dot_tn(a, b):
    return lax.dot_general(a.astype(BF16), b.astype(BF16), (((0,), (0,)), ((), ())),
                           preferred_element_type=F32)


def _split2(a):
    hi = a.astype(BF16)
    lo = (a - hi.astype(F32)).astype(BF16)
    return hi, lo


def _split3(a):
    h1 = a.astype(BF16)
    r = a - h1.astype(F32)
    h2 = r.astype(BF16)
    h3 = (r - h2.astype(F32)).astype(BF16)
    return h1, h2, h3


def _dot_hp(a, b):
    ah, al = _split2(a)
    bh, bl = _split2(b)
    d = functools.partial(jnp.dot, preferred_element_type=F32)
    return d(ah, bh) + (d(ah, bl) + d(al, bh))


def _dot_exact_lhs(a_exact, b):
    a = a_exact.astype(BF16)
    b1, b2, b3 = _split3(b)
    d = functools.partial(jnp.dot, preferred_element_type=F32)
    return d(a, b1) + (d(a, b2) + d(a, b3))


def _sigmoid(x):
    return 1.0 / (1.0 + jnp.exp(-x))


def _silu(x):
    h = 0.5 * x
    return h + h * jnp.tanh(h)


def _softplus(x):
    return jnp.maximum(x, 0.0) + jnp.log1p(jnp.exp(-jnp.abs(x)))


def _ada_kernel(c_ref, w_ref, b_ref, o_ref):
    c = c_ref[...]
    o_ref[...] = _dot_hp(_silu(c), w_ref[...]) + b_ref[...]


def _ada(c, ada_w, ada_b):
    bsz, d = c.shape
    n = ada_w.shape[1]
    tn = 1024
    return pl.pallas_call(
        _ada_kernel,
        grid=(n // tn,),
        in_specs=[pl.BlockSpec((bsz, d), lambda j: (0, 0)),
                  pl.BlockSpec((d, tn), lambda j: (0, j)),
                  pl.BlockSpec((1, tn), lambda j: (0, j))],
        out_specs=pl.BlockSpec((bsz, tn), lambda j: (0, j)),
        out_shape=jax.ShapeDtypeStruct((bsz, n), F32),
        compiler_params=_params(("arbitrary",)),
        name="ada",
    )(c, ada_w, ada_b.reshape(1, n))


def _rms_mod(x, nw, scale, shift):
    y = x * lax.rsqrt(jnp.mean(x * x, axis=-1, keepdims=True) + EPS) * nw
    return y * (1.0 + scale) + shift


def _rms(x, w):
    return x * lax.rsqrt(jnp.mean(x * x, axis=-1, keepdims=True) + EPS) * w


def _inproj_kernel(x_ref, mod_ref, nw_ref, w_in_ref, cw_ref, alog_ref, dtb_ref, qw_ref, kslw_ref,
                   kwnw_ref, gp_ref, sm_ref, cp_ref, qn_ref, kext_ref, vext_ref, kwn_ref, vwext_ref,
                   w_ref, *pre_refs, tm, seq):
    head = SUBLANES
    i = pl.program_id(0)

    @pl.when(i == 0)
    def _():
        n_small = 2 * GDN_HEADS
        n_mid = COL_SMALL - COL_NQ
        n_gate = N_GATES * NSA_HEADS
        rb = 4 * LANES
        for r0 in range(0, COL_NQ, rb):
            w_ref[r0:r0 + rb, :] = w_in_ref[r0:r0 + rb, :].astype(BF16)
        for r0 in range(0, n_mid, rb):
            n = min(rb, n_mid - r0)
            w_ref[COL_NQ + r0:COL_NQ + r0 + n, :] = w_in_ref[COL_NQ + n_small + r0:COL_NQ + n_small + r0 + n, :].astype(BF16)
        tail = jnp.concatenate(
            [w_in_ref[COL_NQ:COL_NQ + n_small, :],
             w_in_ref[COL_NQ + n_small + n_mid:COL_NQ + n_small + n_mid + n_gate, :],
             jnp.zeros((LANES - n_small - n_gate, w_in_ref.shape[1]), F32)], axis=0)
        w_ref[COL_SMALL:, :] = tail.astype(BF16)

    @pl.when(i % (seq // tm) == 0)
    def _():
        for pre_ref in pre_refs:
            pre_ref[:head, :] = jnp.zeros((head, pre_ref.shape[1]), F32)

    m = mod_ref[0]
    h = _rms_mod(x_ref[...], nw_ref[...], m[1:2], m[0:1]).astype(BF16)

    def proj(c0, width):
        return lax.dot_general(h, w_ref[c0:c0 + width, :], (((1,), (1,)), ((), ())),
                               preferred_element_type=F32)

    for idx in range(3):
        cols = slice(idx * GDN_WIDTH, (idx + 1) * GDN_WIDTH)
        pre_ref = pre_refs[idx]
        pre_ref[head:, :] = proj(COL_GQ + idx * GDN_WIDTH, GDN_WIDTH)
        y = cw_ref[GDN_CONV - 1:GDN_CONV, cols] * pre_ref[head:, :]
        for j in range(GDN_CONV - 1):
            y = y + cw_ref[j:j + 1, cols] * pre_ref[pl.ds(head - (GDN_CONV - 1) + j, tm), :]
        pre_ref[:head, :] = pre_ref[tm:tm + head, :]
        y = _silu(y)
        for hh in range(GDN_HEADS):
            hl = slice(hh * HEAD_DIM, (hh + 1) * HEAD_DIM)
            yh = y[:, hl]
            if idx == 0:
                yh = yh * (lax.rsqrt(jnp.sum(yh * yh, axis=-1, keepdims=True) + EPS) * HEAD_DIM ** -0.5)
            elif idx == 1:
                yh = yh * lax.rsqrt(jnp.sum(yh * yh, axis=-1, keepdims=True) + EPS)
            gp_ref[:, idx * GDN_WIDTH + hh * HEAD_DIM:idx * GDN_WIDTH + (hh + 1) * HEAD_DIM] = yh
    gp_ref[:, COL_GZ:COL_GZ + GDN_WIDTH] = _silu(proj(COL_GZ, GDN_WIDTH))

    nq = proj(COL_NQ, NSA_WIDTH)
    for hh in range(NSA_HEADS):
        hl = slice(hh * HEAD_DIM, (hh + 1) * HEAD_DIM)
        qn_ref[:, hl] = (_rms(nq[:, hl], qw_ref[...]) * (HEAD_DIM ** -0.5 * LOG2E)).astype(BF16)
    r = proj(COL_KC, 4 * HEAD_DIM)
    cp_ref[...] = r[:, :2 * HEAD_DIM]
    ones = jnp.ones((tm, LANES), BF16)
    pos = (i * tm + lax.broadcasted_iota(jnp.int32, (tm, LANES), 0)) % seq
    lane = lax.broadcasted_iota(jnp.int32, (tm, LANES), 1)
    kext_ref[:, :HEAD_DIM] = _rms(r[:, 2 * HEAD_DIM:3 * HEAD_DIM], kslw_ref[...]).astype(BF16)
    kext_ref[:, HEAD_DIM:] = jnp.where(lane == pos // SEL_BLOCK, NEG_INF, 0.0).astype(BF16)
    vext_ref[:, :HEAD_DIM] = r[:, 3 * HEAD_DIM:].astype(BF16)
    vext_ref[:, HEAD_DIM:] = ones
    r = proj(COL_KWN, 2 * HEAD_DIM + LANES)
    kwn_ref[...] = _rms(r[:, :HEAD_DIM], kwnw_ref[...]).astype(BF16)
    vwext_ref[:, :HEAD_DIM] = r[:, HEAD_DIM:2 * HEAD_DIM].astype(BF16)
    vwext_ref[:, HEAD_DIM:] = ones
    small = r[:, 2 * HEAD_DIM:]
    log_decay = -jnp.exp(alog_ref[...]) * _softplus(small + dtb_ref[...])
    sm_ref[...] = jnp.where(lane < LANE_BETA, log_decay, _sigmoid(small))


def _inproj(x2, mod3, norm1_w, w_in, gdn_conv_w, alog_p, dtb_p, q_w, ksl_w, kwn_w, seq):
    t, d = x2.shape
    assert w_in.shape[1] == COL_SMALL + 2 * GDN_HEADS + N_GATES * NSA_HEADS
    w_in_t = w_in.T
    tm = 512
    per_b = seq // tm
    assert seq // SEL_BLOCK <= LANES
    ext = HEAD_DIM + LANES
    full = lambda a: pl.BlockSpec(a.shape, lambda i: (0,) * a.ndim)
    row = lambda w: pl.BlockSpec((tm, w), lambda i: (i, 0))
    vecs = [q_w.reshape(1, -1), ksl_w.reshape(1, -1), kwn_w.reshape(1, -1)]
    out_widths = [(4 * GDN_WIDTH, F32), (LANES, F32), (2 * HEAD_DIM, F32), (NSA_WIDTH, BF16),
                  (ext, BF16), (ext, BF16), (HEAD_DIM, BF16), (ext, BF16)]
    return pl.pallas_call(
        functools.partial(_inproj_kernel, tm=tm, seq=seq),
        grid=(t // tm,),
        in_specs=[row(d),
                  pl.BlockSpec((1, 6, d), lambda i: (i // per_b, 0, 0)),
                  pl.BlockSpec((1, d), lambda i: (0, 0)),
                  full(w_in_t), full(gdn_conv_w), full(alog_p), full(dtb_p)] + [full(v) for v in vecs],
        out_specs=[row(w) for w, _ in out_widths],
        out_shape=[jax.ShapeDtypeStruct((t, w), dt) for w, dt in out_widths],
        scratch_shapes=[pltpu.VMEM((N_PROJ, d), BF16)] + [pltpu.VMEM((SUBLANES + tm, GDN_WIDTH), F32)] * 3,
        compiler_params=_params(("arbitrary",)),
        name="inproj",
    )(x2, mod3, norm1_w.reshape(1, d), w_in_t, gdn_conv_w, alog_p, dtb_p, *vecs)


def _gdn_kernel(q_ref, k_ref, v_ref, z_ref, sm_ref, onw_ref, o_ref,
                state_ref, pq_ref, p2_ref, oi_ref, os_ref, *, tile):
    c_len = GDN_CHUNK
    n_chunks = tile // c_len

    @pl.when(pl.program_id(1) == 0)
    def _():
        state_ref[...] = jnp.zeros_like(state_ref)

    qc = q_ref[...]
    kc = k_ref[...]
    vc = v_ref[...]
    g_all = sm_ref[...]
    beta_all = g_all

    ri = lax.broadcasted_iota(jnp.int32, (tile, tile), 0)
    ci = lax.broadcasted_iota(jnp.int32, (tile, tile), 1)
    lblk = jnp.where((ri // c_len == ci // c_len) & (ci <= ri), 1.0, 0.0)
    gc_all = _dot_exact_lhs(lblk, g_all)
    gc_t = gc_all.T
    gl_all = jnp.concatenate(
        [jnp.broadcast_to(gc_all[(c + 1) * c_len - 1:(c + 1) * c_len], (c_len, LANES))
         for c in range(n_chunks)], axis=0)
    eg_all = jnp.exp(gc_all)
    ekd_all = jnp.exp(gl_all - gc_all)
    egl_all = jnp.exp(gl_all)

    r64 = lax.broadcasted_iota(jnp.int32, (c_len, c_len), 0)
    c64 = lax.broadcasted_iota(jnp.int32, (c_len, c_len), 1)
    incl = r64 >= c64
    strict = r64 > c64
    eye = jnp.where(r64 == c64, 1.0, 0.0)

    def lane_bcast(x, lane):
        return jnp.broadcast_to(x[:, lane:lane + 1], (tile, HEAD_DIM))

    units = [(c, h) for c in range(n_chunks) for h in range(GDN_HEADS)]
    heads = []
    for h in range(GDN_HEADS):
        hl = slice(h * HEAD_DIM, (h + 1) * HEAD_DIM)
        qh = qc[:, hl]
        kh = kc[:, hl]
        bb = lane_bcast(beta_all, LANE_BETA + h)
        eg = lane_bcast(eg_all, LANE_A + h)
        kb = kh * bb
        heads.append(dict(qd=qh * eg, k=kh, kb=kb, q=qh, vb=vc[:, hl] * bb, kbe=kb * eg,
                          kd=kh * lane_bcast(ekd_all, LANE_A + h),
                          egl=lane_bcast(egl_all, LANE_A + h)))

    a_mats, attns = [], []
    for c, h in units:
        rows = slice(c * c_len, (c + 1) * c_len)
        hd = heads[h]
        gcol = gc_all[rows, LANE_A + h:LANE_A + h + 1]
        grow = gc_t[LANE_A + h:LANE_A + h + 1, rows]
        decay = jnp.exp(jnp.where(incl, gcol - grow, NEG_INF))
        gram = _dot_nt(jnp.concatenate([hd["kb"][rows], hd["q"][rows]], axis=0), hd["k"][rows])
        a_mats.append(jnp.where(strict, gram[:c_len] * decay, 0.0))
        attns.append(jnp.where(incl, gram[c_len:] * decay, 0.0))
    pack = LANES // c_len
    bd_mask = (lax.broadcasted_iota(jnp.int32, (LANES, LANES), 0) // c_len
               == lax.broadcasted_iota(jnp.int32, (LANES, LANES), 1) // c_len)
    eye_cat = jnp.concatenate([eye] * pack, axis=1)
    row_i = lax.broadcasted_iota(jnp.int32, (c_len, LANES), 0)
    col_i = lax.broadcasted_iota(jnp.int32, (c_len, LANES), 1) % c_len
    leaf = SUBLANES

    def same_block(size):
        return row_i // size == col_i // size

    def block_diag(x):
        zero = jnp.zeros((LANES, LANES), BF16)
        hi, lo = (jnp.where(bd_mask, jnp.concatenate([part] * pack, axis=0), zero) for part in _split2(x))
        return jnp.concatenate([hi, hi], axis=0), lo

    def hp(a, b):
        ah, al = _split2(a)
        d = functools.partial(jnp.dot, preferred_element_type=F32)
        return d(jnp.concatenate([ah, al], axis=1), b[0]) + d(ah, b[1])

    n_groups = len(units) // pack
    a_cat = [jnp.concatenate(a_mats[g * pack:(g + 1) * pack], axis=1) for g in range(n_groups)]
    xs = [jnp.where(same_block(leaf), -a, 0.0) for a in a_cat]
    ts = [eye_cat + x for x in xs]
    xs = [hp(x, block_diag(x)) for x in xs]
    for _ in range(leaf.bit_length() - 3):
        rs = [hp(jnp.concatenate([x, t], axis=0), block_diag(x)) for x, t in zip(xs, ts)]
        xs = [r[:c_len] for r in rs]
        ts = [t + r[c_len:] for t, r in zip(ts, rs)]
    ts = [t + hp(t, block_diag(x)) for x, t in zip(xs, ts)]
    size = leaf
    while size < c_len:
        lower = [jnp.where(same_block(2 * size), jnp.where(same_block(size), 0.0, a), 0.0) for a in a_cat]
        tl = [hp(t, block_diag(l)) for t, l in zip(ts, lower)]
        ts = [t - hp(x, block_diag(t)) for x, t in zip(tl, ts)]
        size *= 2
    t_mats = [t[:, j * c_len:(j + 1) * c_len] for t in ts for j in range(pack)]
    for i, (c, h) in enumerate(units):
        rows = slice(c * c_len, (c + 1) * c_len)
        hd = heads[h]
        uw = _dot(t_mats[i], jnp.concatenate([hd["vb"][rows], hd["kbe"][rows]], axis=1))
        aw = _dot(attns[i], uw)
        kw = _dot_tn(hd["kd"][rows], uw)
        pq_ref[i, :HEAD_DIM] = kw[:, HEAD_DIM:]
        pq_ref[i, HEAD_DIM:] = hd["qd"][rows] - aw[:, HEAD_DIM:]
        p2_ref[i] = kw[:, :HEAD_DIM]
        oi_ref[i] = aw[:, :HEAD_DIM]

    for i, (c, h) in enumerate(units):
        rows = slice(c * c_len, (c + 1) * c_len)
        state = state_ref[h]
        r = _dot(pq_ref[i], state)
        os_ref[rows, h * HEAD_DIM:(h + 1) * HEAD_DIM] = r[HEAD_DIM:] + oi_ref[i]
        egl = heads[h]["egl"][c * c_len:c * c_len + 1]
        state_ref[h] = state * egl - r[:HEAD_DIM] + p2_ref[i]

    for h in range(GDN_HEADS):
        hl = slice(h * HEAD_DIM, (h + 1) * HEAD_DIM)
        o = os_ref[:, hl]
        o = o * lax.rsqrt(jnp.mean(o * o, axis=-1, keepdims=True) + EPS) * onw_ref[...]
        o_ref[:, hl] = o * z_ref[:, hl]


def _gdn(gp, sm, out_norm_w, bsz, seq):
    tile = 256
    per_b = seq // tile
    n_units = tile // GDN_CHUNK * GDN_HEADS
    col = lambda c: pl.BlockSpec((tile, GDN_WIDTH), lambda b, s: (b * per_b + s, c))
    return pl.pallas_call(
        functools.partial(_gdn_kernel, tile=tile),
        grid=(bsz, per_b),
        in_specs=[col(0), col(1), col(2), col(3),
                  pl.BlockSpec((tile, LANES), lambda b, s: (b * per_b + s, 0)),
                  pl.BlockSpec((1, HEAD_DIM), lambda b, s: (0, 0))],
        out_specs=pl.BlockSpec((tile, GDN_WIDTH), lambda b, s: (b * per_b + s, 0)),
        out_shape=jax.ShapeDtypeStruct((bsz * seq, GDN_WIDTH), F32),
        scratch_shapes=[pltpu.VMEM((GDN_HEADS, HEAD_DIM, HEAD_DIM), F32),
                        pltpu.VMEM((n_units, HEAD_DIM + GDN_CHUNK, HEAD_DIM), F32),
                        pltpu.VMEM((n_units, HEAD_DIM, HEAD_DIM), F32),
                        pltpu.VMEM((n_units, GDN_CHUNK, HEAD_DIM), F32),
                        pltpu.VMEM((tile, GDN_WIDTH), F32)],
        compiler_params=_params(("arbitrary", "arbitrary")),
        name="gdn",
    )(gp, gp, gp, gp, sm, out_norm_w.reshape(1, HEAD_DIM))


def _compress_kernel(kc_ref, vc_ref, kpos_ref, kw1_ref, kw2_ref, knw_ref, vpos_ref, vw1_ref, vw2_ref,
                     ko_ref, vo_ref, *, n_cmp):
    ncp = ko_ref.shape[0]
    half = CMP_BLOCK // 2

    def one(x_ref, pos_ref, w1_ref, w2_ref):
        p = jnp.zeros((ncp, HEAD_DIM), F32)
        q = jnp.zeros((ncp, HEAD_DIM), F32)
        for j in range(half):
            a = x_ref[pl.ds(j, ncp, stride=CMP_STRIDE), :]
            p = p + _dot(a + pos_ref[j:j + 1, :], w1_ref[j * HEAD_DIM:(j + 1) * HEAD_DIM, :])
            q = q + _dot(a + pos_ref[half + j:half + j + 1, :],
                         w1_ref[(half + j) * HEAD_DIM:(half + j + 1) * HEAD_DIM, :])
        hid = _silu(p + pltpu.roll(q, ncp - 1, axis=0))
        out = _dot(hid, w2_ref[...])
        valid = lax.broadcasted_iota(jnp.int32, (ncp, HEAD_DIM), 0) < n_cmp
        return jnp.where(valid, out, 0.0)

    k = one(kc_ref, kpos_ref, kw1_ref, kw2_ref)
    ko_ref[...] = jnp.where(lax.broadcasted_iota(jnp.int32, k.shape, 0) < n_cmp,
                            _rms(k, knw_ref[...]), 0.0).astype(BF16)
    vo_ref[...] = one(vc_ref, vpos_ref, vw1_ref, vw2_ref).astype(BF16)


def _compress(cp, kpos, kw1, kw2, knw, vpos, vw1, vw2, bsz, seq):
    ncp = seq // CMP_STRIDE
    n_cmp = (seq - CMP_BLOCK) // CMP_STRIDE + 1
    full = lambda a: pl.BlockSpec(a.shape, lambda b: (0,) * a.ndim)
    knw2 = knw.reshape(1, HEAD_DIM)
    out = pl.BlockSpec((None, ncp, HEAD_DIM), lambda b: (b, 0, 0))
    shp = jax.ShapeDtypeStruct((bsz, ncp, HEAD_DIM), BF16)
    return pl.pallas_call(
        functools.partial(_compress_kernel, n_cmp=n_cmp),
        grid=(bsz,),
        in_specs=[pl.BlockSpec((seq, HEAD_DIM), lambda b: (b, 0)),
                  pl.BlockSpec((seq, HEAD_DIM), lambda b: (b, 1)),
                  full(kpos), full(kw1), full(kw2), full(knw2), full(vpos), full(vw1), full(vw2)],
        out_specs=[out, out],
        out_shape=[shp, shp],
        compiler_params=_params(("arbitrary",)),
        name="compress",
    )(cp, cp, kpos, kw1, kw2, knw2, vpos, vw1, vw2)


def _nsa_kernel(q_ref, kc_ref, vc_ref, ks_ref, vs_ref, kw_ref, vw_ref, sm_ref, ovt_ref, o_ref,
                m_ref, acc_ref, qx_ref, sa_ref, sb_ref, sw_ref, bc_ref, bw_ref, ps_ref, *, tq, tk, n_pick):
    qi = pl.program_id(1)
    t0 = qi * tq
    nh = NSA_HEADS
    ncp = kc_ref.shape[0]
    nsp = ovt_ref.shape[0]
    q = q_ref[...]
    for h in range(nh):
        qx_ref[h * tq:(h + 1) * tq, :HEAD_DIM] = q[:, h * HEAD_DIM:(h + 1) * HEAD_DIM]
    gates = sm_ref[...]

    def gate(h, g):
        c = LANE_GATE + N_GATES * h + g
        return gates[:, c:c + 1]

    t_c = t0 + lax.broadcasted_iota(jnp.int32, (tq, ncp), 0)
    cend = lax.broadcasted_iota(jnp.int32, (tq, ncp), 1) * CMP_STRIDE + (CMP_BLOCK - 1)
    bc_ref[...] = jnp.where(cend <= t_c, 0.0, NEG_INF)
    sees_any = (t0 + lax.broadcasted_iota(jnp.int32, (tq, 1), 0)) >= CMP_BLOCK - 1
    sa_ref[:, :ncp] = _dot_nt(qx_ref[:, :HEAD_DIM], kc_ref[...])
    vcmp = vc_ref[...]
    halves = [slice(i * (tq // 2), (i + 1) * (tq // 2)) for i in range(2)]
    for h in range(nh):
        for qr in halves:
            rows = slice(h * tq + qr.start, h * tq + qr.stop)
            s = sa_ref[rows, :ncp] + bc_ref[qr, :]
            p = jnp.exp2(s - jnp.max(s, axis=-1, keepdims=True))
            l = jnp.sum(p, axis=-1, keepdims=True)
            p = p * jnp.where(sees_any[qr], 1.0 / l, 0.0)
            if h == 0:
                ps_ref[qr, :] = p
            else:
                ps_ref[qr, :] += p
            o_ref[qr, h * HEAD_DIM:(h + 1) * HEAD_DIM] = gates[qr, LANE_GATE + N_GATES * h:
                                                                 LANE_GATE + N_GATES * h + 1] * _dot(p, vcmp)
    ph, pl_ = _split2(ps_ref[...])
    ovt = ovt_ref[...]
    dnt = functools.partial(lax.dot_general, dimension_numbers=(((1,), (1,)), ((), ())),
                            preferred_element_type=F32)
    imp_t = dnt(ovt, ph) + dnt(ovt, pl_)

    wlen = WINDOW + tq
    start = pl.multiple_of(jnp.maximum(t0 - WINDOW, 0), tq)
    vw = vw_ref[pl.ds(start, wlen), :]
    sw_ref[...] = _dot_nt(qx_ref[:, :HEAD_DIM], kw_ref[pl.ds(start, wlen), :])
    kpos_w = start + lax.broadcasted_iota(jnp.int32, (tq, wlen), 1)
    t_w = t0 + lax.broadcasted_iota(jnp.int32, (tq, wlen), 0)
    bw_ref[...] = jnp.where(kpos_w <= t_w, jnp.where(kpos_w > t_w - WINDOW, 0.0, NEG_INF), NEG_INF)
    for h in range(nh):
        for qr in halves:
            s = sw_ref[h * tq + qr.start:h * tq + qr.stop] + bw_ref[qr, :]
            p = jnp.exp2(s - jnp.max(s, axis=-1, keepdims=True))
            pv = _dot(p, vw)
            g_w = gates[qr, LANE_GATE + N_GATES * h + 2:LANE_GATE + N_GATES * h + 3]
            o_ref[qr, h * HEAD_DIM:(h + 1) * HEAD_DIM] += g_w * (pv[:, :HEAD_DIM] / pv[:, HEAD_DIM:])

    jj = lax.broadcasted_iota(jnp.int32, (nsp, tq), 0)
    cur = (t0 + lax.broadcasted_iota(jnp.int32, (nsp, tq), 1)) // SEL_BLOCK
    forced = (jj == 0) | (jj == cur) | (jj == cur - 1)
    valid = jj <= cur
    notsel = jnp.where(valid & forced, 0.0, 1.0)
    work = jnp.where(valid, jnp.where(forced, -BIG, imp_t), -BIG)
    jf = jj.astype(F32)
    for _ in range(n_pick - 3):
        mx = jnp.max(work, axis=0, keepdims=True)
        cand = jnp.where(work == mx, jnp.where(mx > -1e38, jf, float(nsp)), float(nsp))
        first = jnp.min(cand, axis=0, keepdims=True)
        pick = jf == first
        notsel = jnp.where(pick, 0.0, notsel)
        work = jnp.where(pick, -BIG, work)
    notsel_q = notsel.T.astype(BF16)
    for h in range(nh):
        qx_ref[h * tq:(h + 1) * tq, HEAD_DIM:] = notsel_q

    m_ref[...] = jnp.full(m_ref.shape, NEG_INF, F32)
    acc_ref[...] = jnp.zeros(acc_ref.shape, F32)
    n_last = (t0 + tq - 1) // tk

    def scores_into(dst_ref, kt):
        k0 = pl.multiple_of(kt * tk, tk)
        dst_ref[...] = _dot_nt(qx_ref[...], ks_ref[pl.ds(k0, tk), :])

    def step(kt, s_ref, causal):
        if causal:
            kpos = kt * tk + lax.broadcasted_iota(jnp.int32, (tq, tk), 1)
            t_s = t0 + lax.broadcasted_iota(jnp.int32, (tq, tk), 0)
            bias = jnp.where(kpos <= t_s, 0.0, NEG_INF)
        ps, alphas = [], []
        for h in range(nh):
            rows = slice(h * tq, (h + 1) * tq)
            sh = s_ref[rows] + bias if causal else s_ref[rows]
            m_prev = m_ref[rows]
            m_next = jnp.maximum(m_prev, jnp.max(sh, axis=-1, keepdims=True))
            alphas.append(jnp.exp2(m_prev - m_next))
            ps.append(jnp.exp2(sh - jnp.concatenate([m_next] * (tk // LANES), axis=1)).astype(BF16))
            m_ref[rows] = m_next
        k0 = pl.multiple_of(kt * tk, tk)
        pv = jnp.dot(jnp.concatenate(ps, axis=0), vs_ref[pl.ds(k0, tk), :],
                     preferred_element_type=F32)
        alpha = jnp.concatenate(alphas, axis=0)
        acc_ref[...] = jnp.concatenate([alpha, alpha], axis=1) * acc_ref[...] + pv

    scores_into(sa_ref, 0)

    def pair(j, carry):
        scores_into(sb_ref, 2 * j + 1)
        step(2 * j, sa_ref, False)
        scores_into(sa_ref, 2 * j + 2)
        step(2 * j + 1, sb_ref, False)
        return carry

    lax.fori_loop(0, n_last // 2, pair, 0)

    @pl.when(n_last % 2 == 1)
    def _():
        scores_into(sb_ref, n_last)
        step(n_last - 1, sa_ref, False)
        step(n_last, sb_ref, True)

    @pl.when(n_last % 2 == 0)
    def _():
        step(n_last, sa_ref, True)

    for h in range(nh):
        rows = slice(h * tq, (h + 1) * tq)
        l_s = acc_ref[rows, HEAD_DIM:]
        o_s = acc_ref[rows, :HEAD_DIM] * jnp.where(l_s > 0.0, 1.0 / l_s, 0.0)
        o_ref[:, h * HEAD_DIM:(h + 1) * HEAD_DIM] += gate(h, 1) * o_s


def _nsa(qn, kcmp, vcmp, ksl, vsl, kwn, vwn, sm, overlap, bsz, seq):
    tq, tk = 256, 512
    per_b = seq // tq
    ncp = kcmp.shape[1]
    nsp = overlap.shape[0]
    n_pick = min(N_SELECT, seq // SEL_BLOCK)
    assert n_pick == N_SELECT and seq % tk == 0 and seq >= WINDOW + tq and nsp == LANES and ncp <= tk
    ext = HEAD_DIM + LANES
    seq_kv = lambda w: pl.BlockSpec((seq, w), lambda b, i: (b, 0))
    cmp_kv = lambda: pl.BlockSpec((None, ncp, HEAD_DIM), lambda b, i: (b, 0, 0))
    return pl.pallas_call(
        functools.partial(_nsa_kernel, tq=tq, tk=tk, n_pick=n_pick),
        grid=(bsz, per_b),
        in_specs=[pl.BlockSpec((tq, NSA_WIDTH), lambda b, i: (b * per_b + i, 0)),
                  cmp_kv(), cmp_kv(), seq_kv(ext), seq_kv(ext), seq_kv(HEAD_DIM), seq_kv(ext),
                  pl.BlockSpec((tq, LANES), lambda b, i: (b * per_b + i, 0)),
                  pl.BlockSpec((nsp, ncp), lambda b, i: (0, 0))],
        out_specs=pl.BlockSpec((tq, NSA_WIDTH), lambda b, i: (b * per_b + i, 0)),
        out_shape=jax.ShapeDtypeStruct((bsz * seq, NSA_WIDTH), F32),
        scratch_shapes=[pltpu.VMEM((NSA_HEADS * tq, LANES), F32),
                        pltpu.VMEM((NSA_HEADS * tq, ext), F32),
                        pltpu.VMEM((NSA_HEADS * tq, ext), BF16),
                        pltpu.VMEM((NSA_HEADS * tq, tk), F32),
                        pltpu.VMEM((NSA_HEADS * tq, tk), F32),
                        pltpu.VMEM((NSA_HEADS * tq, WINDOW + tq), F32),
                        pltpu.VMEM((tq, ncp), F32),
                        pltpu.VMEM((tq, WINDOW + tq), F32),
                        pltpu.VMEM((tq, ncp), F32)],
        compiler_params=_params(("arbitrary", "arbitrary")),
        name="nsa",
    )(qn, kcmp, vcmp, ksl, vsl, kwn, vwn, sm, overlap)


def _overlap_matrix(seq):
    n_cmp = (seq - CMP_BLOCK) // CMP_STRIDE + 1
    n_sel = seq // SEL_BLOCK
    ncp = seq // CMP_STRIDE
    nsp = -(-n_sel // LANES) * LANES
    cs = np.arange(n_cmp) * CMP_STRIDE
    ss = np.arange(n_sel) * SEL_BLOCK
    ov = np.minimum(cs[:, None] + CMP_BLOCK, ss[None, :] + SEL_BLOCK) - np.maximum(cs[:, None], ss[None, :])
    out = np.zeros((nsp, ncp), np.float32)
    out[:n_sel, :n_cmp] = (np.clip(ov, 0, None).astype(np.float32) / CMP_BLOCK).T
    return jnp.asarray(out, dtype=BF16)


def _ffn_kernel(x_ref, og_ref, on_ref, mod_ref, wo_ref, nw_ref, wu_ref, cw_ref, cb_ref, wd_ref, o_ref,
                up_ref, act_ref, *, tm, d_ff, cw):
    head = SUBLANES

    @pl.when(pl.program_id(1) == 0)
    def _():
        up_ref[:, :head, :] = jnp.zeros((2, head, d_ff), F32)

    m = mod_ref[0]
    mixed = _dot(og_ref[...], wo_ref[:GDN_WIDTH, :]) + _dot(on_ref[...], wo_ref[GDN_WIDTH:, :])
    x1 = x_ref[...] + m[2:3] * mixed
    h2 = _rms_mod(x1, nw_ref[...], m[4:5], m[3:4]).astype(BF16)
    for c in range(d_ff // cw):
        cols = slice(c * cw, (c + 1) * cw)
        halves = []
        for g in range(2):
            wcols = slice(g * d_ff + c * cw, g * d_ff + (c + 1) * cw)
            up_ref[g, head:, cols] = jnp.dot(h2, wu_ref[:, wcols], preferred_element_type=F32)
            z = cb_ref[:, wcols]
            for j in range(FFN_CONV):
                tap = up_ref[g, pl.ds(head - (FFN_CONV - 1) + j, tm), cols]
                z = z + cw_ref[j:j + 1, wcols] * tap
            up_ref[g, :head, cols] = up_ref[g, tm:tm + head, cols]
            halves.append(z)
        act_ref[:, cols] = (_silu(halves[0]) * halves[1]).astype(BF16)
    y = jnp.dot(act_ref[...], wd_ref[...], preferred_element_type=F32)
    o_ref[...] = x1 + m[5:6] * y


def _ffn(x2, o_g, o_n, mod3, w_out_b, norm2_w, w_up_b, conv_w, conv_b, w_down_b, bsz, seq):
    t, d = x2.shape
    d_ff = w_down_b.shape[0]
    tm, cw = 512, 256
    assert d_ff % cw == 0
    per_b = seq // tm
    full = lambda a: pl.BlockSpec(a.shape, lambda b, s: (0,) * a.ndim)
    conv_b2 = conv_b.reshape(1, -1)
    norm2_w2 = norm2_w.reshape(1, d)
    return pl.pallas_call(
        functools.partial(_ffn_kernel, tm=tm, d_ff=d_ff, cw=cw),
        grid=(bsz, per_b),
        in_specs=[pl.BlockSpec((tm, d), lambda b, s: (b * per_b + s, 0)),
                  pl.BlockSpec((tm, GDN_WIDTH), lambda b, s: (b * per_b + s, 0)),
                  pl.BlockSpec((tm, NSA_WIDTH), lambda b, s: (b * per_b + s, 0)),
                  pl.BlockSpec((1, 6, d), lambda b, s: (b, 0, 0)),
                  full(w_out_b), full(norm2_w2), full(w_up_b), full(conv_w), full(conv_b2),
                  full(w_down_b)],
        out_specs=pl.BlockSpec((tm, d), lambda b, s: (b * per_b + s, 0)),
        out_shape=jax.ShapeDtypeStruct((t, d), F32),
        scratch_shapes=[pltpu.VMEM((2, SUBLANES + tm, d_ff), F32),
                        pltpu.VMEM((tm, d_ff), BF16)],
        compiler_params=_params(("arbitrary", "arbitrary")),
        name="ffn",
    )(x2, o_g, o_n, mod3, w_out_b, norm2_w2, w_up_b, conv_w, conv_b2, w_down_b)


def _pad_lanes(a, n):
    return jnp.pad(a, ((0, 0), (0, n - a.shape[1])))


def _layer(x, c, ada_w, ada_b, norm1_w, w_in, gdn_conv_w, gdn_A_log, gdn_dt_bias, gdn_out_norm_w,
           nsa_q_norm_w, nsa_k_norm_cmp, nsa_k_norm_slc, nsa_k_norm_win, cmp_k_pos, cmp_k_w1, cmp_k_w2,
           cmp_v_pos, cmp_v_w1, cmp_v_w2, w_out, norm2_w, ffn_w_up, ffn_conv_w, ffn_conv_b, ffn_w_down):
    bsz, seq, d = x.shape
    x2 = x.reshape(bsz * seq, d)
    mod3 = _ada(c, ada_w, ada_b).reshape(bsz, 6, d)

    gp, sm, cp, qn, ksl, vsl, kwn, vwn = _inproj(
        x2, mod3, norm1_w, w_in, gdn_conv_w, _pad_lanes(gdn_A_log.reshape(1, -1), LANES),
        _pad_lanes(gdn_dt_bias.reshape(1, -1), LANES), nsa_q_norm_w, nsa_k_norm_slc, nsa_k_norm_win, seq)

    o_g = _gdn(gp, sm, gdn_out_norm_w, bsz, seq)

    kcmp, vcmp = _compress(cp, cmp_k_pos, cmp_k_w1.astype(BF16), cmp_k_w2.astype(BF16), nsa_k_norm_cmp,
                           cmp_v_pos, cmp_v_w1.astype(BF16), cmp_v_w2.astype(BF16), bsz, seq)
    o_n = _nsa(qn, kcmp, vcmp, ksl, vsl, kwn, vwn, sm, _overlap_matrix(seq), bsz, seq)

    out = _ffn(x2, o_g, o_n, mod3, w_out.astype(BF16), norm2_w, ffn_w_up.astype(BF16), ffn_conv_w,
               ffn_conv_b, ffn_w_down.astype(BF16), bsz, seq)
    return out.reshape(bsz, seq, d)


def kernel(x, c, ada_w, ada_b, norm1_w, w_in, gdn_conv_w, gdn_A_log, gdn_dt_bias, gdn_out_norm_w, nsa_q_norm_w, nsa_k_norm_cmp, nsa_k_norm_slc, nsa_k_norm_win, cmp_k_pos, cmp_k_w1, cmp_k_w2, cmp_v_pos, cmp_v_w1, cmp_v_w2, w_out, norm2_w, ffn_w_up, ffn_conv_w, ffn_conv_b, ffn_w_down):
    for l in range(ada_w.shape[0]):
        x = _layer(x, c, ada_w[l], ada_b[l], norm1_w[l], w_in[l], gdn_conv_w[l], gdn_A_log[l],
                   gdn_dt_bias[l], gdn_out_norm_w[l], nsa_q_norm_w[l], nsa_k_norm_cmp[l],
                   nsa_k_norm_slc[l], nsa_k_norm_win[l], cmp_k_pos[l], cmp_k_w1[l], cmp_k_w2[l],
                   cmp_v_pos[l], cmp_v_w1[l], cmp_v_w2[l], w_out[l], norm2_w[l], ffn_w_up[l],
                   ffn_conv_w[l], ffn_conv_b[l], ffn_w_down[l])
    return x
```
